```python
import math
import jax, jax.numpy as jnp
from jax import lax
import numpy as np

D_MODEL = 2048
BATCH = 4
SEQ = 2048
DEPTH = 4

GRID_W = 64
CTX_LEN = 256
N_MIXERS = 4
DEEPNORM_ALPHA = (2 * DEPTH) ** 0.25
DEEPNORM_BETA = (8 * DEPTH) ** -0.25
LN_EPS = 1e-5
ADA_CHUNKS = 6
D_FF = 4 * D_MODEL
BLOCK = 128
ROPE_BASE = 10000.0
NEG_INF = -1e30

LRU_WIDTH = D_MODEL
LRU_BLOCK_W = 256
LRU_BLOCKS = LRU_WIDTH // LRU_BLOCK_W
LRU_CONV_W = 4
LRU_C = 8.0

DIF_HEADS = 16
DIF_DH = 64
DIF_DV = 2 * DIF_DH

RET_HEADS = 8
RET_DK = D_MODEL // RET_HEADS
RET_DV = 2 * RET_DK
RET_CHUNK = 128

SWA_HEADS = 32
SWA_KV_HEADS = 8
SWA_GROUP = SWA_HEADS // SWA_KV_HEADS
SWA_DH = 64
WINDOW = 128

F32 = jnp.float32

kernel_name = 'hybrid_interleaved_flow_block'


def layer_norm(x, g, b):
    xf = x.astype(F32)
    mu = jnp.mean(xf, -1, keepdims=True)
    var = jnp.mean(jnp.square(xf - mu), -1, keepdims=True)
    y = (xf - mu) * lax.rsqrt(var + LN_EPS)
    return (y * g.astype(F32) + b.astype(F32)).astype(x.dtype)


def rms_norm(x, g=None):
    xf = x.astype(F32)
    y = xf * lax.rsqrt(jnp.mean(jnp.square(xf), -1, keepdims=True) + LN_EPS)
    if g is not None:
        y = y * g.astype(F32)
    return y.astype(x.dtype)


def axial_rope_tables(rows, head_dim):
    row = jnp.repeat(jnp.arange(rows, dtype=F32), GRID_W)
    col = jnp.tile(jnp.arange(GRID_W, dtype=F32), rows)
    half = head_dim // 2
    inv_freq = ROPE_BASE ** (-jnp.arange(0, half, 2, dtype=F32) / half)
    ang_r = row[:, None] * inv_freq[None, :]
    ang_c = col[:, None] * inv_freq[None, :]
    ang = jnp.concatenate([ang_r, ang_r, ang_c, ang_c], -1)
    return jnp.cos(ang), jnp.sin(ang)


def apply_rope(x, cos, sin):
    shape = (1, cos.shape[0]) + (1,) * (x.ndim - 3) + (cos.shape[1],)
    x1, x2, x3, x4 = jnp.split(x, 4, axis=-1)
    rot = jnp.concatenate([-x2, x1, -x4, x3], -1)
    return (x * cos.reshape(shape) + rot * sin.reshape(shape)).astype(x.dtype)


def squared_relu_mlp(u, w1, w2):
    return jnp.square(jax.nn.relu(u @ w1)) @ w2


def depthwise_conv_centred(x, w, b):
    k = w.shape[0]
    left = k // 2
    right = k - 1 - left
    y = lax.conv_general_dilated(x, w[:, None, :], (1,), [(left, right)],
                                 dimension_numbers=('NWC', 'WIO', 'NWC'),
                                 feature_group_count=x.shape[-1])
    return y + b


def block_diag(x, w):
    nb, bw, _ = w.shape
    xb = x.reshape(x.shape[:-1] + (nb, bw))
    return jnp.einsum('bsnc,ncd->bsnd', xb, w).reshape(x.shape)


def rglru_coeffs(xc, w_gate, b_gate, lam):
    r = jax.nn.sigmoid((block_diag(xc, w_gate[0]) + b_gate[0]).astype(F32))
    i = jax.nn.sigmoid((block_diag(xc, w_gate[1]) + b_gate[1]).astype(F32))
    log_a = -LRU_C * r * jax.nn.softplus(-lam.astype(F32))
    a = jnp.exp(log_a)
    b = jnp.sqrt(-jnp.expm1(2.0 * log_a)) * (i * xc.astype(F32))
    return a, b


def linear_scan(a, b, h0=None):
    def combine(left, right):
        return left[0] * right[0], right[0] * left[1] + right[1]
    a_cum, h = lax.associative_scan(combine, (a, b), axis=1)
    if h0 is not None:
        h = h + a_cum * h0[:, None, :]
    return h


def rglru_direction(x_c, x_l, w_gate, b_gate, lam, reverse):
    fl = (lambda t: jnp.flip(t, 1)) if reverse else (lambda t: t)
    a_c, b_c = rglru_coeffs(fl(x_c), w_gate, b_gate, lam)
    h_c = linear_scan(a_c, b_c)
    a_l, b_l = rglru_coeffs(fl(x_l), w_gate, b_gate, lam)
    h_l = linear_scan(a_l, b_l, h_c[:, -1])
    return fl(h_c), fl(h_l)


def rglru_mixer(u_lat, u_ctx, w_in, conv_w, conv_b, gate_w, gate_b, lam, w_out, ctx_out):
    def branches(u):
        gate_br, rec_br = jnp.split(u @ w_in, 2, -1)
        return jax.nn.gelu(gate_br), depthwise_conv_centred(rec_br, conv_w, conv_b)
    gate_c, x_c = branches(u_ctx)
    gate_l, x_l = branches(u_lat)
    hc_f, hl_f = rglru_direction(x_c, x_l, gate_w[0], gate_b[0], lam[0], False)
    hc_b, hl_b = rglru_direction(x_c, x_l, gate_w[1], gate_b[1], lam[1], True)
    y_lat = (gate_l * (hl_f + hl_b).astype(gate_l.dtype)) @ w_out
    if not ctx_out:
        return y_lat, None
    y_ctx = (gate_c * (hc_f + hc_b).astype(gate_c.dtype)) @ w_out
    return y_lat, y_ctx


def diff_attention_mixer(u_lat, u_ctx, w_qkv, lam, subln, w_out, cos, sin, layer_idx, ctx_out):
    B, S, _ = u_lat.shape
    scale = DIF_DH ** -0.5
    lam_init = 0.8 - 0.6 * math.exp(-0.3 * layer_idx)
    lam_full = (jnp.exp(jnp.sum(lam[0] * lam[1]).astype(F32))
                - jnp.exp(jnp.sum(lam[2] * lam[3]).astype(F32)) + lam_init)

    def proj(u):
        q, k, v = jnp.split(u @ w_qkv, 3, -1)
        n = u.shape[:2]
        return (q.reshape(n + (DIF_HEADS, 2, DIF_DH)), k.reshape(n + (DIF_HEADS, 2, DIF_DH)),
                v.reshape(n + (DIF_HEADS, DIF_DV)))

    q_l, k_l, v_l = proj(u_lat)
    q_c, k_c, v_c = proj(u_ctx)
    q_l = apply_rope(q_l, cos, sin)
    k_l = apply_rope(k_l, cos, sin)
    k_all = jnp.concatenate([k_l, k_c], 1)
    v_all = jnp.concatenate([v_l, v_c], 1)

    def attend(q, k, v):
        s = jnp.einsum('bqhmd,bkhmd->bhmqk', q, k).astype(F32) * scale
        p = jax.nn.softmax(s, -1)
        attn = p[:, :, 0] - lam_full * p[:, :, 1]
        return jnp.einsum('bhqk,bkhe->bqhe', attn.astype(v.dtype), v)

    def finish(o):
        o = rms_norm(o, subln) * (1.0 - lam_init)
        return o.reshape(o.shape[:2] + (DIF_HEADS * DIF_DV,)) @ w_out

    nb = S // BLOCK
    q_blocks = jnp.moveaxis(q_l.reshape(B, nb, BLOCK, DIF_HEADS, 2, DIF_DH), 1, 0)
    o = lax.map(lambda qb: attend(qb, k_all, v_all), q_blocks)
    y_lat = finish(jnp.moveaxis(o, 0, 1).reshape(B, S, DIF_HEADS, DIF_DV))
    if not ctx_out:
        return y_lat, None
    y_ctx = finish(attend(q_c, k_c, v_c))
    return y_lat, y_ctx


def retention_log_decays():
    log_gf = jnp.log1p(-jnp.exp2(-5.0 - jnp.arange(RET_HEADS, dtype=F32)))
    return log_gf, log_gf[::-1]


def retention_intra(q, k, v, log_gf, log_gb):
    C = q.shape[2]
    idx = jnp.arange(C)
    rel = (idx[:, None] - idx[None, :]).astype(F32)
    lg = jnp.where(rel[None] >= 0, log_gf[:, None, None], log_gb[:, None, None])
    decay = jnp.exp(jnp.abs(rel)[None] * lg)
    s = jnp.einsum('bnihd,bnjhd->bnhij', q, k) * decay
    return jnp.einsum('bnhij,bnjhe->bnihe', s, v)


def retention_inter(q, k, v, log_g, state0=None):
    B, N, C, H, dk = q.shape
    dv = v.shape[-1]
    pos = jnp.arange(C, dtype=F32)[:, None]
    q_dec = jnp.exp((pos + 1.0) * log_g)
    k_dec = jnp.exp((C - 1.0 - pos) * log_g)
    c_dec = jnp.exp(C * log_g)[:, None, None]
    if state0 is None:
        state0 = jnp.zeros((B, H, dk, dv), F32)

    def step(R, xs):
        qc, kc, vc = xs
        out = jnp.einsum('bihd,bhde->bihe', qc * q_dec[:, :, None], R)
        R = c_dec * R + jnp.einsum('bjhd,bjhe->bhde', kc * k_dec[:, :, None], vc)
        return R, out

    R, out = lax.scan(step, state0, (jnp.moveaxis(q, 1, 0), jnp.moveaxis(k, 1, 0), jnp.moveaxis(v, 1, 0)))
    return jnp.moveaxis(out, 0, 1), R


def retention_mixer(u_lat, u_ctx, w_qkvg, w_out, ctx_out):
    log_gf, log_gb = retention_log_decays()

    def proj(u):
        q, k, v, g = jnp.split(u @ w_qkvg, [D_MODEL, 2 * D_MODEL, 2 * D_MODEL + RET_HEADS * RET_DV], -1)
        n = u.shape[:2]
        q = q.reshape(n + (RET_HEADS, RET_DK)).astype(F32) * (RET_DK ** -0.5)
        k = k.reshape(n + (RET_HEADS, RET_DK)).astype(F32)
        v = v.reshape(n + (RET_HEADS, RET_DV)).astype(F32)
        return q, k, v, g

    def chunks(t):
        return t.reshape((t.shape[0], t.shape[1] // RET_CHUNK, RET_CHUNK) + t.shape[2:])

    def unchunk(t):
        return t.reshape((t.shape[0], t.shape[1] * t.shape[2]) + t.shape[3:])

    def rev(t):
        return jnp.flip(t, 1)

    def finish(q, k, v, g, inter_f, inter_b):
        B, S = q.shape[:2]
        o = unchunk(retention_intra(chunks(q), chunks(k), chunks(v), log_gf, log_gb) + inter_f) + rev(unchunk(inter_b))
        o = rms_norm(o).reshape(B, S, RET_HEADS * RET_DV)
        return (jax.nn.silu(g) * o.astype(g.dtype)) @ w_out

    q_c, k_c, v_c, g_c = proj(u_ctx)
    q_l, k_l, v_l, g_l = proj(u_lat)
    inter_cf, s_f = retention_inter(chunks(q_c), chunks(k_c), chunks(v_c), log_gf)
    inter_cb, s_b = retention_inter(chunks(rev(q_c)), chunks(rev(k_c)), chunks(rev(v_c)), log_gb)
    inter_lf, _ = retention_inter(chunks(q_l), chunks(k_l), chunks(v_l), log_gf, s_f)
    inter_lb, _ = retention_inter(chunks(rev(q_l)), chunks(rev(k_l)), chunks(rev(v_l)), log_gb, s_b)
    y_lat = finish(q_l, k_l, v_l, g_l, inter_lf, inter_lb)
    if not ctx_out:
        return y_lat, None
    y_ctx = finish(q_c, k_c, v_c, g_c, inter_cf, inter_cb)
    return y_lat, y_ctx


def window_gqa_mixer(u_lat, u_ctx, w_qkv, sink, w_out, cos, sin, ctx_out):
    B, S, _ = u_lat.shape
    L = u_ctx.shape[1]
    scale = SWA_DH ** -0.5

    def proj(u):
        q, k, v = jnp.split(u @ w_qkv, [SWA_HEADS * SWA_DH, (SWA_HEADS + SWA_KV_HEADS) * SWA_DH], -1)
        n = u.shape[:2]
        return (q.reshape(n + (SWA_KV_HEADS, SWA_GROUP, SWA_DH)), k.reshape(n + (SWA_KV_HEADS, SWA_DH)),
                v.reshape(n + (SWA_KV_HEADS, SWA_DH)))

    q_l, k_l, v_l = proj(u_lat)
    q_c, k_c, v_c = proj(u_ctx)
    q_l = apply_rope(q_l, cos, sin)
    k_l = apply_rope(k_l, cos, sin)
    sink_hg = sink.astype(F32).reshape(SWA_KV_HEADS, SWA_GROUP)

    def softmax_with_sink(s):
        col = jnp.broadcast_to(sink_hg[None, :, :, None, None], s.shape[:-1] + (1,))
        return jax.nn.softmax(jnp.concatenate([s, col], -1), -1)[..., :-1]

    span = BLOCK + 2 * WINDOW
    k_pad = jnp.pad(k_l, ((0, 0), (WINDOW, WINDOW), (0, 0), (0, 0)))
    v_pad = jnp.pad(v_l, ((0, 0), (WINDOW, WINDOW), (0, 0), (0, 0)))
    rel = (jnp.arange(span)[None, :] - WINDOW) - jnp.arange(BLOCK)[:, None]
    band = jnp.abs(rel) <= WINDOW

    def block(n):
        start = n * BLOCK
        qb = lax.dynamic_slice_in_dim(q_l, start, BLOCK, 1)
        kb = lax.dynamic_slice_in_dim(k_pad, start, span, 1)
        vb = lax.dynamic_slice_in_dim(v_pad, start, span, 1)
        kpos = start - WINDOW + jnp.arange(span)
        valid = band & ((kpos >= 0) & (kpos < S))[None, :]
        s_win = jnp.where(valid, jnp.einsum('bqhgd,bshd->bhgqs', qb, kb).astype(F32) * scale, NEG_INF)
        s_ctx = jnp.einsum('bqhgd,bchd->bhgqc', qb, k_c).astype(F32) * scale
        p = softmax_with_sink(jnp.concatenate([s_win, s_ctx], -1)).astype(vb.dtype)
        return (jnp.einsum('bhgqs,bshd->bqhgd', p[..., :span], vb)
                + jnp.einsum('bhgqc,bchd->bqhgd', p[..., span:], v_c))

    o = lax.map(block, jnp.arange(S // BLOCK))
    y_lat = jnp.moveaxis(o, 0, 1).reshape(B, S, SWA_HEADS * SWA_DH) @ w_out
    if not ctx_out:
        return y_lat, None
    s_cc = jnp.einsum('bqhgd,bchd->bhgqc', q_c, k_c).astype(F32) * scale
    o_c = jnp.einsum('bhgqc,bchd->bqhgd', softmax_with_sink(s_cc).astype(v_c.dtype), v_c)
    y_ctx = o_c.reshape(B, L, SWA_HEADS * SWA_DH) @ w_out
    return y_lat, y_ctx


def setup_inputs(seed: int = 0) -> dict:
    key = jax.random.key(seed)
    ks = iter(jax.random.split(key, 32))
    D = D_MODEL

    def nrm(shape, scale):
        return jax.random.normal(next(ks), shape, F32) * scale

    def uses(kind):
        return len(range(kind, DEPTH, N_MIXERS))

    def lru_lambda_init(shape):
        a = jax.random.uniform(next(ks), shape, F32, 0.9, 0.999) ** (1.0 / LRU_C)
        return jnp.log(a) - jnp.log1p(-a)

    nA, nB, nC, nD = uses(0), uses(1), uses(2), uses(3)
    beta = DEEPNORM_BETA
    inputs = {}
    inputs['x'] = nrm((BATCH, SEQ, D), 1.0)
    inputs['c'] = nrm((BATCH, D), 1.0)
    inputs['ctx'] = nrm((BATCH, CTX_LEN, D), 1.0)
    inputs['c_ctx'] = nrm((D,), 1.0)
    inputs['ada_w'] = nrm((DEPTH, D, ADA_CHUNKS * D), 0.5 * D ** -0.5)
    inputs['ada_b'] = nrm((DEPTH, ADA_CHUNKS * D), 0.02)
    inputs['ln_g'] = 1.0 + nrm((DEPTH, 2, D), 0.02)
    inputs['ln_b'] = nrm((DEPTH, 2, D), 0.02)
    inputs['mlp_w1'] = nrm((DEPTH, D, D_FF), D ** -0.5)
    inputs['mlp_w2'] = nrm((DEPTH, D_FF, D), beta * D_FF ** -0.5)
    inputs['lru_w_in'] = nrm((nA, D, 2 * LRU_WIDTH), D ** -0.5)
    inputs['lru_conv_w'] = nrm((nA, LRU_CONV_W, LRU_WIDTH), LRU_CONV_W ** -0.5)
    inputs['lru_conv_b'] = nrm((nA, LRU_WIDTH), 0.02)
    inputs['lru_gate_w'] = nrm((nA, 2, 2, LRU_BLOCKS, LRU_BLOCK_W, LRU_BLOCK_W), LRU_BLOCK_W ** -0.5)
    inputs['lru_gate_b'] = nrm((nA, 2, 2, LRU_WIDTH), 0.02)
    inputs['lru_lambda'] = lru_lambda_init((nA, 2, LRU_WIDTH))
    inputs['lru_w_out'] = nrm((nA, LRU_WIDTH, D), beta * LRU_WIDTH ** -0.5)
    inputs['dif_w_qkv'] = nrm((nB, D, 3 * DIF_HEADS * DIF_DV), D ** -0.5)
    inputs['dif_lambda'] = nrm((nB, 4, DIF_DH), 0.1)
    inputs['dif_subln'] = 1.0 + nrm((nB, DIF_DV), 0.02)
    inputs['dif_w_out'] = nrm((nB, DIF_HEADS * DIF_DV, D), beta * (DIF_HEADS * DIF_DV) ** -0.5)
    inputs['ret_w_qkvg'] = nrm((nC, D, 2 * D + 2 * RET_HEADS * RET_DV), D ** -0.5)
    inputs['ret_w_out'] = nrm((nC, RET_HEADS * RET_DV, D), beta * (RET_HEADS * RET_DV) ** -0.5)
    inputs['swa_w_qkv'] = nrm((nD, D, (SWA_HEADS + 2 * SWA_KV_HEADS) * SWA_DH), D ** -0.5)
    inputs['swa_sink'] = nrm((nD, SWA_HEADS), 0.5)
    inputs['swa_w_out'] = nrm((nD, SWA_HEADS * SWA_DH, D), beta * (SWA_HEADS * SWA_DH) ** -0.5)
    return inputs


def reference(x, c, ctx, c_ctx, ada_w, ada_b, ln_g, ln_b, mlp_w1, mlp_w2,
              lru_w_in, lru_conv_w, lru_conv_b, lru_gate_w, lru_gate_b, lru_lambda, lru_w_out,
              dif_w_qkv, dif_lambda, dif_subln, dif_w_out,
              ret_w_qkvg, ret_w_out,
              swa_w_qkv, swa_sink, swa_w_out):
    ROWS = x.shape[1] // GRID_W
    cos_d, sin_d = axial_rope_tables(ROWS, DIF_DH)
    cos_w, sin_w = axial_rope_tables(ROWS, SWA_DH)
    h = x
    hc = ctx
    for i in range(DEPTH):
        kind, j = i % N_MIXERS, i // N_MIXERS
        ctx_out = i < DEPTH - 1
        m = jax.nn.silu(c) @ ada_w[i] + ada_b[i]
        mc = jax.nn.silu(c_ctx) @ ada_w[i] + ada_b[i]
        sh1, sc1, g1, sh2, sc2, g2 = jnp.split(m[:, None, :], ADA_CHUNKS, -1)
        csh1, csc1, cg1, csh2, csc2, cg2 = jnp.split(mc, ADA_CHUNKS, -1)
        u_l = h * (1 + sc1) + sh1
        u_c = hc * (1 + csc1) + csh1
        if kind == 0:
            y_l, y_c = rglru_mixer(u_l, u_c, lru_w_in[j], lru_conv_w[j], lru_conv_b[j], lru_gate_w[j],
                                   lru_gate_b[j], lru_lambda[j], lru_w_out[j], ctx_out)
        elif kind == 1:
            y_l, y_c = diff_attention_mixer(u_l, u_c, dif_w_qkv[j], dif_lambda[j], dif_subln[j], dif_w_out[j],
                                            cos_d, sin_d, i, ctx_out)
        elif kind == 2:
            y_l, y_c = retention_mixer(u_l, u_c, ret_w_qkvg[j], ret_w_out[j], ctx_out)
        else:
            y_l, y_c = window_gqa_mixer(u_l, u_c, swa_w_qkv[j], swa_sink[j], swa_w_out[j], cos_w, sin_w, ctx_out)
        h = layer_norm(DEEPNORM_ALPHA * h + g1 * y_l, ln_g[i, 0], ln_b[i, 0])
        f_l = squared_relu_mlp(h * (1 + sc2) + sh2, mlp_w1[i], mlp_w2[i])
        h = layer_norm(DEEPNORM_ALPHA * h + g2 * f_l, ln_g[i, 1], ln_b[i, 1])
        if ctx_out:
            hc = layer_norm(DEEPNORM_ALPHA * hc + cg1 * y_c, ln_g[i, 0], ln_b[i, 0])
            f_c = squared_relu_mlp(hc * (1 + csc2) + csh2, mlp_w1[i], mlp_w2[i])
            hc = layer_norm(DEEPNORM_ALPHA * hc + cg2 * f_c, ln_g[i, 1], ln_b[i, 1])
    return h
```

```python
import functools
import math

import jax
import jax.numpy as jnp
from jax import lax
from jax.experimental import pallas as pl
from jax.experimental.pallas import tpu as pltpu

F32 = jnp.float32
BF16 = jnp.bfloat16

D = 2048
B = 4
S = 2048
L = 256
DEPTH = 4
GRID_W = 64
TL = B * S
TC = B * L
T = TL + TC
NSEG = 8
ADA = 6
D_FF = 4 * D
ALPHA = (2 * DEPTH) ** 0.25
LN_EPS = 1e-5
ROPE_BASE = 10000.0
NEG_INF = -1e30

LRU_BW = 256
LRU_NB = D // LRU_BW
LRU_C = 8.0
DIF_H = 16
DIF_DV = 128
RET_H = 8
RET_DK = 256
RET_DV = 512
RET_CH = 128
SWA_H = 32
SWA_KV = 8
SWA_DH = 64
WINDOW = 128
QB = 128

VMEM_LIMIT = 52 * 1024 * 1024
LANES = 128

TM = 1024
TM_LN = 512


def _cp(*sem):
    return pltpu.CompilerParams(dimension_semantics=sem, vmem_limit_bytes=VMEM_LIMIT)


def _seg(i, tm):
    return jnp.minimum((i * tm) // S, B)


def _ada_kernel(c_ref, w_ref, b_ref, o_ref):
    c = c_ref[...]
    a = (c * jax.nn.sigmoid(c)).astype(BF16)
    o_ref[0] = jnp.dot(a, w_ref[0].astype(BF16), preferred_element_type=F32) + b_ref[0]


def _ada(cvec, ada_w, ada_b):
    tn = 1024
    out = pl.pallas_call(
        _ada_kernel,
        grid=(DEPTH, ADA * D // tn),
        in_specs=[
            pl.BlockSpec((NSEG, D), lambda l, j: (0, 0)),
            pl.BlockSpec((1, D, tn), lambda l, j: (l, 0, j)),
            pl.BlockSpec((1, 1, tn), lambda l, j: (l, 0, j)),
        ],
        out_specs=pl.BlockSpec((1, NSEG, tn), lambda l, j: (l, 0, j)),
        out_shape=jax.ShapeDtypeStruct((DEPTH, NSEG, ADA * D), F32),
        compiler_params=_cp("arbitrary", "arbitrary"),
        name="ada",
    )(cvec, ada_w, ada_b.reshape(DEPTH, 1, ADA * D))
    return out.reshape(DEPTH * NSEG * ADA, 1, D)


def _mod_spec(layer, chunk, tm, grid_pos):
    base = layer * NSEG * ADA + chunk

    def index_map(*g):
        return (base + _seg(g[grid_pos], tm) * ADA, 0, 0)

    return pl.BlockSpec((1, 1, D), index_map)


def _modulate_kernel(x_ref, sc_ref, sh_ref, u_ref):
    u_ref[...] = (x_ref[...] * (1.0 + sc_ref[0]) + sh_ref[0]).astype(BF16)


def _modulate(x, mods, layer):
    tm = TM
    return pl.pallas_call(
        _modulate_kernel,
        grid=(T // tm,),
        in_specs=[
            pl.BlockSpec((tm, D), lambda i: (i, 0)),
            _mod_spec(layer, 1, tm, 0),
            _mod_spec(layer, 0, tm, 0),
        ],
        out_specs=pl.BlockSpec((tm, D), lambda i: (i, 0)),
        out_shape=jax.ShapeDtypeStruct((T, D), BF16),
        compiler_params=_cp("arbitrary"),
        name="modulate",
    )(x, mods, mods)


def _cast_kernel(w_ref, o_ref):
    o_ref[...] = w_ref[...].astype(BF16)


def _cast_bf16(w):
    r, c = w.shape
    tr = 512
    return pl.pallas_call(
        _cast_kernel,
        grid=(r // tr,),
        in_specs=[pl.BlockSpec((tr, c), lambda i: (i, 0))],
        out_specs=pl.BlockSpec((tr, c), lambda i: (i, 0)),
        out_shape=jax.ShapeDtypeStruct((r, c), BF16),
        compiler_params=_cp("arbitrary"),
        name="cast",
    )(w)


def _rope_tables(tm):
    rows = S // GRID_W
    row = jnp.repeat(jnp.arange(rows, dtype=F32), GRID_W)
    col = jnp.tile(jnp.arange(GRID_W, dtype=F32), rows)
    half = SWA_DH // 2
    inv_freq = ROPE_BASE ** (-jnp.arange(0, half, 2, dtype=F32) / half)
    ang_r = row[:, None] * inv_freq[None, :]
    ang_c = col[:, None] * inv_freq[None, :]
    ang = jnp.concatenate([ang_r, ang_r, ang_c, ang_c], -1)
    cos, sin = jnp.cos(ang), jnp.sin(ang)
    first = (jnp.arange(SWA_DH) % 32) < 16
    sin_up = jnp.where(first[None, :], -sin, 0.0)
    sin_dn = jnp.where(first[None, :], 0.0, sin)

    def lay(t, ident):
        t = jnp.tile(t, (1, LANES // SWA_DH))
        return jnp.concatenate([t, jnp.full((tm, LANES), ident, F32)], 0)

    return lay(cos, 1.0), lay(sin_up, 0.0), lay(sin_dn, 0.0)


def _proj_kernel(a_ref, w_ref, *refs, rope, nq, nqk, qscale):
    if rope:
        cos_ref, up_ref, dn_ref, o_ref, wb_ref = refs
    else:
        o_ref, wb_ref = refs
    j = pl.program_id(0)
    i = pl.program_id(1)

    @pl.when(i == 0)
    def _():
        wb_ref[...] = w_ref[...].astype(BF16)

    acc = jnp.dot(a_ref[...], wb_ref[...], preferred_element_type=F32)
    scale = jnp.where(j < nq, qscale, 1.0).astype(F32)
    if not rope:
        o_ref[...] = (acc * scale if nq else acc).astype(o_ref.dtype)
        return

    @pl.when(j < nqk)
    def _():
        cos = cos_ref[...] * scale
        up = up_ref[...] * scale
        dn = dn_ref[...] * scale
        for c in range(acc.shape[1] // LANES):
            x = acc[:, c * LANES:(c + 1) * LANES]
            y = x * cos + pltpu.roll(x, LANES - 16, 1) * up + pltpu.roll(x, 16, 1) * dn
            o_ref[:, c * LANES:(c + 1) * LANES] = y.astype(o_ref.dtype)

    @pl.when(j >= nqk)
    def _():
        o_ref[...] = acc.astype(o_ref.dtype)


def _proj(a, w, tn, rope=None, qscale_cols=None):
    n = w.shape[1]
    tm = TM
    in_specs = [
        pl.BlockSpec((tm, D), lambda j, i: (i, 0)),
        pl.BlockSpec((D, tn), lambda j, i: (0, j)),
    ]
    args = [a, w]
    kw = dict(rope=False, nq=0, nqk=0, qscale=1.0)
    if qscale_cols is not None:
        kw = dict(rope=False, nq=qscale_cols[0] // tn, nqk=0, qscale=qscale_cols[1])
    if rope is not None:
        nq_cols, nqk_cols, qscale, tables = rope
        nlat = TL // tm
        per_seq = S // tm

        def tmap(j, i):
            return (jnp.where(i < nlat, i % per_seq, per_seq), 0)

        in_specs += [pl.BlockSpec((tm, LANES), tmap)] * 3
        args += list(tables)
        kw = dict(rope=True, nq=nq_cols // tn, nqk=nqk_cols // tn, qscale=qscale)
    return pl.pallas_call(
        functools.partial(_proj_kernel, **kw),
        grid=(n // tn, T // tm),
        in_specs=in_specs,
        out_specs=pl.BlockSpec((tm, tn), lambda j, i: (i, j)),
        out_shape=jax.ShapeDtypeStruct((T, n), BF16),
        scratch_shapes=[pltpu.VMEM((D, tn), BF16)],
        compiler_params=_cp("arbitrary", "arbitrary"),
        name="proj",
    )(*args)


def _ln_epilogue(z, lng, lnb):
    mu = jnp.mean(z, -1, keepdims=True)
    zc = z - mu
    var = jnp.mean(zc * zc, -1, keepdims=True)
    return zc * lax.rsqrt(var + LN_EPS) * lng + lnb


def _outln_kernel(o_ref, w_ref, h_ref, g_ref, lng_ref, lnb_ref, sc_ref, sh_ref, hout_ref, uout_ref):
    y = jnp.dot(o_ref[...], w_ref[...], preferred_element_type=F32)
    hn = _ln_epilogue(ALPHA * h_ref[...] + g_ref[0] * y, lng_ref[...], lnb_ref[...])
    hout_ref[...] = hn
    uout_ref[...] = (hn * (1.0 + sc_ref[0]) + sh_ref[0]).astype(BF16)


def _outln(o, w_bf16, h, mods, layer, lng, lnb, n_rows):
    kin = o.shape[1]
    tm = 256 if kin > D else TM_LN
    const = lambda i: (0, 0)
    return pl.pallas_call(
        _outln_kernel,
        grid=(n_rows // tm,),
        in_specs=[
            pl.BlockSpec((tm, kin), lambda i: (i, 0)),
            pl.BlockSpec((kin, D), const, pipeline_mode=pl.Buffered(1)),
            pl.BlockSpec((tm, D), lambda i: (i, 0)),
            _mod_spec(layer, 2, tm, 0),
            pl.BlockSpec((1, D), const),
            pl.BlockSpec((1, D), const),
            _mod_spec(layer, 4, tm, 0),
            _mod_spec(layer, 3, tm, 0),
        ],
        out_specs=[pl.BlockSpec((tm, D), lambda i: (i, 0)), pl.BlockSpec((tm, D), lambda i: (i, 0))],
        out_shape=[jax.ShapeDtypeStruct((n_rows, D), F32), jax.ShapeDtypeStruct((n_rows, D), BF16)],
        compiler_params=_cp("arbitrary"),
        name="outln",
    )(o, w_bf16, h, mods, lng, lnb, mods, mods)


def _mlp_kernel(u_ref, w1_ref, w2_ref, h_ref, g_ref, lng_ref, lnb_ref, sc_ref, sh_ref,
                hout_ref, uout_ref, acc_ref):
    j = pl.program_id(1)

    @pl.when(j == 0)
    def _():
        acc_ref[...] = jnp.zeros_like(acc_ref)

    hid = jnp.dot(u_ref[...], w1_ref[...], preferred_element_type=F32)
    hid = jnp.square(jnp.maximum(hid, 0.0)).astype(BF16)
    acc_ref[...] += jnp.dot(hid, w2_ref[...], preferred_element_type=F32)

    @pl.when(j == pl.num_programs(1) - 1)
    def _():
        hn = _ln_epilogue(ALPHA * h_ref[...] + g_ref[0] * acc_ref[...], lng_ref[...], lnb_ref[...])
        hout_ref[...] = hn
        uout_ref[...] = (hn * (1.0 + sc_ref[0]) + sh_ref[0]).astype(BF16)


def _mlp(u, w1, w2, h, mods, layer, next_layer, lng, lnb, n_rows):
    tm = TM_LN
    tf = 512
    const = lambda i, j: (0, 0)
    row = lambda i, j: (i, 0)
    return pl.pallas_call(
        _mlp_kernel,
        grid=(n_rows // tm, D_FF // tf),
        in_specs=[
            pl.BlockSpec((tm, D), row),
            pl.BlockSpec((D, tf), lambda i, j: (0, j)),
            pl.BlockSpec((tf, D), lambda i, j: (j, 0)),
            pl.BlockSpec((tm, D), row),
            _mod_spec(layer, 5, tm, 0),
            pl.BlockSpec((1, D), const),
            pl.BlockSpec((1, D), const),
            _mod_spec(next_layer, 1, tm, 0),
            _mod_spec(next_layer, 0, tm, 0),
        ],
        out_specs=[pl.BlockSpec((tm, D), row), pl.BlockSpec((tm, D), row)],
        out_shape=[jax.ShapeDtypeStruct((n_rows, D), F32), jax.ShapeDtypeStruct((n_rows, D), BF16)],
        scratch_shapes=[pltpu.VMEM((tm, D), F32)],
        compiler_params=_cp("arbitrary", "arbitrary"),
        name="mlp",
    )(u, w1, w2, h, mods, lng, lnb, mods, mods)


def _shift_rows(x, k):
    n = x.shape[0]
    t = lax.broadcasted_iota(jnp.int32, x.shape, 0)
    y = pltpu.roll(x, (-k) % n, 0)
    return jnp.where((t + k >= 0) & (t + k < n), y, 0.0)


def _lru_conv(r_ref, cw, cb):
    x = r_ref[...].astype(F32)
    y = cb + cw[2:3] * x
    y = y + cw[0:1] * _shift_rows(x, -2)
    y = y + cw[1:2] * _shift_rows(x, -1)
    y = y + cw[3:4] * _shift_rows(x, 1)
    return y


def _local_scan(a, b, reverse):
    n = a.shape[0]
    sub = lax.broadcasted_iota(jnp.int32, a.shape, 0) % 8
    for d in (1, 2, 4):
        if reverse:
            ok = sub < 8 - d
            a_sh = pltpu.roll(a, n - d, 0)
            b_sh = pltpu.roll(b, n - d, 0)
        else:
            ok = sub >= d
            a_sh = pltpu.roll(a, d, 0)
            b_sh = pltpu.roll(b, d, 0)
        b = jnp.where(ok, a * b_sh + b, b)
        a = jnp.where(ok, a * a_sh, a)
    return a, b


LRU_CHUNK = 256
LRU_T = L + S


def _lru_kernel(gl_ref, gc_ref, rl_ref, rc_ref, cw_ref, cb_ref, wg_ref, bg_ref, lam_ref,
                ol_ref, oc_ref, x_ref, af_ref, bf_ref, ab_ref, bb_ref):
    cw = cw_ref[...]
    cb = cb_ref[...]
    x_ref[0:L, :] = _lru_conv(rc_ref, cw, cb)
    x_ref[L:LRU_T, :] = _lru_conv(rl_ref, cw, cb)

    lam = lam_ref[...]
    nsp = -(jnp.maximum(-lam, 0.0) + jnp.log1p(jnp.exp(-jnp.abs(lam))))
    wg = wg_ref[0]
    bg = bg_ref[...]

    for c in range(LRU_T // LRU_CHUNK):
        rows = pl.ds(c * LRU_CHUNK, LRU_CHUNK)
        xs = x_ref[rows, :]
        z = jnp.dot(xs.astype(BF16), wg, preferred_element_type=F32)
        for d, (a_ref, b_ref) in enumerate(((af_ref, bf_ref), (ab_ref, bb_ref))):
            r = jax.nn.sigmoid(z[:, (2 * d) * LRU_BW:(2 * d + 1) * LRU_BW] + bg[2 * d:2 * d + 1])
            g = jax.nn.sigmoid(z[:, (2 * d + 1) * LRU_BW:(2 * d + 2) * LRU_BW] + bg[2 * d + 1:2 * d + 2])
            log_a = LRU_C * r * nsp[d:d + 1]
            a = jnp.exp(log_a)
            b = jnp.sqrt(-jnp.tanh(log_a) * (a * a + 1.0)) * (g * xs)
            a, b = _local_scan(a, b, reverse=(d == 1))
            a_ref[rows, :] = a
            b_ref[rows, :] = b

    nblk = LRU_T // 8
    ncb = L // 8

    def fwd(k, h):
        rows = pl.ds(pl.multiple_of(k * 8, 8), 8)
        hb = bf_ref[rows, :] + af_ref[rows, :] * h
        bf_ref[rows, :] = hb
        return jnp.broadcast_to(hb[7:8, :], hb.shape)

    def bwd(k, h):
        rows = pl.ds(pl.multiple_of(k * 8, 8), 8)
        hb = bb_ref[rows, :] + ab_ref[rows, :] * h
        bb_ref[rows, :] = hb
        return jnp.broadcast_to(hb[0:1, :], hb.shape)

    h0 = jnp.zeros((8, LRU_BW), F32)
    lax.fori_loop(0, nblk, fwd, h0)
    hc = lax.fori_loop(0, ncb, lambda k, h: bwd(ncb - 1 - k, h), h0)
    lax.fori_loop(0, nblk - ncb, lambda k, h: bwd(nblk - 1 - k, h), hc)

    oc_ref[...] = (jax.nn.gelu(gc_ref[...].astype(F32)) * (bf_ref[0:L, :] + bb_ref[0:L, :])).astype(BF16)
    ol_ref[...] = (jax.nn.gelu(gl_ref[...].astype(F32)) * (bf_ref[L:LRU_T, :] + bb_ref[L:LRU_T, :])).astype(BF16)


def _lru(p, conv_w, conv_b, gate_w, gate_b, lam):
    wg = jnp.transpose(gate_w, (2, 3, 0, 1, 4)).reshape(LRU_NB, LRU_BW, 4 * LRU_BW).astype(BF16)
    bg = gate_b.reshape(4, D)
    nlb = S // L
    ctx0 = TL // L
    lat = pl.BlockSpec((S, LRU_BW), lambda b, n: (b, n))
    ctx = pl.BlockSpec((L, LRU_BW), lambda b, n: (ctx0 + b, n))
    out_lat, out_ctx = pl.pallas_call(
        _lru_kernel,
        grid=(B, LRU_NB),
        in_specs=[
            lat, ctx,
            pl.BlockSpec((S, LRU_BW), lambda b, n: (b, LRU_NB + n)),
            pl.BlockSpec((L, LRU_BW), lambda b, n: (ctx0 + b, LRU_NB + n)),
            pl.BlockSpec((4, LRU_BW), lambda b, n: (0, n)),
            pl.BlockSpec((1, LRU_BW), lambda b, n: (0, n)),
            pl.BlockSpec((1, LRU_BW, 4 * LRU_BW), lambda b, n: (n, 0, 0)),
            pl.BlockSpec((4, LRU_BW), lambda b, n: (0, n)),
            pl.BlockSpec((2, LRU_BW), lambda b, n: (0, n)),
        ],
        out_specs=[lat, pl.BlockSpec((L, LRU_BW), lambda b, n: (b, n))],
        out_shape=[jax.ShapeDtypeStruct((T, D), BF16), jax.ShapeDtypeStruct((TC, D), BF16)],
        scratch_shapes=[pltpu.VMEM((LRU_T, LRU_BW), F32)] * 5,
        compiler_params=_cp("arbitrary", "arbitrary"),
        name="lru",
    )(p, p, p, p, conv_w, conv_b.reshape(1, D), wg, bg, lam)
    del nlb
    return lax.dynamic_update_slice(out_lat, out_ctx, (TL, 0))


def _dif_kernel(*refs, n_kv, lam_init):
    q_ref = refs[0]
    kv = refs[1:1 + 2 * n_kv]
    lam_ref, sub_ref, o_ref = refs[1 + 2 * n_kv:]
    lam = lam_ref[...]
    lam_full = (jnp.exp(jnp.sum(lam[0:1] * lam[1:2], -1, keepdims=True))
                - jnp.exp(jnp.sum(lam[2:3] * lam[3:4], -1, keepdims=True)) + lam_init)
    q = q_ref[...]
    lane = lax.broadcasted_iota(jnp.int32, q.shape, 1)
    ks = [kv[2 * t][...] for t in range(n_kv)]
    vs = [kv[2 * t + 1][...] for t in range(n_kv)]
    o = None
    for m in range(2):
        qm = jnp.where((lane >= 64 * m) & (lane < 64 * (m + 1)), q, jnp.zeros_like(q))
        s = [lax.dot_general(qm, k, (((1,), (1,)), ((), ())), preferred_element_type=F32) for k in ks]
        mx = functools.reduce(jnp.maximum, [jnp.max(x, -1, keepdims=True) for x in s])
        e = [jnp.exp(x - mx) for x in s]
        den = functools.reduce(lambda u, w: u + w, [jnp.sum(x, -1, keepdims=True) for x in e])
        coef = 1.0 / den if m == 0 else -lam_full / den
        part = functools.reduce(
            lambda u, w: u + w,
            [jnp.dot((x * coef).astype(BF16), v, preferred_element_type=F32) for x, v in zip(e, vs)])
        o = part if o is None else o + part
    y = o * lax.rsqrt(jnp.mean(o * o, -1, keepdims=True) + LN_EPS) * sub_ref[...]
    o_ref[...] = (y * (1.0 - lam_init)).astype(BF16)


def _dif_attention(qkv, lam, subln, layer_idx):
    lam_init = 0.8 - 0.6 * math.exp(-0.3 * layer_idx)
    tq = 256
    nh = DIF_H
    kern = functools.partial(_dif_kernel, lam_init=lam_init)
    small = [pl.BlockSpec((4, 64), lambda *g: (0, 0)), pl.BlockSpec((1, DIF_DV), lambda *g: (0, 0))]
    sub = subln.reshape(1, DIF_DV)
    ctx0 = TL // L
    o_lat = pl.pallas_call(
        functools.partial(kern, n_kv=2),
        grid=(B, nh, S // tq),
        in_specs=[
            pl.BlockSpec((tq, DIF_DV), lambda b, h, i: (b * (S // tq) + i, h)),
            pl.BlockSpec((S, DIF_DV), lambda b, h, i: (b, nh + h)),
            pl.BlockSpec((S, DIF_DV), lambda b, h, i: (b, 2 * nh + h)),
            pl.BlockSpec((L, DIF_DV), lambda b, h, i: (ctx0 + b, nh + h)),
            pl.BlockSpec((L, DIF_DV), lambda b, h, i: (ctx0 + b, 2 * nh + h)),
        ] + small,
        out_specs=pl.BlockSpec((tq, DIF_DV), lambda b, h, i: (b * (S // tq) + i, h)),
        out_shape=jax.ShapeDtypeStruct((T, D), BF16),
        compiler_params=_cp("arbitrary", "arbitrary", "arbitrary"),
        name="dif_lat",
    )(qkv, qkv, qkv, qkv, qkv, lam, sub)
    o_ctx = pl.pallas_call(
        functools.partial(kern, n_kv=1),
        grid=(B, nh),
        in_specs=[
            pl.BlockSpec((L, DIF_DV), lambda b, h: (ctx0 + b, h)),
            pl.BlockSpec((L, DIF_DV), lambda b, h: (ctx0 + b, nh + h)),
            pl.BlockSpec((L, DIF_DV), lambda b, h: (ctx0 + b, 2 * nh + h)),
        ] + small,
        out_specs=pl.BlockSpec((L, DIF_DV), lambda b, h: (b, h)),
        out_shape=jax.ShapeDtypeStruct((TC, D), BF16),
        compiler_params=_cp("arbitrary", "arbitrary"),
        name="dif_ctx",
    )(qkv, qkv, qkv, lam, sub)
    return lax.dynamic_update_slice(o_lat, o_ctx, (TL, 0))


def _ret_kernel(ql_ref, qc_ref, kl_ref, kc_ref, vl_ref, vc_ref, gl_ref, gc_ref, lg_ref,
                ol_ref, oc_ref, rf_ref, rb_ref, il_ref, ic_ref):
    C = RET_CH
    lg = lg_ref[0]
    lgf = lg[0:1, :]
    lgb = lg[1:2, :]
    row = lax.broadcasted_iota(jnp.int32, (C, C), 0).astype(F32)
    col = lax.broadcasted_iota(jnp.int32, (C, C), 1).astype(F32)
    rel = row - col
    decay = jnp.exp(jnp.abs(rel) * jnp.where(rel >= 0, lgf, lgb))
    wide = lambda t, n: jnp.concatenate([t] * n, axis=1)
    qdec_f = wide(jnp.exp((row + 1.0) * lgf), RET_DK // C)
    kdec_f = wide(jnp.exp((C - 1.0 - row) * lgf), RET_DK // C)
    qdec_b = wide(jnp.exp((C - row) * lgb), RET_DK // C)
    kdec_b = wide(jnp.exp(row * lgb), RET_DK // C)
    cdec_f = wide(jnp.exp(C * lgf), RET_DV // C)
    cdec_b = wide(jnp.exp(C * lgb), RET_DV // C)

    tdot = lambda a, b: lax.dot_general(a, b, (((0,), (0,)), ((), ())), preferred_element_type=F32)

    def bwd_chunk(q_ref, k_ref, v_ref, i_ref, rows):
        q = q_ref[rows, :].astype(F32)
        k = k_ref[rows, :].astype(F32)
        v = v_ref[rows, :]
        i_ref[rows, :] = jnp.dot((q * qdec_b).astype(BF16), rb_ref[...].astype(BF16),
                                 preferred_element_type=F32)
        rb_ref[...] = cdec_b * rb_ref[...] + tdot((k * kdec_b).astype(BF16), v)

    def fwd_chunk(q_ref, k_ref, v_ref, g_ref, i_ref, o_ref, rows):
        qb = q_ref[rows, :]
        kb = k_ref[rows, :]
        v = v_ref[rows, :]
        q = qb.astype(F32)
        k = kb.astype(F32)
        s = lax.dot_general(qb, kb, (((1,), (1,)), ((), ())), preferred_element_type=F32) * decay
        o = jnp.dot(s.astype(BF16), v, preferred_element_type=F32)
        o = o + jnp.dot((q * qdec_f).astype(BF16), rf_ref[...].astype(BF16), preferred_element_type=F32)
        o = o + i_ref[rows, :]
        rf_ref[...] = cdec_f * rf_ref[...] + tdot((k * kdec_f).astype(BF16), v)
        o = o * lax.rsqrt(jnp.mean(o * o, -1, keepdims=True) + LN_EPS)
        g = g_ref[rows, :].astype(F32)
        o_ref[rows, :] = (g * jax.nn.sigmoid(g) * o).astype(BF16)

    rf_ref[...] = jnp.zeros_like(rf_ref)
    rb_ref[...] = jnp.zeros_like(rb_ref)
    nlc = S // C
    ncc = L // C
    dyn = lambda n: pl.ds(pl.multiple_of(n * C, C), C)

    for n in reversed(range(ncc)):
        bwd_chunk(qc_ref, kc_ref, vc_ref, ic_ref, pl.ds(n * C, C))

    def bl(n, carry):
        bwd_chunk(ql_ref, kl_ref, vl_ref, il_ref, dyn(nlc - 1 - n))
        return carry

    lax.fori_loop(0, nlc, bl, 0)

    for n in range(ncc):
        fwd_chunk(qc_ref, kc_ref, vc_ref, gc_ref, ic_ref, oc_ref, pl.ds(n * C, C))

    def fl(n, carry):
        fwd_chunk(ql_ref, kl_ref, vl_ref, gl_ref, il_ref, ol_ref, dyn(n))
        return carry

    lax.fori_loop(0, nlc, fl, 0)


def _retention(p):
    log_gf = jnp.log1p(-jnp.exp2(-5.0 - jnp.arange(RET_H, dtype=F32)))
    lg = jnp.zeros((RET_H, 8, LANES), F32)
    lg = lg.at[:, 0, :].set(log_gf[:, None]).at[:, 1, :].set(log_gf[::-1][:, None])
    ctx0 = TL // L
    nk = D // RET_DK
    nv = 2 * D // RET_DV
    lat = lambda w, off: pl.BlockSpec((S, w), lambda b, h: (b, off + h))
    ctx = lambda w, off: pl.BlockSpec((L, w), lambda b, h: (ctx0 + b, off + h))
    o_lat, o_ctx = pl.pallas_call(
        _ret_kernel,
        grid=(B, RET_H),
        in_specs=[
            lat(RET_DK, 0), ctx(RET_DK, 0),
            lat(RET_DK, nk), ctx(RET_DK, nk),
            lat(RET_DV, nv), ctx(RET_DV, nv),
            lat(RET_DV, nv + RET_H), ctx(RET_DV, nv + RET_H),
            pl.BlockSpec((1, 8, LANES), lambda b, h: (h, 0, 0)),
        ],
        out_specs=[pl.BlockSpec((S, RET_DV), lambda b, h: (b, h)),
                   pl.BlockSpec((L, RET_DV), lambda b, h: (b, h))],
        out_shape=[jax.ShapeDtypeStruct((T, 2 * D), BF16), jax.ShapeDtypeStruct((TC, 2 * D), BF16)],
        scratch_shapes=[
            pltpu.VMEM((RET_DK, RET_DV), F32), pltpu.VMEM((RET_DK, RET_DV), F32),
            pltpu.VMEM((S, RET_DV), F32), pltpu.VMEM((L, RET_DV), F32),
        ],
        compiler_params=_cp("arbitrary", "arbitrary"),
        name="retention",
    )(p, p, p, p, p, p, p, p, lg)
    return lax.dynamic_update_slice(o_lat, o_ctx, (TL, 0))


def _swa_kernel(q_ref, kp_ref, kc_ref, kn_ref, kx_ref, vp_ref, vc_ref, vn_ref, vx_ref, sink_ref, o_ref):
    n = pl.program_id(2)
    nb = pl.num_programs(2)
    k2 = jnp.concatenate([kp_ref[...], kc_ref[...], kn_ref[...], kx_ref[...]], axis=0)
    v2 = jnp.concatenate([vp_ref[...], vc_ref[...], vn_ref[...], vx_ref[...]], axis=0)
    nk = k2.shape[0]
    qi = lax.broadcasted_iota(jnp.int32, (QB, nk), 0)
    kj = lax.broadcasted_iota(jnp.int32, (QB, nk), 1)
    band = (kj >= qi) & (kj <= qi + 2 * WINDOW)
    band = band & ((kj >= QB) | (n > 0)) & ((kj < 2 * QB) | (n < nb - 1))
    valid = band | (kj >= 3 * QB)
    lane = lax.broadcasted_iota(jnp.int32, (QB, LANES), 1)
    lo = lane < SWA_DH
    for c in range(4):
        t = c // 2
        qc = q_ref[:, c * LANES:(c + 1) * LANES].astype(F32)
        qsw = pltpu.roll(qc, SWA_DH, 1)
        keep = lo if t == 0 else ~lo
        out_c = None
        for hf in range(2):
            src = qc if hf == t else qsw
            lhs = jnp.where(keep, src, 0.0).astype(BF16)
            s = lax.dot_general(lhs, k2, (((1,), (1,)), ((), ())), preferred_element_type=F32)
            s = jnp.where(valid, s, NEG_INF)
            sink = sink_ref[2 * c + hf][:, 0:1]
            mx = jnp.maximum(jnp.max(s, -1, keepdims=True), sink)
            e = jnp.exp(s - mx)
            den = jnp.sum(e, -1, keepdims=True) + jnp.exp(sink - mx)
            o = jnp.dot((e * (1.0 / den)).astype(BF16), v2, preferred_element_type=F32)
            if hf != t:
                o = pltpu.roll(o, SWA_DH, 1)
            sel = lo if hf == 0 else ~lo
            o = jnp.where(sel, o, 0.0)
            out_c = o if out_c is None else out_c + o
        o_ref[:, c * LANES:(c + 1) * LANES] = out_c.astype(BF16)


def _swa_attention(qkv, sink):
    nqb = S // QB
    kcol = SWA_H * SWA_DH // LANES
    vcol = kcol + SWA_KV * SWA_DH // LANES
    ctx0 = TL // L
    sink_t = jnp.broadcast_to(sink.astype(F32)[:, None, None], (SWA_H, 1, LANES))

    def win(col0, d):
        def index_map(b, p, n):
            return (b * nqb + jnp.clip(n + d, 0, nqb - 1), col0 + p)
        return pl.BlockSpec((QB, LANES), index_map)

    ctx = lambda col0: pl.BlockSpec((L, LANES), lambda b, p, n: (ctx0 + b, col0 + p))
    return pl.pallas_call(
        _swa_kernel,
        grid=(B, SWA_KV // 2, nqb),
        in_specs=[
            pl.BlockSpec((QB, 4 * LANES), lambda b, p, n: (b * nqb + n, p)),
            win(kcol, -1), win(kcol, 0), win(kcol, 1), ctx(kcol),
            win(vcol, -1), win(vcol, 0), win(vcol, 1), ctx(vcol),
            pl.BlockSpec((8, 1, LANES), lambda b, p, n: (p, 0, 0)),
        ],
        out_specs=pl.BlockSpec((QB, 4 * LANES), lambda b, p, n: (b * nqb + n, p)),
        out_shape=jax.ShapeDtypeStruct((TL, D), BF16),
        compiler_params=_cp("arbitrary", "arbitrary", "arbitrary"),
        name="swa",
    )(qkv, qkv, qkv, qkv, qkv, qkv, qkv, qkv, qkv, sink_t)


@jax.jit
def _forward(x, c, ctx, c_ctx, ada_w, ada_b, ln_g, ln_b, mlp_w1, mlp_w2,
             lru_w_in, lru_conv_w, lru_conv_b, lru_gate_w, lru_gate_b, lru_lambda, lru_w_out,
             dif_w_qkv, dif_lambda, dif_subln, dif_w_out, ret_w_qkvg, ret_w_out,
             swa_w_qkv, swa_sink, swa_w_out):
    h = jnp.concatenate([x.reshape(TL, D), ctx.reshape(TC, D)], 0)
    cvec = jnp.concatenate([c, c_ctx[None, :], jnp.zeros((NSEG - B - 1, D), F32)], 0)
    mods = _ada(cvec, ada_w, ada_b)
    tables = _rope_tables(TM)
    u = _modulate(h, mods, 0)
    w_outs = [lru_w_out[0], dif_w_out[0], ret_w_out[0], swa_w_out[0]]
    for i in range(DEPTH):
        last = i == DEPTH - 1
        n_rows = TL if last else T
        if i == 0:
            p = _proj(u, lru_w_in[0], 1024)
            o = _lru(p, lru_conv_w[0], lru_conv_b[0], lru_gate_w[0], lru_gate_b[0], lru_lambda[0])
        elif i == 1:
            p = _proj(u, dif_w_qkv[0], 1024, rope=(D, 2 * D, 64 ** -0.5, tables))
            o = _dif_attention(p, dif_lambda[0], dif_subln[0], i)
        elif i == 2:
            p = _proj(u, ret_w_qkvg[0], 1024, qscale_cols=(D, RET_DK ** -0.5))
            o = _retention(p)
        else:
            p = _proj(u, swa_w_qkv[0], 512, rope=(SWA_H * SWA_DH, (SWA_H + SWA_KV) * SWA_DH, SWA_DH ** -0.5, tables))
            o = _swa_attention(p, swa_sink[0])
        h, u2 = _outln(o, _cast_bf16(w_outs[i]), h, mods, i, ln_g[i, 0:1], ln_b[i, 0:1], n_rows)
        h, u = _mlp(u2, _cast_bf16(mlp_w1[i]), _cast_bf16(mlp_w2[i]), h, mods, i, min(i + 1, DEPTH - 1),
                    ln_g[i, 1:2], ln_b[i, 1:2], n_rows)
    return h.reshape(B, S, D)


def kernel(x, c, ctx, c_ctx, ada_w, ada_b, ln_g, ln_b, mlp_w1, mlp_w2, lru_w_in, lru_conv_w, lru_conv_b,
           lru_gate_w, lru_gate_b, lru_lambda, lru_w_out, dif_w_qkv, dif_lambda, dif_subln, dif_w_out,
           ret_w_qkvg, ret_w_out, swa_w_qkv, swa_sink, swa_w_out):
    return _forward(x, c, ctx, c_ctx, ada_w, ada_b, ln_g, ln_b, mlp_w1, mlp_w2,
                    lru_w_in, lru_conv_w, lru_conv_b, lru_gate_w, lru_gate_b, lru_lambda, lru_w_out,
                    dif_w_qkv, dif_lambda, dif_subln, dif_w_out, ret_w_qkvg, ret_w_out,
                    swa_w_qkv, swa_sink, swa_w_out)
```

```python
import functools
import math

import jax
import jax.numpy as jnp
from jax import lax
from jax.experimental import pallas as pl
from jax.experimental.pallas import tpu as pltpu

F32 = jnp.float32
BF16 = jnp.bfloat16

D = 2048
B = 4
S = 2048
L = 256
DEPTH = 4
GRID_W = 64
TL = B * S
TC = B * L
T = TL + TC
NSEG = 8
ADA = 6
D_FF = 4 * D
ALPHA = (2 * DEPTH) ** 0.25
LN_EPS = 1e-5
ROPE_BASE = 10000.0
NEG_INF = -1e30

LRU_BW = 256
LRU_NB = D // LRU_BW
LRU_C = 8.0
DIF_H = 16
DIF_DV = 128
RET_H = 8
RET_DK = 256
RET_DV = 512
RET_CH = 128
SWA_H = 32
SWA_KV = 8
SWA_DH = 64
SWA_G = SWA_H // SWA_KV
WINDOW = 128
QB = 128

VMEM_LIMIT = 52 * 1024 * 1024
LANES = 128

TM = 1024
TM_LN = 512
TM_MLP = 1024
PROJ_SUB = 256
LN_SUB = 256
MLP_SUB = 1024
LOG2E = math.log2(math.e)


def _cp(*sem):
    return pltpu.CompilerParams(dimension_semantics=sem, vmem_limit_bytes=VMEM_LIMIT)


def _seg(i, tm):
    return jnp.minimum((i * tm) // S, B)


def _ada_kernel(c_ref, w_ref, b_ref, o_ref):
    c = c_ref[...]
    a = (c * jax.nn.sigmoid(c)).astype(BF16)
    o_ref[0] = jnp.dot(a, w_ref[0].astype(BF16), preferred_element_type=F32) + b_ref[0]


def _ada(cvec, ada_w, ada_b):
    tn = 1024
    out = pl.pallas_call(
        _ada_kernel,
        grid=(DEPTH, ADA * D // tn),
        in_specs=[
            pl.BlockSpec((NSEG, D), lambda l, j: (0, 0)),
            pl.BlockSpec((1, D, tn), lambda l, j: (l, 0, j)),
            pl.BlockSpec((1, 1, tn), lambda l, j: (l, 0, j)),
        ],
        out_specs=pl.BlockSpec((1, NSEG, tn), lambda l, j: (l, 0, j)),
        out_shape=jax.ShapeDtypeStruct((DEPTH, NSEG, ADA * D), F32),
        compiler_params=_cp("arbitrary", "arbitrary"),
        name="ada",
    )(cvec, ada_w, ada_b.reshape(DEPTH, 1, ADA * D))
    return out.reshape(DEPTH * NSEG * ADA, 1, D)


def _mod_spec(layer, chunk, tm, grid_pos):
    base = layer * NSEG * ADA + chunk

    def index_map(*g):
        return (base + _seg(g[grid_pos], tm) * ADA, 0, 0)

    return pl.BlockSpec((1, 1, D), index_map)


def _modulate_kernel(x_ref, ctx_ref, sc_ref, sh_ref, h_ref, u_ref, *, nlat):
    def emit(src_ref):
        v = src_ref[...]
        h_ref[...] = v
        u_ref[...] = (v * (1.0 + sc_ref[0]) + sh_ref[0]).astype(BF16)

    pl.when(pl.program_id(0) < nlat)(lambda: emit(x_ref))
    pl.when(pl.program_id(0) >= nlat)(lambda: emit(ctx_ref))


def _modulate(x, ctx, mods, layer):
    tm = TM_LN
    nlat = TL // tm
    return pl.pallas_call(
        functools.partial(_modulate_kernel, nlat=nlat),
        grid=(T // tm,),
        in_specs=[
            pl.BlockSpec((tm, D), lambda i: (jnp.minimum(i, nlat - 1), 0)),
            pl.BlockSpec((tm, D), lambda i: (jnp.maximum(i - nlat, 0), 0)),
            _mod_spec(layer, 1, tm, 0),
            _mod_spec(layer, 0, tm, 0),
        ],
        out_specs=[pl.BlockSpec((tm, D), lambda i: (i, 0))] * 2,
        out_shape=[jax.ShapeDtypeStruct((T, D), F32), jax.ShapeDtypeStruct((T, D), BF16)],
        compiler_params=_cp("arbitrary"),
        name="modulate",
    )(x, ctx, mods, mods)


def _cast_kernel(w_ref, o_ref):
    o_ref[...] = w_ref[...].astype(BF16)


CAST_BLOCK_ELEMS = 1 << 20


def _cast_bf16(w):
    r, c = w.shape
    tr = CAST_BLOCK_ELEMS // c
    return pl.pallas_call(
        _cast_kernel,
        grid=(r // tr,),
        in_specs=[pl.BlockSpec((tr, c), lambda i: (i, 0))],
        out_specs=pl.BlockSpec((tr, c), lambda i: (i, 0)),
        out_shape=jax.ShapeDtypeStruct((r, c), BF16),
        compiler_params=_cp("arbitrary"),
        name="cast",
    )(w)


def _rope_tables(tm):
    rows = S // GRID_W
    row = jnp.repeat(jnp.arange(rows, dtype=F32), GRID_W)
    col = jnp.tile(jnp.arange(GRID_W, dtype=F32), rows)
    half = SWA_DH // 2
    inv_freq = ROPE_BASE ** (-jnp.arange(0, half, 2, dtype=F32) / half)
    ang_r = row[:, None] * inv_freq[None, :]
    ang_c = col[:, None] * inv_freq[None, :]
    ang = jnp.concatenate([ang_r, ang_r, ang_c, ang_c], -1)
    cos, sin = jnp.cos(ang), jnp.sin(ang)
    first = (jnp.arange(SWA_DH) % 32) < 16
    sin_up = jnp.where(first[None, :], -sin, 0.0)
    sin_dn = jnp.where(first[None, :], 0.0, sin)

    def lay(t, ident):
        t = jnp.tile(t, (1, LANES // SWA_DH))
        return jnp.concatenate([t, jnp.full((tm, LANES), ident, F32)], 0)

    return lay(cos, 1.0), lay(sin_up, 0.0), lay(sin_dn, 0.0)


def _proj_kernel(a_ref, w_ref, *refs, rope, nq, nqk, qscale):
    if rope:
        cos_ref, up_ref, dn_ref, o_ref, wb_ref = refs
    else:
        o_ref, wb_ref = refs
    j = pl.program_id(0)
    i = pl.program_id(1)

    @pl.when(i == 0)
    def _():
        wb_ref[...] = w_ref[...].astype(BF16)

    tm, tn = o_ref.shape
    subs = [pl.ds(r * PROJ_SUB, PROJ_SUB) for r in range(tm // PROJ_SUB)]
    scale = jnp.where(j < nq, qscale, 1.0).astype(F32)

    def plain(scaled):
        for rows in subs:
            acc = jnp.dot(a_ref[rows, :], wb_ref[...], preferred_element_type=F32)
            o_ref[rows, :] = (acc * scale if scaled else acc).astype(o_ref.dtype)

    if not rope:
        plain(nq > 0)
        return

    @pl.when(j < nqk)
    def _():
        for rows in subs:
            acc = jnp.dot(a_ref[rows, :], wb_ref[...], preferred_element_type=F32)
            cos = cos_ref[rows, :] * scale
            up = up_ref[rows, :] * scale
            dn = dn_ref[rows, :] * scale
            for c in range(tn // LANES):
                x = acc[:, c * LANES:(c + 1) * LANES]
                y = x * cos + pltpu.roll(x, LANES - 16, 1) * up + pltpu.roll(x, 16, 1) * dn
                o_ref[rows, c * LANES:(c + 1) * LANES] = y.astype(o_ref.dtype)

    pl.when(j >= nqk)(lambda: plain(False))


def _proj(a, w, tn, rope=None, qscale_cols=None):
    n = w.shape[1]
    tm = TM
    in_specs = [
        pl.BlockSpec((tm, D), lambda j, i: (i, 0)),
        pl.BlockSpec((D, tn), lambda j, i: (0, j)),
    ]
    args = [a, w]
    kw = dict(rope=False, nq=0, nqk=0, qscale=1.0)
    if qscale_cols is not None:
        kw = dict(rope=False, nq=qscale_cols[0] // tn, nqk=0, qscale=qscale_cols[1])
    if rope is not None:
        nq_cols, nqk_cols, qscale, tables = rope
        nlat = TL // tm
        per_seq = S // tm

        def tmap(j, i):
            return (jnp.where(i < nlat, i % per_seq, per_seq), 0)

        in_specs += [pl.BlockSpec((tm, LANES), tmap)] * 3
        args += list(tables)
        kw = dict(rope=True, nq=nq_cols // tn, nqk=nqk_cols // tn, qscale=qscale)
    return pl.pallas_call(
        functools.partial(_proj_kernel, **kw),
        grid=(n // tn, T // tm),
        in_specs=in_specs,
        out_specs=pl.BlockSpec((tm, tn), lambda j, i: (i, j)),
        out_shape=jax.ShapeDtypeStruct((T, n), BF16),
        scratch_shapes=[pltpu.VMEM((D, tn), BF16)],
        compiler_params=_cp("arbitrary", "arbitrary"),
        name="proj",
    )(*args)


def _ln_epilogue(z, lng, lnb):
    mu = jnp.mean(z, -1, keepdims=True)
    zc = z - mu
    var = jnp.mean(zc * zc, -1, keepdims=True)
    return zc * lax.rsqrt(var + LN_EPS) * lng + lnb


def _residual_ln(rows, y, h_ref, g_ref, lng_ref, lnb_ref, sc_ref, sh_ref, hout_ref, uout_ref):
    hn = _ln_epilogue(ALPHA * h_ref[rows, :] + g_ref[0] * y, lng_ref[...], lnb_ref[...])
    hout_ref[rows, :] = hn
    if uout_ref is not None:
        uout_ref[rows, :] = (hn * (1.0 + sc_ref[0]) + sh_ref[0]).astype(BF16)


def _outln_kernel(o_ref, w_ref, h_ref, g_ref, lng_ref, lnb_ref, sc_ref, sh_ref, hout_ref, uout_ref):
    for r in range(o_ref.shape[0] // LN_SUB):
        rows = pl.ds(r * LN_SUB, LN_SUB)
        y = jnp.dot(o_ref[rows, :], w_ref[...], preferred_element_type=F32)
        _residual_ln(rows, y, h_ref, g_ref, lng_ref, lnb_ref, sc_ref, sh_ref, hout_ref, uout_ref)


def _outln(o, w_bf16, h, mods, layer, lng, lnb, n_rows):
    kin = o.shape[1]
    tm = TM_LN
    const = lambda i: (0, 0)
    return pl.pallas_call(
        _outln_kernel,
        grid=(n_rows // tm,),
        in_specs=[
            pl.BlockSpec((tm, kin), lambda i: (i, 0)),
            pl.BlockSpec((kin, D), const, pipeline_mode=pl.Buffered(1)),
            pl.BlockSpec((tm, D), lambda i: (i, 0)),
            _mod_spec(layer, 2, tm, 0),
            pl.BlockSpec((1, D), const),
            pl.BlockSpec((1, D), const),
            _mod_spec(layer, 4, tm, 0),
            _mod_spec(layer, 3, tm, 0),
        ],
        out_specs=[pl.BlockSpec((tm, D), lambda i: (i, 0)), pl.BlockSpec((tm, D), lambda i: (i, 0))],
        out_shape=[jax.ShapeDtypeStruct((n_rows, D), F32), jax.ShapeDtypeStruct((n_rows, D), BF16)],
        compiler_params=_cp("arbitrary"),
        name="outln",
    )(o, w_bf16, h, mods, lng, lnb, mods, mods)


def _mlp_kernel(u_ref, w1_ref, w2_ref, h_ref, g_ref, lng_ref, lnb_ref, *refs, emit_u):
    if emit_u:
        sc_ref, sh_ref, hout_ref, uout_ref = refs
    else:
        (hout_ref,) = refs
        sc_ref = sh_ref = uout_ref = None
    j = pl.program_id(1)
    tm = u_ref.shape[0]

    @pl.when(j == 0)
    def _():
        hout_ref[...] = jnp.zeros_like(hout_ref)

    subs = [pl.ds(r * MLP_SUB, MLP_SUB) for r in range(tm // MLP_SUB)]
    hids = []
    for rows in subs:
        hid = jnp.dot(u_ref[rows, :], w1_ref[...], preferred_element_type=F32)
        hids.append(jnp.square(jnp.maximum(hid, 0.0)).astype(BF16))
    for rows, hid in zip(subs, hids):
        hout_ref[rows, :] += jnp.dot(hid, w2_ref[...], preferred_element_type=F32)

    @pl.when(j == pl.num_programs(1) - 1)
    def _():
        for r in range(tm // LN_SUB):
            rows = pl.ds(r * LN_SUB, LN_SUB)
            _residual_ln(rows, hout_ref[rows, :], h_ref, g_ref, lng_ref, lnb_ref, sc_ref, sh_ref,
                         hout_ref, uout_ref)


def _mlp(u, w1, w2, h, mods, layer, lng, lnb, n_rows):
    tm = TM_MLP
    tf = 512
    emit_u = layer < DEPTH - 1
    const = lambda i, j: (0, 0)
    row = lambda i, j: (i, 0)
    nff = D_FF // tf
    once = pl.Buffered(1)
    in_specs = [
        pl.BlockSpec((tm, D), row),
        pl.BlockSpec((D, tf), lambda i, j: (layer, j)),
        pl.BlockSpec((tf, D), lambda i, j: (layer * nff + j, 0)),
        pl.BlockSpec((tm, D), row, pipeline_mode=once),
        _mod_spec(layer, 5, tm, 0),
        pl.BlockSpec((1, D), const),
        pl.BlockSpec((1, D), const),
    ]
    args = [u, w1, w2, h, mods, lng, lnb]
    out_specs = [pl.BlockSpec((tm, D), row, pipeline_mode=once)]
    out_shape = [jax.ShapeDtypeStruct((n_rows, D), F32)]
    if emit_u:
        in_specs += [_mod_spec(layer + 1, 1, tm, 0), _mod_spec(layer + 1, 0, tm, 0)]
        args += [mods, mods]
        out_specs.append(pl.BlockSpec((tm, D), row, pipeline_mode=once))
        out_shape.append(jax.ShapeDtypeStruct((n_rows, D), BF16))
    res = pl.pallas_call(
        functools.partial(_mlp_kernel, emit_u=emit_u),
        grid=(n_rows // tm, nff),
        in_specs=in_specs,
        out_specs=out_specs,
        out_shape=out_shape,
        compiler_params=_cp("arbitrary", "arbitrary"),
        name="mlp",
    )(*args)
    return (res[0], res[1]) if emit_u else (res[0], None)


def _shift_rows(x, k):
    n = x.shape[0]
    t = lax.broadcasted_iota(jnp.int32, x.shape, 0)
    y = pltpu.roll(x, (-k) % n, 0)
    return jnp.where((t + k >= 0) & (t + k < n), y, 0.0)


def _lru_conv(r_ref, cw, cb):
    x = r_ref[...].astype(F32)
    y = cb + cw[2:3] * x
    y = y + cw[0:1] * _shift_rows(x, -2)
    y = y + cw[1:2] * _shift_rows(x, -1)
    y = y + cw[3:4] * _shift_rows(x, 1)
    return y


def _local_scan(a, b, reverse):
    n = a.shape[0]
    sub = lax.broadcasted_iota(jnp.int32, a.shape, 0) % 8
    for d in (1, 2, 4):
        if reverse:
            ok = sub < 8 - d
            a_sh = pltpu.roll(a, n - d, 0)
            b_sh = pltpu.roll(b, n - d, 0)
        else:
            ok = sub >= d
            a_sh = pltpu.roll(a, d, 0)
            b_sh = pltpu.roll(b, d, 0)
        b = jnp.where(ok, a * b_sh + b, b)
        a = jnp.where(ok, a * a_sh, a)
    return a, b


LRU_CHUNK = 256
LRU_T = L + S


def _lru_kernel(gl_ref, gc_ref, rl_ref, rc_ref, cw_ref, cb_ref, wg_ref, bg_ref, lam_ref,
                ol_ref, oc_ref, x_ref, af_ref, bf_ref, ab_ref, bb_ref):
    cw = cw_ref[...]
    cb = cb_ref[...]
    x_ref[0:L, :] = _lru_conv(rc_ref, cw, cb)
    x_ref[L:LRU_T, :] = _lru_conv(rl_ref, cw, cb)

    lam = lam_ref[...]
    nsp = -(jnp.maximum(-lam, 0.0) + jnp.log1p(jnp.exp(-jnp.abs(lam))))
    wg = wg_ref[0]
    bg = bg_ref[...]

    for c in range(LRU_T // LRU_CHUNK):
        rows = pl.ds(c * LRU_CHUNK, LRU_CHUNK)
        xs = x_ref[rows, :]
        z = jnp.dot(xs.astype(BF16), wg, preferred_element_type=F32)
        for d, (a_ref, b_ref) in enumerate(((af_ref, bf_ref), (ab_ref, bb_ref))):
            r = jax.nn.sigmoid(z[:, (2 * d) * LRU_BW:(2 * d + 1) * LRU_BW] + bg[2 * d:2 * d + 1])
            g = jax.nn.sigmoid(z[:, (2 * d + 1) * LRU_BW:(2 * d + 2) * LRU_BW] + bg[2 * d + 1:2 * d + 2])
            log_a = LRU_C * r * nsp[d:d + 1]
            a = jnp.exp(log_a)
            b = jnp.sqrt(-jnp.tanh(log_a) * (a * a + 1.0)) * (g * xs)
            a, b = _local_scan(a, b, reverse=(d == 1))
            a_ref[rows, :] = a
            b_ref[rows, :] = b

    nblk = LRU_T // 8
    ncb = L // 8

    def fwd(k, h):
        rows = pl.ds(pl.multiple_of(k * 8, 8), 8)
        hb = bf_ref[rows, :] + af_ref[rows, :] * h
        bf_ref[rows, :] = hb
        return jnp.broadcast_to(hb[7:8, :], hb.shape)

    def bwd(k, h):
        rows = pl.ds(pl.multiple_of(k * 8, 8), 8)
        hb = bb_ref[rows, :] + ab_ref[rows, :] * h
        bb_ref[rows, :] = hb
        return jnp.broadcast_to(hb[0:1, :], hb.shape)

    h0 = jnp.zeros((8, LRU_BW), F32)
    lax.fori_loop(0, nblk, fwd, h0)
    hc = lax.fori_loop(0, ncb, lambda k, h: bwd(ncb - 1 - k, h), h0)
    lax.fori_loop(0, nblk - ncb, lambda k, h: bwd(nblk - 1 - k, h), hc)

    oc_ref[...] = (jax.nn.gelu(gc_ref[...].astype(F32)) * (bf_ref[0:L, :] + bb_ref[0:L, :])).astype(BF16)
    ol_ref[...] = (jax.nn.gelu(gl_ref[...].astype(F32)) * (bf_ref[L:LRU_T, :] + bb_ref[L:LRU_T, :])).astype(BF16)


def _lru(p, conv_w, conv_b, gate_w, gate_b, lam):
    wg = jnp.transpose(gate_w, (2, 3, 0, 1, 4)).reshape(LRU_NB, LRU_BW, 4 * LRU_BW).astype(BF16)
    bg = gate_b.reshape(4, D)
    nlb = S // L
    ctx0 = TL // L
    lat = pl.BlockSpec((S, LRU_BW), lambda b, n: (b, n))
    ctx = pl.BlockSpec((L, LRU_BW), lambda b, n: (ctx0 + b, n))
    out_lat, out_ctx = pl.pallas_call(
        _lru_kernel,
        grid=(B, LRU_NB),
        in_specs=[
            lat, ctx,
            pl.BlockSpec((S, LRU_BW), lambda b, n: (b, LRU_NB + n)),
            pl.BlockSpec((L, LRU_BW), lambda b, n: (ctx0 + b, LRU_NB + n)),
            pl.BlockSpec((4, LRU_BW), lambda b, n: (0, n)),
            pl.BlockSpec((1, LRU_BW), lambda b, n: (0, n)),
            pl.BlockSpec((1, LRU_BW, 4 * LRU_BW), lambda b, n: (n, 0, 0)),
            pl.BlockSpec((4, LRU_BW), lambda b, n: (0, n)),
            pl.BlockSpec((2, LRU_BW), lambda b, n: (0, n)),
        ],
        out_specs=[lat, pl.BlockSpec((L, LRU_BW), lambda b, n: (b, n))],
        out_shape=[jax.ShapeDtypeStruct((T, D), BF16), jax.ShapeDtypeStruct((TC, D), BF16)],
        scratch_shapes=[pltpu.VMEM((LRU_T, LRU_BW), F32)] * 5,
        compiler_params=_cp("arbitrary", "arbitrary"),
        name="lru",
    )(p, p, p, p, conv_w, conv_b.reshape(1, D), wg, bg, lam)
    del nlb
    return lax.dynamic_update_slice(out_lat, out_ctx, (TL, 0))


def _dif_kernel(*refs, n_kv, lam_init):
    q_ref = refs[0]
    kv = refs[1:1 + 2 * n_kv]
    lam_ref, sub_ref, o_ref, kcat_ref, vcat_ref = refs[1 + 2 * n_kv:]

    @pl.when(pl.program_id(2) == 0)
    def _():
        off = 0
        for t in range(n_kv):
            n = kv[2 * t].shape[0]
            kcat_ref[off:off + n, :] = kv[2 * t][...]
            vcat_ref[off:off + n, 0:DIF_DV] = kv[2 * t + 1][...]
            off += n
        vcat_ref[:, DIF_DV:] = jnp.ones((vcat_ref.shape[0], DIF_DV), BF16)

    lam = lam_ref[...]
    lam_full = (jnp.exp(jnp.sum(lam[0:1] * lam[1:2], -1, keepdims=True))
                - jnp.exp(jnp.sum(lam[2:3] * lam[3:4], -1, keepdims=True)) + lam_init)
    q = q_ref[...]
    lane = lax.broadcasted_iota(jnp.int32, q.shape, 1)
    outs = []
    for m in range(2):
        qm = jnp.where((lane >= 64 * m) & (lane < 64 * (m + 1)), q, jnp.zeros_like(q))
        s = lax.dot_general(qm, kcat_ref[...], (((1,), (1,)), ((), ())), preferred_element_type=F32)
        e = jnp.exp2(s - jnp.max(s, -1, keepdims=True)).astype(BF16)
        r = jnp.dot(e, vcat_ref[...], preferred_element_type=F32)
        outs.append(r[:, :DIF_DV] / r[:, DIF_DV:])
    o = outs[0] - lam_full * outs[1]
    y = o * lax.rsqrt(jnp.mean(o * o, -1, keepdims=True) + LN_EPS) * sub_ref[...]
    o_ref[...] = (y * (1.0 - lam_init)).astype(BF16)


def _dif_attention(qkv, lam, subln, layer_idx):
    lam_init = 0.8 - 0.6 * math.exp(-0.3 * layer_idx)
    tq = 256
    nh = DIF_H
    kern = functools.partial(_dif_kernel, lam_init=lam_init)
    small = [pl.BlockSpec((4, 64), lambda *g: (0, 0)), pl.BlockSpec((1, DIF_DV), lambda *g: (0, 0))]
    sub = subln.reshape(1, DIF_DV)
    ctx0 = TL // L
    sem = _cp("arbitrary", "arbitrary", "arbitrary")
    scratch = lambda nk: [pltpu.VMEM((nk, DIF_DV), BF16), pltpu.VMEM((nk, 2 * DIF_DV), BF16)]
    o_lat = pl.pallas_call(
        functools.partial(kern, n_kv=2),
        grid=(B, nh, S // tq),
        in_specs=[
            pl.BlockSpec((tq, DIF_DV), lambda b, h, i: (b * (S // tq) + i, h)),
            pl.BlockSpec((S, DIF_DV), lambda b, h, i: (b, nh + h)),
            pl.BlockSpec((S, DIF_DV), lambda b, h, i: (b, 2 * nh + h)),
            pl.BlockSpec((L, DIF_DV), lambda b, h, i: (ctx0 + b, nh + h)),
            pl.BlockSpec((L, DIF_DV), lambda b, h, i: (ctx0 + b, 2 * nh + h)),
        ] + small,
        out_specs=pl.BlockSpec((tq, DIF_DV), lambda b, h, i: (b * (S // tq) + i, h)),
        out_shape=jax.ShapeDtypeStruct((T, D), BF16),
        scratch_shapes=scratch(S + L),
        compiler_params=sem,
        name="dif_lat",
    )(qkv, qkv, qkv, qkv, qkv, lam, sub)
    o_ctx = pl.pallas_call(
        functools.partial(kern, n_kv=1),
        grid=(B, nh, 1),
        in_specs=[
            pl.BlockSpec((L, DIF_DV), lambda b, h, i: (ctx0 + b, h)),
            pl.BlockSpec((L, DIF_DV), lambda b, h, i: (ctx0 + b, nh + h)),
            pl.BlockSpec((L, DIF_DV), lambda b, h, i: (ctx0 + b, 2 * nh + h)),
        ] + small,
        out_specs=pl.BlockSpec((L, DIF_DV), lambda b, h, i: (b, h)),
        out_shape=jax.ShapeDtypeStruct((TC, D), BF16),
        scratch_shapes=scratch(L),
        compiler_params=sem,
        name="dif_ctx",
    )(qkv, qkv, qkv, lam, sub)
    return lax.dynamic_update_slice(o_lat, o_ctx, (TL, 0))


def _ret_kernel(ql_ref, qc_ref, kl_ref, kc_ref, vl_ref, vc_ref, gl_ref, gc_ref, lg_ref,
                ol_ref, oc_ref, rf_ref, rb_ref, il_ref, ic_ref):
    C = RET_CH
    lg = lg_ref[0]
    lgf = lg[0:1, :]
    lgb = lg[1:2, :]
    row = lax.broadcasted_iota(jnp.int32, (C, C), 0).astype(F32)
    col = lax.broadcasted_iota(jnp.int32, (C, C), 1).astype(F32)
    rel = row - col
    decay = jnp.exp(jnp.abs(rel) * jnp.where(rel >= 0, lgf, lgb))
    wide = lambda t, n: jnp.concatenate([t] * n, axis=1)
    qdec_f = wide(jnp.exp((row + 1.0) * lgf), RET_DK // C)
    kdec_f = wide(jnp.exp((C - 1.0 - row) * lgf), RET_DK // C)
    qdec_b = wide(jnp.exp((C - row) * lgb), RET_DK // C)
    kdec_b = wide(jnp.exp(row * lgb), RET_DK // C)
    cdec_f = wide(jnp.exp(C * lgf), RET_DV // C)
    cdec_b = wide(jnp.exp(C * lgb), RET_DV // C)

    tdot = lambda a, b: lax.dot_general(a, b, (((0,), (0,)), ((), ())), preferred_element_type=F32)

    def bwd_chunk(q_ref, k_ref, v_ref, i_ref, rows):
        q = q_ref[rows, :].astype(F32)
        k = k_ref[rows, :].astype(F32)
        v = v_ref[rows, :]
        i_ref[rows, :] = jnp.dot((q * qdec_b).astype(BF16), rb_ref[...].astype(BF16),
                                 preferred_element_type=F32)
        rb_ref[...] = cdec_b * rb_ref[...] + tdot((k * kdec_b).astype(BF16), v)

    def fwd_chunk(q_ref, k_ref, v_ref, g_ref, i_ref, o_ref, rows):
        qb = q_ref[rows, :]
        kb = k_ref[rows, :]
        v = v_ref[rows, :]
        q = qb.astype(F32)
        k = kb.astype(F32)
        s = lax.dot_general(qb, kb, (((1,), (1,)), ((), ())), preferred_element_type=F32) * decay
        o = jnp.dot(s.astype(BF16), v, preferred_element_type=F32)
        o = o + jnp.dot((q * qdec_f).astype(BF16), rf_ref[...].astype(BF16), preferred_element_type=F32)
        o = o + i_ref[rows, :]
        rf_ref[...] = cdec_f * rf_ref[...] + tdot((k * kdec_f).astype(BF16), v)
        o = o * lax.rsqrt(jnp.mean(o * o, -1, keepdims=True) + LN_EPS)
        g = g_ref[rows, :].astype(F32)
        o_ref[rows, :] = (g * jax.nn.sigmoid(g) * o).astype(BF16)

    rf_ref[...] = jnp.zeros_like(rf_ref)
    rb_ref[...] = jnp.zeros_like(rb_ref)
    nlc = S // C
    ncc = L // C
    dyn = lambda n: pl.ds(pl.multiple_of(n * C, C), C)

    for n in reversed(range(ncc)):
        bwd_chunk(qc_ref, kc_ref, vc_ref, ic_ref, pl.ds(n * C, C))

    def bl(n, carry):
        bwd_chunk(ql_ref, kl_ref, vl_ref, il_ref, dyn(nlc - 1 - n))
        return carry

    lax.fori_loop(0, nlc, bl, 0)

    for n in range(ncc):
        fwd_chunk(qc_ref, kc_ref, vc_ref, gc_ref, ic_ref, oc_ref, pl.ds(n * C, C))

    def fl(n, carry):
        fwd_chunk(ql_ref, kl_ref, vl_ref, gl_ref, il_ref, ol_ref, dyn(n))
        return carry

    lax.fori_loop(0, nlc, fl, 0)


def _retention(p):
    log_gf = jnp.log1p(-jnp.exp2(-5.0 - jnp.arange(RET_H, dtype=F32)))
    lg = jnp.zeros((RET_H, 8, LANES), F32)
    lg = lg.at[:, 0, :].set(log_gf[:, None]).at[:, 1, :].set(log_gf[::-1][:, None])
    ctx0 = TL // L
    nk = D // RET_DK
    nv = 2 * D // RET_DV
    lat = lambda w, off: pl.BlockSpec((S, w), lambda b, h: (b, off + h))
    ctx = lambda w, off: pl.BlockSpec((L, w), lambda b, h: (ctx0 + b, off + h))
    o_lat, o_ctx = pl.pallas_call(
        _ret_kernel,
        grid=(B, RET_H),
        in_specs=[
            lat(RET_DK, 0), ctx(RET_DK, 0),
            lat(RET_DK, nk), ctx(RET_DK, nk),
            lat(RET_DV, nv), ctx(RET_DV, nv),
            lat(RET_DV, nv + RET_H), ctx(RET_DV, nv + RET_H),
            pl.BlockSpec((1, 8, LANES), lambda b, h: (h, 0, 0)),
        ],
        out_specs=[pl.BlockSpec((S, RET_DV), lambda b, h: (b, h)),
                   pl.BlockSpec((L, RET_DV), lambda b, h: (b, h))],
        out_shape=[jax.ShapeDtypeStruct((T, 2 * D), BF16), jax.ShapeDtypeStruct((TC, 2 * D), BF16)],
        scratch_shapes=[
            pltpu.VMEM((RET_DK, RET_DV), F32), pltpu.VMEM((RET_DK, RET_DV), F32),
            pltpu.VMEM((S, RET_DV), F32), pltpu.VMEM((L, RET_DV), F32),
        ],
        compiler_params=_cp("arbitrary", "arbitrary"),
        name="retention",
    )(p, p, p, p, p, p, p, p, lg)
    return lax.dynamic_update_slice(o_lat, o_ctx, (TL, 0))


def _swa_kernel(q_ref, kp_ref, kc_ref, kn_ref, kx_ref, vp_ref, vc_ref, vn_ref, vx_ref, sink_ref, o_ref):
    n = pl.program_id(2)
    nb = pl.num_programs(2)
    k2 = jnp.concatenate([kp_ref[...], kc_ref[...], kn_ref[...], kx_ref[...]], axis=0)
    v2 = jnp.concatenate([vp_ref[...], vc_ref[...], vn_ref[...], vx_ref[...]], axis=0)
    nk = k2.shape[0]
    v2 = jnp.concatenate([v2, jnp.ones((nk, LANES), BF16)], axis=1)
    qi = lax.broadcasted_iota(jnp.int32, (QB, nk), 0)
    kj = lax.broadcasted_iota(jnp.int32, (QB, nk), 1)
    band = (kj >= qi) & (kj <= qi + 2 * WINDOW)
    band = band & ((kj >= QB) | (n > 0)) & ((kj < 2 * QB) | (n < nb - 1))
    valid = band | (kj >= 3 * QB)
    bias = jnp.where(valid, 0.0, NEG_INF).astype(F32)
    bias = jnp.concatenate([bias] * SWA_G, axis=0)
    lane = lax.broadcasted_iota(jnp.int32, (QB, LANES), 1)
    lo = lane < SWA_DH
    qcols = [q_ref[:, c * LANES:(c + 1) * LANES].astype(F32) for c in range(4)]
    qswap = [pltpu.roll(x, SWA_DH, 1) for x in qcols]
    out_cols = [None] * 4
    for t in range(2):
        keep = lo if t == 0 else ~lo
        heads = [(2 * t + g // 2, g % 2) for g in range(SWA_G)]
        qs = jnp.concatenate(
            [jnp.where(keep, qcols[c] if hf == t else qswap[c], 0.0).astype(BF16) for c, hf in heads], axis=0)
        s = lax.dot_general(qs, k2, (((1,), (1,)), ((), ())), preferred_element_type=F32) + bias
        sink = jnp.concatenate(
            [jnp.broadcast_to(sink_ref[2 * c + hf][:, 0:1] * LOG2E, (QB, 1)) for c, hf in heads], axis=0)
        mx = jnp.maximum(jnp.max(s, -1, keepdims=True), sink)
        e = jnp.exp2(s - mx).astype(BF16)
        r = jnp.dot(e, v2, preferred_element_type=F32)
        o = r[:, :LANES] / (r[:, LANES:] + jnp.exp2(sink - mx))
        for g, (c, hf) in enumerate(heads):
            og = o[g * QB:(g + 1) * QB, :]
            if hf != t:
                og = pltpu.roll(og, SWA_DH, 1)
            og = jnp.where(lo if hf == 0 else ~lo, og, 0.0)
            out_cols[c] = og if out_cols[c] is None else out_cols[c] + og
    for c in range(4):
        o_ref[:, c * LANES:(c + 1) * LANES] = out_cols[c].astype(BF16)


def _swa_attention(qkv, sink):
    nqb = S // QB
    kcol = SWA_H * SWA_DH // LANES
    vcol = kcol + SWA_KV * SWA_DH // LANES
    ctx0 = TL // L
    sink_t = jnp.broadcast_to(sink.astype(F32)[:, None, None], (SWA_H, 1, LANES))

    def win(col0, d):
        def index_map(b, p, n):
            return (b * nqb + jnp.clip(n + d, 0, nqb - 1), col0 + p)
        return pl.BlockSpec((QB, LANES), index_map)

    ctx = lambda col0: pl.BlockSpec((L, LANES), lambda b, p, n: (ctx0 + b, col0 + p))
    return pl.pallas_call(
        _swa_kernel,
        grid=(B, SWA_KV // 2, nqb),
        in_specs=[
            pl.BlockSpec((QB, 4 * LANES), lambda b, p, n: (b * nqb + n, p)),
            win(kcol, -1), win(kcol, 0), win(kcol, 1), ctx(kcol),
            win(vcol, -1), win(vcol, 0), win(vcol, 1), ctx(vcol),
            pl.BlockSpec((8, 1, LANES), lambda b, p, n: (p, 0, 0)),
        ],
        out_specs=pl.BlockSpec((QB, 4 * LANES), lambda b, p, n: (b * nqb + n, p)),
        out_shape=jax.ShapeDtypeStruct((TL, D), BF16),
        compiler_params=_cp("arbitrary", "arbitrary", "arbitrary"),
        name="swa",
    )(qkv, qkv, qkv, qkv, qkv, qkv, qkv, qkv, qkv, sink_t)


@jax.jit
def _forward(x, c, ctx, c_ctx, ada_w, ada_b, ln_g, ln_b, mlp_w1, mlp_w2,
             lru_w_in, lru_conv_w, lru_conv_b, lru_gate_w, lru_gate_b, lru_lambda, lru_w_out,
             dif_w_qkv, dif_lambda, dif_subln, dif_w_out, ret_w_qkvg, ret_w_out,
             swa_w_qkv, swa_sink, swa_w_out):
    cvec = jnp.concatenate([c, c_ctx[None, :], jnp.zeros((NSEG - B - 1, D), F32)], 0)
    mods = _ada(cvec, ada_w, ada_b)
    tables = _rope_tables(TM)
    h, u = _modulate(x.reshape(TL, D), ctx.reshape(TC, D), mods, 0)
    w_outs = [lru_w_out[0], dif_w_out[0], ret_w_out[0], swa_w_out[0]]
    w1 = _cast_bf16(mlp_w1.reshape(DEPTH * D, D_FF))
    w2 = _cast_bf16(mlp_w2.reshape(DEPTH * D_FF, D))
    qscale = SWA_DH ** -0.5 * LOG2E
    for i in range(DEPTH):
        last = i == DEPTH - 1
        n_rows = TL if last else T
        if i == 0:
            p = _proj(u, lru_w_in[0], 1024)
            o = _lru(p, lru_conv_w[0], lru_conv_b[0], lru_gate_w[0], lru_gate_b[0], lru_lambda[0])
        elif i == 1:
            p = _proj(u, dif_w_qkv[0], 1024, rope=(D, 2 * D, qscale, tables))
            o = _dif_attention(p, dif_lambda[0], dif_subln[0], i)
        elif i == 2:
            p = _proj(u, ret_w_qkvg[0], 1024, qscale_cols=(D, RET_DK ** -0.5))
            o = _retention(p)
        else:
            p = _proj(u, swa_w_qkv[0], 512, rope=(SWA_H * SWA_DH, (SWA_H + SWA_KV) * SWA_DH, qscale, tables))
            o = _swa_attention(p, swa_sink[0])
        h, u2 = _outln(o, _cast_bf16(w_outs[i]), h, mods, i, ln_g[i, 0:1], ln_b[i, 0:1], n_rows)
        h, u = _mlp(u2, w1, w2, h, mods, i, ln_g[i, 1:2], ln_b[i, 1:2], n_rows)
    return h.reshape(B, S, D)


def kernel(x, c, ctx, c_ctx, ada_w, ada_b, ln_g, ln_b, mlp_w1, mlp_w2, lru_w_in, lru_conv_w, lru_conv_b,
           lru_gate_w, lru_gate_b, lru_lambda, lru_w_out, dif_w_qkv, dif_lambda, dif_subln, dif_w_out,
           ret_w_qkvg, ret_w_out, swa_w_qkv, swa_sink, swa_w_out):
    return _forward(x, c, ctx, c_ctx, ada_w, ada_b, ln_g, ln_b, mlp_w1, mlp_w2,
                    lru_w_in, lru_conv_w, lru_conv_b, lru_gate_w, lru_gate_b, lru_lambda, lru_w_out,
                    dif_w_qkv, dif_lambda, dif_subln, dif_w_out, ret_w_qkvg, ret_w_out,
                    swa_w_qkv, swa_sink, swa_w_out)
```

```python
import functools
import math

import jax
import jax.numpy as jnp
from jax import lax
from jax.experimental import pallas as pl
from jax.experimental.pallas import tpu as pltpu

F32 = jnp.float32
BF16 = jnp.bfloat16

D = 2048
B = 4
S = 2048
L = 256
DEPTH = 4
GRID_W = 64
TL = B * S
TC = B * L
T = TL + TC
NSEG = 8
ADA = 6
D_FF = 4 * D
ALPHA = (2 * DEPTH) ** 0.25
LN_EPS = 1e-5
ROPE_BASE = 10000.0
NEG_INF = -1e30

LRU_BW = 256
LRU_NB = D // LRU_BW
LRU_C = 8.0
DIF_H = 16
DIF_DV = 128
RET_H = 8
RET_DK = 256
RET_DV = 512
RET_CH = 128
SWA_H = 32
SWA_KV = 8
SWA_DH = 64
SWA_G = SWA_H // SWA_KV
WINDOW = 128
QB = 128

VMEM_LIMIT = 52 * 1024 * 1024
LANES = 128

TM = 1024
TM_LN = 512
TM_MLP = 1024
PROJ_SUB = 256
LN_SUB = 256
MLP_SUB = 1024
LOG2E = math.log2(math.e)


def _cp(*sem):
    return pltpu.CompilerParams(dimension_semantics=sem, vmem_limit_bytes=VMEM_LIMIT)


def _seg(i, tm):
    return jnp.minimum((i * tm) // S, B)


def _ada_kernel(c_ref, w_ref, b_ref, o_ref):
    c = c_ref[...]
    a = (c * jax.nn.sigmoid(c)).astype(BF16)
    o_ref[0] = jnp.dot(a, w_ref[0].astype(BF16), preferred_element_type=F32) + b_ref[0]


def _ada(cvec, ada_w, ada_b):
    tn = 1024
    out = pl.pallas_call(
        _ada_kernel,
        grid=(DEPTH, ADA * D // tn),
        in_specs=[
            pl.BlockSpec((NSEG, D), lambda l, j: (0, 0)),
            pl.BlockSpec((1, D, tn), lambda l, j: (l, 0, j)),
            pl.BlockSpec((1, 1, tn), lambda l, j: (l, 0, j)),
        ],
        out_specs=pl.BlockSpec((1, NSEG, tn), lambda l, j: (l, 0, j)),
        out_shape=jax.ShapeDtypeStruct((DEPTH, NSEG, ADA * D), F32),
        compiler_params=_cp("arbitrary", "arbitrary"),
        name="ada",
    )(cvec, ada_w, ada_b.reshape(DEPTH, 1, ADA * D))
    return out.reshape(DEPTH * NSEG * ADA, 1, D)


def _mod_spec(layer, chunk, tm, grid_pos):
    base = layer * NSEG * ADA + chunk

    def index_map(*g):
        return (base + _seg(g[grid_pos], tm) * ADA, 0, 0)

    return pl.BlockSpec((1, 1, D), index_map)


def _modulate_kernel(x_ref, ctx_ref, sc_ref, sh_ref, h_ref, u_ref, *, nlat):
    def emit(src_ref):
        v = src_ref[...]
        h_ref[...] = v
        u_ref[...] = (v * (1.0 + sc_ref[0]) + sh_ref[0]).astype(BF16)

    pl.when(pl.program_id(0) < nlat)(lambda: emit(x_ref))
    pl.when(pl.program_id(0) >= nlat)(lambda: emit(ctx_ref))


def _modulate(x, ctx, mods, layer):
    tm = TM_LN
    nlat = TL // tm
    return pl.pallas_call(
        functools.partial(_modulate_kernel, nlat=nlat),
        grid=(T // tm,),
        in_specs=[
            pl.BlockSpec((tm, D), lambda i: (jnp.minimum(i, nlat - 1), 0)),
            pl.BlockSpec((tm, D), lambda i: (jnp.maximum(i - nlat, 0), 0)),
            _mod_spec(layer, 1, tm, 0),
            _mod_spec(layer, 0, tm, 0),
        ],
        out_specs=[pl.BlockSpec((tm, D), lambda i: (i, 0))] * 2,
        out_shape=[jax.ShapeDtypeStruct((T, D), F32), jax.ShapeDtypeStruct((T, D), BF16)],
        compiler_params=_cp("arbitrary"),
        name="modulate",
    )(x, ctx, mods, mods)


def _cast_kernel(w_ref, o_ref):
    o_ref[...] = w_ref[...].astype(BF16)


CAST_BLOCK_ELEMS = 1 << 20


def _cast_bf16(w):
    r, c = w.shape
    tr = CAST_BLOCK_ELEMS // c
    return pl.pallas_call(
        _cast_kernel,
        grid=(r // tr,),
        in_specs=[pl.BlockSpec((tr, c), lambda i: (i, 0))],
        out_specs=pl.BlockSpec((tr, c), lambda i: (i, 0)),
        out_shape=jax.ShapeDtypeStruct((r, c), BF16),
        compiler_params=_cp("arbitrary"),
        name="cast",
    )(w)


def _rope_tables(tm):
    rows = S // GRID_W
    row = jnp.repeat(jnp.arange(rows, dtype=F32), GRID_W)
    col = jnp.tile(jnp.arange(GRID_W, dtype=F32), rows)
    half = SWA_DH // 2
    inv_freq = ROPE_BASE ** (-jnp.arange(0, half, 2, dtype=F32) / half)
    ang_r = row[:, None] * inv_freq[None, :]
    ang_c = col[:, None] * inv_freq[None, :]
    ang = jnp.concatenate([ang_r, ang_r, ang_c, ang_c], -1)
    cos, sin = jnp.cos(ang), jnp.sin(ang)
    first = (jnp.arange(SWA_DH) % 32) < 16
    sin_up = jnp.where(first[None, :], -sin, 0.0)
    sin_dn = jnp.where(first[None, :], 0.0, sin)

    def lay(t, ident):
        t = jnp.tile(t, (1, LANES // SWA_DH))
        return jnp.concatenate([t, jnp.full((tm, LANES), ident, F32)], 0)

    return lay(cos, 1.0), lay(sin_up, 0.0), lay(sin_dn, 0.0)


def _proj_kernel(a_ref, w_ref, *refs, rope, nq, nqk, qscale):
    if rope:
        cos_ref, up_ref, dn_ref, o_ref, wb_ref = refs
    else:
        o_ref, wb_ref = refs
    j = pl.program_id(0)
    i = pl.program_id(1)

    @pl.when(i == 0)
    def _():
        wb_ref[...] = w_ref[...].astype(BF16)

    tm, tn = o_ref.shape
    subs = [pl.ds(r * PROJ_SUB, PROJ_SUB) for r in range(tm // PROJ_SUB)]
    scale = jnp.where(j < nq, qscale, 1.0).astype(F32)

    def plain(scaled):
        for rows in subs:
            acc = jnp.dot(a_ref[rows, :], wb_ref[...], preferred_element_type=F32)
            o_ref[rows, :] = (acc * scale if scaled else acc).astype(o_ref.dtype)

    if not rope:
        plain(nq > 0)
        return

    @pl.when(j < nqk)
    def _():
        for rows in subs:
            acc = jnp.dot(a_ref[rows, :], wb_ref[...], preferred_element_type=F32)
            cos = cos_ref[rows, :] * scale
            up = up_ref[rows, :] * scale
            dn = dn_ref[rows, :] * scale
            for c in range(tn // LANES):
                x = acc[:, c * LANES:(c + 1) * LANES]
                y = x * cos + pltpu.roll(x, LANES - 16, 1) * up + pltpu.roll(x, 16, 1) * dn
                o_ref[rows, c * LANES:(c + 1) * LANES] = y.astype(o_ref.dtype)

    pl.when(j >= nqk)(lambda: plain(False))


def _proj(a, w, tn, rope=None, qscale_cols=None):
    n = w.shape[1]
    tm = TM
    in_specs = [
        pl.BlockSpec((tm, D), lambda j, i: (i, 0)),
        pl.BlockSpec((D, tn), lambda j, i: (0, j)),
    ]
    args = [a, w]
    kw = dict(rope=False, nq=0, nqk=0, qscale=1.0)
    if qscale_cols is not None:
        kw = dict(rope=False, nq=qscale_cols[0] // tn, nqk=0, qscale=qscale_cols[1])
    if rope is not None:
        nq_cols, nqk_cols, qscale, tables = rope
        nlat = TL // tm
        per_seq = S // tm

        def tmap(j, i):
            return (jnp.where(i < nlat, i % per_seq, per_seq), 0)

        in_specs += [pl.BlockSpec((tm, LANES), tmap)] * 3
        args += list(tables)
        kw = dict(rope=True, nq=nq_cols // tn, nqk=nqk_cols // tn, qscale=qscale)
    return pl.pallas_call(
        functools.partial(_proj_kernel, **kw),
        grid=(n // tn, T // tm),
        in_specs=in_specs,
        out_specs=pl.BlockSpec((tm, tn), lambda j, i: (i, j)),
        out_shape=jax.ShapeDtypeStruct((T, n), BF16),
        scratch_shapes=[pltpu.VMEM((D, tn), BF16)],
        compiler_params=_cp("arbitrary", "arbitrary"),
        name="proj",
    )(*args)


def _ln_epilogue(z, lng, lnb):
    mu = jnp.mean(z, -1, keepdims=True)
    zc = z - mu
    var = jnp.mean(zc * zc, -1, keepdims=True)
    return zc * lax.rsqrt(var + LN_EPS) * lng + lnb


def _residual_ln(rows, y, h_ref, g_ref, lng_ref, lnb_ref, sc_ref, sh_ref, hout_ref, uout_ref):
    hn = _ln_epilogue(ALPHA * h_ref[rows, :] + g_ref[0] * y, lng_ref[...], lnb_ref[...])
    hout_ref[rows, :] = hn
    if uout_ref is not None:
        uout_ref[rows, :] = (hn * (1.0 + sc_ref[0]) + sh_ref[0]).astype(BF16)


def _outln_kernel(*refs, nlat):
    if nlat is None:
        ol_ref, oc_ref = refs[0], None
        refs = refs[1:]
    else:
        ol_ref, oc_ref = refs[:2]
        refs = refs[2:]
    w_ref, h_ref, g_ref, lng_ref, lnb_ref, sc_ref, sh_ref, hout_ref, uout_ref = refs

    def body(o_ref):
        for r in range(o_ref.shape[0] // LN_SUB):
            rows = pl.ds(r * LN_SUB, LN_SUB)
            y = jnp.dot(o_ref[rows, :], w_ref[...], preferred_element_type=F32)
            _residual_ln(rows, y, h_ref, g_ref, lng_ref, lnb_ref, sc_ref, sh_ref, hout_ref, uout_ref)

    if nlat is None:
        body(ol_ref)
    else:
        pl.when(pl.program_id(0) < nlat)(lambda: body(ol_ref))
        pl.when(pl.program_id(0) >= nlat)(lambda: body(oc_ref))


def _outln(o_lat, o_ctx, w_bf16, h, mods, layer, lng, lnb):
    kin = o_lat.shape[1]
    tm = TM_LN * D // kin
    const = lambda i: (0, 0)
    nlat = TL // tm
    if o_ctx is None:
        n_rows = TL
        o_specs = [pl.BlockSpec((tm, kin), lambda i: (i, 0))]
        o_args = [o_lat]
    else:
        n_rows = T
        o_specs = [pl.BlockSpec((tm, kin), lambda i: (jnp.minimum(i, nlat - 1), 0)),
                   pl.BlockSpec((tm, kin), lambda i: (jnp.maximum(i - nlat, 0), 0))]
        o_args = [o_lat, o_ctx]
    return pl.pallas_call(
        functools.partial(_outln_kernel, nlat=None if o_ctx is None else nlat),
        grid=(n_rows // tm,),
        in_specs=o_specs + [
            pl.BlockSpec((kin, D), const, pipeline_mode=pl.Buffered(1)),
            pl.BlockSpec((tm, D), lambda i: (i, 0)),
            _mod_spec(layer, 2, tm, 0),
            pl.BlockSpec((1, D), const),
            pl.BlockSpec((1, D), const),
            _mod_spec(layer, 4, tm, 0),
            _mod_spec(layer, 3, tm, 0),
        ],
        out_specs=[pl.BlockSpec((tm, D), lambda i: (i, 0)), pl.BlockSpec((tm, D), lambda i: (i, 0))],
        out_shape=[jax.ShapeDtypeStruct((n_rows, D), F32), jax.ShapeDtypeStruct((n_rows, D), BF16)],
        compiler_params=_cp("arbitrary"),
        name="outln",
    )(*o_args, w_bf16, h, mods, lng, lnb, mods, mods)


def _mlp_kernel(u_ref, w1_ref, w2_ref, h_ref, g_ref, lng_ref, lnb_ref, *refs, emit_u):
    if emit_u:
        sc_ref, sh_ref, hout_ref, uout_ref = refs
    else:
        (hout_ref,) = refs
        sc_ref = sh_ref = uout_ref = None
    j = pl.program_id(1)
    tm = u_ref.shape[0]

    @pl.when(j == 0)
    def _():
        hout_ref[...] = jnp.zeros_like(hout_ref)

    subs = [pl.ds(r * MLP_SUB, MLP_SUB) for r in range(tm // MLP_SUB)]
    w1 = w1_ref[...].astype(BF16)
    w2 = w2_ref[...].astype(BF16)
    hids = []
    for rows in subs:
        hid = jnp.dot(u_ref[rows, :], w1, preferred_element_type=F32)
        hids.append(jnp.square(jnp.maximum(hid, 0.0)).astype(BF16))
    for rows, hid in zip(subs, hids):
        hout_ref[rows, :] += jnp.dot(hid, w2, preferred_element_type=F32)

    @pl.when(j == pl.num_programs(1) - 1)
    def _():
        for r in range(tm // LN_SUB):
            rows = pl.ds(r * LN_SUB, LN_SUB)
            _residual_ln(rows, hout_ref[rows, :], h_ref, g_ref, lng_ref, lnb_ref, sc_ref, sh_ref,
                         hout_ref, uout_ref)


def _mlp(u, w1, w2, h, mods, layer, lng, lnb, n_rows):
    tm = TM_MLP
    tf = 512
    emit_u = layer < DEPTH - 1
    const = lambda i, j: (0, 0)
    row = lambda i, j: (i, 0)
    nff = D_FF // tf
    once = pl.Buffered(1)
    in_specs = [
        pl.BlockSpec((tm, D), row, pipeline_mode=once),
        pl.BlockSpec((D, tf), lambda i, j: (layer, j)),
        pl.BlockSpec((tf, D), lambda i, j: (layer * nff + j, 0)),
        pl.BlockSpec((tm, D), row, pipeline_mode=once),
        _mod_spec(layer, 5, tm, 0),
        pl.BlockSpec((1, D), const),
        pl.BlockSpec((1, D), const),
    ]
    args = [u, w1, w2, h, mods, lng, lnb]
    out_specs = [pl.BlockSpec((tm, D), row, pipeline_mode=once)]
    out_shape = [jax.ShapeDtypeStruct((n_rows, D), F32)]
    if emit_u:
        in_specs += [_mod_spec(layer + 1, 1, tm, 0), _mod_spec(layer + 1, 0, tm, 0)]
        args += [mods, mods]
        out_specs.append(pl.BlockSpec((tm, D), row, pipeline_mode=once))
        out_shape.append(jax.ShapeDtypeStruct((n_rows, D), BF16))
    res = pl.pallas_call(
        functools.partial(_mlp_kernel, emit_u=emit_u),
        grid=(n_rows // tm, nff),
        in_specs=in_specs,
        out_specs=out_specs,
        out_shape=out_shape,
        compiler_params=_cp("arbitrary", "arbitrary"),
        name="mlp",
    )(*args)
    return (res[0], res[1]) if emit_u else (res[0], None)


def _shift_rows(x, k):
    n = x.shape[0]
    t = lax.broadcasted_iota(jnp.int32, x.shape, 0)
    y = pltpu.roll(x, (-k) % n, 0)
    return jnp.where((t + k >= 0) & (t + k < n), y, 0.0)


def _lru_conv(r_ref, cw, cb):
    x = r_ref[...].astype(F32)
    y = cb + cw[2:3] * x
    y = y + cw[0:1] * _shift_rows(x, -2)
    y = y + cw[1:2] * _shift_rows(x, -1)
    y = y + cw[3:4] * _shift_rows(x, 1)
    return y


def _local_scan(a, b, reverse):
    n = a.shape[0]
    sub = lax.broadcasted_iota(jnp.int32, a.shape, 0) % 8
    for d in (1, 2, 4):
        if reverse:
            ok = sub < 8 - d
            a_sh = pltpu.roll(a, n - d, 0)
            b_sh = pltpu.roll(b, n - d, 0)
        else:
            ok = sub >= d
            a_sh = pltpu.roll(a, d, 0)
            b_sh = pltpu.roll(b, d, 0)
        b = jnp.where(ok, a * b_sh + b, b)
        a = jnp.where(ok, a * a_sh, a)
    return a, b


LRU_CHUNK = 256
LRU_T = L + S


def _lru_kernel(gl_ref, gc_ref, rl_ref, rc_ref, cw_ref, cb_ref, wg_ref, bg_ref, lam_ref,
                ol_ref, oc_ref, x_ref, af_ref, bf_ref, ab_ref, bb_ref):
    cw = cw_ref[...]
    cb = cb_ref[...]
    x_ref[0:L, :] = _lru_conv(rc_ref, cw, cb)
    x_ref[L:LRU_T, :] = _lru_conv(rl_ref, cw, cb)

    lam = lam_ref[...]
    nsp = -(jnp.maximum(-lam, 0.0) + jnp.log1p(jnp.exp(-jnp.abs(lam))))
    wg = wg_ref[0]
    bg = bg_ref[...]

    for c in range(LRU_T // LRU_CHUNK):
        rows = pl.ds(c * LRU_CHUNK, LRU_CHUNK)
        xs = x_ref[rows, :]
        z = jnp.dot(xs.astype(BF16), wg, preferred_element_type=F32)
        for d, (a_ref, b_ref) in enumerate(((af_ref, bf_ref), (ab_ref, bb_ref))):
            r = jax.nn.sigmoid(z[:, (2 * d) * LRU_BW:(2 * d + 1) * LRU_BW] + bg[2 * d:2 * d + 1])
            g = jax.nn.sigmoid(z[:, (2 * d + 1) * LRU_BW:(2 * d + 2) * LRU_BW] + bg[2 * d + 1:2 * d + 2])
            log_a = LRU_C * r * nsp[d:d + 1]
            a = jnp.exp(log_a)
            b = jnp.sqrt(-jnp.tanh(log_a) * (a * a + 1.0)) * (g * xs)
            a, b = _local_scan(a, b, reverse=(d == 1))
            a_ref[rows, :] = a
            b_ref[rows, :] = b

    nblk = LRU_T // 8
    ncb = L // 8

    def fwd(k, h):
        rows = pl.ds(pl.multiple_of(k * 8, 8), 8)
        hb = bf_ref[rows, :] + af_ref[rows, :] * h
        bf_ref[rows, :] = hb
        return jnp.broadcast_to(hb[7:8, :], hb.shape)

    def bwd(k, h):
        rows = pl.ds(pl.multiple_of(k * 8, 8), 8)
        hb = bb_ref[rows, :] + ab_ref[rows, :] * h
        bb_ref[rows, :] = hb
        return jnp.broadcast_to(hb[0:1, :], hb.shape)

    h0 = jnp.zeros((8, LRU_BW), F32)
    lax.fori_loop(0, nblk, fwd, h0)
    hc = lax.fori_loop(0, ncb, lambda k, h: bwd(ncb - 1 - k, h), h0)
    lax.fori_loop(0, nblk - ncb, lambda k, h: bwd(nblk - 1 - k, h), hc)

    oc_ref[...] = (jax.nn.gelu(gc_ref[...].astype(F32)) * (bf_ref[0:L, :] + bb_ref[0:L, :])).astype(BF16)
    ol_ref[...] = (jax.nn.gelu(gl_ref[...].astype(F32)) * (bf_ref[L:LRU_T, :] + bb_ref[L:LRU_T, :])).astype(BF16)


def _lru(p, conv_w, conv_b, gate_w, gate_b, lam):
    wg = jnp.transpose(gate_w, (2, 3, 0, 1, 4)).reshape(LRU_NB, LRU_BW, 4 * LRU_BW).astype(BF16)
    bg = gate_b.reshape(4, D)
    nlb = S // L
    ctx0 = TL // L
    lat = pl.BlockSpec((S, LRU_BW), lambda b, n: (b, n))
    ctx = pl.BlockSpec((L, LRU_BW), lambda b, n: (ctx0 + b, n))
    out_lat, out_ctx = pl.pallas_call(
        _lru_kernel,
        grid=(B, LRU_NB),
        in_specs=[
            lat, ctx,
            pl.BlockSpec((S, LRU_BW), lambda b, n: (b, LRU_NB + n)),
            pl.BlockSpec((L, LRU_BW), lambda b, n: (ctx0 + b, LRU_NB + n)),
            pl.BlockSpec((4, LRU_BW), lambda b, n: (0, n)),
            pl.BlockSpec((1, LRU_BW), lambda b, n: (0, n)),
            pl.BlockSpec((1, LRU_BW, 4 * LRU_BW), lambda b, n: (n, 0, 0)),
            pl.BlockSpec((4, LRU_BW), lambda b, n: (0, n)),
            pl.BlockSpec((2, LRU_BW), lambda b, n: (0, n)),
        ],
        out_specs=[lat, pl.BlockSpec((L, LRU_BW), lambda b, n: (b, n))],
        out_shape=[jax.ShapeDtypeStruct((TL, D), BF16), jax.ShapeDtypeStruct((TC, D), BF16)],
        scratch_shapes=[pltpu.VMEM((LRU_T, LRU_BW), F32)] * 5,
        compiler_params=_cp("arbitrary", "arbitrary"),
        name="lru",
    )(p, p, p, p, conv_w, conv_b.reshape(1, D), wg, bg, lam)
    del nlb
    return out_lat, out_ctx


def _dif_kernel(*refs, n_kv, lam_init):
    q_ref = refs[0]
    kv = refs[1:1 + 2 * n_kv]
    lam_ref, sub_ref, o_ref, kcat_ref, vt_ref = refs[1 + 2 * n_kv:]

    @pl.when(pl.program_id(2) == 0)
    def _():
        off = 0
        for t in range(n_kv):
            n = kv[2 * t].shape[0]
            kcat_ref[off:off + n, :] = kv[2 * t][...]
            vt_ref[0:DIF_DV, off:off + n] = kv[2 * t + 1][...].astype(F32).T.astype(BF16)
            off += n
        vt_ref[DIF_DV:, :] = jnp.ones((vt_ref.shape[0] - DIF_DV, vt_ref.shape[1]), BF16)

    lam = lam_ref[...]
    lam_full = (jnp.exp(jnp.sum(lam[0:1] * lam[1:2], -1, keepdims=True))
                - jnp.exp(jnp.sum(lam[2:3] * lam[3:4], -1, keepdims=True)) + lam_init)
    q = q_ref[...]
    lane = lax.broadcasted_iota(jnp.int32, q.shape, 1)
    st = []
    for m in range(2):
        qm = jnp.where((lane >= 64 * m) & (lane < 64 * (m + 1)), q, jnp.zeros_like(q))
        st.append(lax.dot_general(kcat_ref[...], qm, (((1,), (1,)), ((), ())), preferred_element_type=F32))
    outs = []
    for s in st:
        e = jnp.exp2(s - jnp.max(s, 0, keepdims=True)).astype(BF16)
        r = jnp.dot(vt_ref[...], e, preferred_element_type=F32)
        outs.append(r[:DIF_DV, :] / r[DIF_DV:DIF_DV + 1, :])
    ot = outs[0] - lam_full * outs[1]
    yt = ot * lax.rsqrt(jnp.mean(ot * ot, 0, keepdims=True) + LN_EPS)
    o_ref[...] = (yt.T * sub_ref[...] * (1.0 - lam_init)).astype(BF16)


def _dif_attention(qkv, lam, subln, layer_idx):
    lam_init = 0.8 - 0.6 * math.exp(-0.3 * layer_idx)
    tq = 1024
    nh = DIF_H
    kern = functools.partial(_dif_kernel, lam_init=lam_init)
    small = [pl.BlockSpec((4, 64), lambda *g: (0, 0)), pl.BlockSpec((1, DIF_DV), lambda *g: (0, 0))]
    sub = subln.reshape(1, DIF_DV)
    ctx0 = TL // L
    sem = _cp("arbitrary", "arbitrary", "arbitrary")
    ones_rows = 16
    scratch = lambda nk: [pltpu.VMEM((nk, DIF_DV), BF16), pltpu.VMEM((DIF_DV + ones_rows, nk), BF16)]
    o_lat = pl.pallas_call(
        functools.partial(kern, n_kv=2),
        grid=(B, nh, S // tq),
        in_specs=[
            pl.BlockSpec((tq, DIF_DV), lambda b, h, i: (b * (S // tq) + i, h)),
            pl.BlockSpec((S, DIF_DV), lambda b, h, i: (b, nh + h)),
            pl.BlockSpec((S, DIF_DV), lambda b, h, i: (b, 2 * nh + h)),
            pl.BlockSpec((L, DIF_DV), lambda b, h, i: (ctx0 + b, nh + h)),
            pl.BlockSpec((L, DIF_DV), lambda b, h, i: (ctx0 + b, 2 * nh + h)),
        ] + small,
        out_specs=pl.BlockSpec((tq, DIF_DV), lambda b, h, i: (b * (S // tq) + i, h)),
        out_shape=jax.ShapeDtypeStruct((TL, D), BF16),
        scratch_shapes=scratch(S + L),
        compiler_params=sem,
        name="dif_lat",
    )(qkv, qkv, qkv, qkv, qkv, lam, sub)
    o_ctx = pl.pallas_call(
        functools.partial(kern, n_kv=1),
        grid=(B, nh, 1),
        in_specs=[
            pl.BlockSpec((L, DIF_DV), lambda b, h, i: (ctx0 + b, h)),
            pl.BlockSpec((L, DIF_DV), lambda b, h, i: (ctx0 + b, nh + h)),
            pl.BlockSpec((L, DIF_DV), lambda b, h, i: (ctx0 + b, 2 * nh + h)),
        ] + small,
        out_specs=pl.BlockSpec((L, DIF_DV), lambda b, h, i: (b, h)),
        out_shape=jax.ShapeDtypeStruct((TC, D), BF16),
        scratch_shapes=scratch(L),
        compiler_params=sem,
        name="dif_ctx",
    )(qkv, qkv, qkv, lam, sub)
    return o_lat, o_ctx


def _ret_kernel(ql_ref, qc_ref, kl_ref, kc_ref, vl_ref, vc_ref, gl_ref, gc_ref, lg_ref,
                ol_ref, oc_ref, rf_ref, rb_ref, il_ref, ic_ref):
    C = RET_CH
    lg = lg_ref[0]
    lgf = lg[0:1, :]
    lgb = lg[1:2, :]
    row = lax.broadcasted_iota(jnp.int32, (C, C), 0).astype(F32)
    col = lax.broadcasted_iota(jnp.int32, (C, C), 1).astype(F32)
    rel = row - col
    decay = jnp.exp(jnp.abs(rel) * jnp.where(rel >= 0, lgf, lgb))
    wide = lambda t, n: jnp.concatenate([t] * n, axis=1)
    qdec_f = wide(jnp.exp((row + 1.0) * lgf), RET_DK // C)
    kdec_f = wide(jnp.exp((C - 1.0 - row) * lgf), RET_DK // C)
    qdec_b = wide(jnp.exp((C - row) * lgb), RET_DK // C)
    kdec_b = wide(jnp.exp(row * lgb), RET_DK // C)
    cdec_f = wide(jnp.exp(C * lgf), RET_DV // C)
    cdec_b = wide(jnp.exp(C * lgb), RET_DV // C)

    tdot = lambda a, b: lax.dot_general(a, b, (((0,), (0,)), ((), ())), preferred_element_type=F32)

    def bwd_chunk(q_ref, k_ref, v_ref, i_ref, rows):
        q = q_ref[rows, :].astype(F32)
        k = k_ref[rows, :].astype(F32)
        v = v_ref[rows, :]
        i_ref[rows, :] = jnp.dot((q * qdec_b).astype(BF16), rb_ref[...].astype(BF16),
                                 preferred_element_type=F32)
        rb_ref[...] = cdec_b * rb_ref[...] + tdot((k * kdec_b).astype(BF16), v)

    def fwd_chunk(q_ref, k_ref, v_ref, g_ref, i_ref, o_ref, rows):
        qb = q_ref[rows, :]
        kb = k_ref[rows, :]
        v = v_ref[rows, :]
        q = qb.astype(F32)
        k = kb.astype(F32)
        s = lax.dot_general(qb, kb, (((1,), (1,)), ((), ())), preferred_element_type=F32) * decay
        o = jnp.dot(s.astype(BF16), v, preferred_element_type=F32)
        o = o + jnp.dot((q * qdec_f).astype(BF16), rf_ref[...].astype(BF16), preferred_element_type=F32)
        o = o + i_ref[rows, :]
        rf_ref[...] = cdec_f * rf_ref[...] + tdot((k * kdec_f).astype(BF16), v)
        o = o * lax.rsqrt(jnp.mean(o * o, -1, keepdims=True) + LN_EPS)
        g = g_ref[rows, :].astype(F32)
        o_ref[rows, :] = (g * jax.nn.sigmoid(g) * o).astype(BF16)

    rf_ref[...] = jnp.zeros_like(rf_ref)
    rb_ref[...] = jnp.zeros_like(rb_ref)
    nlc = S // C
    ncc = L // C
    dyn = lambda n: pl.ds(pl.multiple_of(n * C, C), C)

    for n in reversed(range(ncc)):
        bwd_chunk(qc_ref, kc_ref, vc_ref, ic_ref, pl.ds(n * C, C))

    def bl(n, carry):
        bwd_chunk(ql_ref, kl_ref, vl_ref, il_ref, dyn(nlc - 1 - n))
        return carry

    lax.fori_loop(0, nlc, bl, 0)

    for n in range(ncc):
        fwd_chunk(qc_ref, kc_ref, vc_ref, gc_ref, ic_ref, oc_ref, pl.ds(n * C, C))

    def fl(n, carry):
        fwd_chunk(ql_ref, kl_ref, vl_ref, gl_ref, il_ref, ol_ref, dyn(n))
        return carry

    lax.fori_loop(0, nlc, fl, 0)


def _retention(p):
    log_gf = jnp.log1p(-jnp.exp2(-5.0 - jnp.arange(RET_H, dtype=F32)))
    lg = jnp.zeros((RET_H, 8, LANES), F32)
    lg = lg.at[:, 0, :].set(log_gf[:, None]).at[:, 1, :].set(log_gf[::-1][:, None])
    ctx0 = TL // L
    nk = D // RET_DK
    nv = 2 * D // RET_DV
    lat = lambda w, off: pl.BlockSpec((S, w), lambda b, h: (b, off + h))
    ctx = lambda w, off: pl.BlockSpec((L, w), lambda b, h: (ctx0 + b, off + h))
    o_lat, o_ctx = pl.pallas_call(
        _ret_kernel,
        grid=(B, RET_H),
        in_specs=[
            lat(RET_DK, 0), ctx(RET_DK, 0),
            lat(RET_DK, nk), ctx(RET_DK, nk),
            lat(RET_DV, nv), ctx(RET_DV, nv),
            lat(RET_DV, nv + RET_H), ctx(RET_DV, nv + RET_H),
            pl.BlockSpec((1, 8, LANES), lambda b, h: (h, 0, 0)),
        ],
        out_specs=[pl.BlockSpec((S, RET_DV), lambda b, h: (b, h)),
                   pl.BlockSpec((L, RET_DV), lambda b, h: (b, h))],
        out_shape=[jax.ShapeDtypeStruct((TL, 2 * D), BF16), jax.ShapeDtypeStruct((TC, 2 * D), BF16)],
        scratch_shapes=[
            pltpu.VMEM((RET_DK, RET_DV), F32), pltpu.VMEM((RET_DK, RET_DV), F32),
            pltpu.VMEM((S, RET_DV), F32), pltpu.VMEM((L, RET_DV), F32),
        ],
        compiler_params=_cp("arbitrary", "arbitrary"),
        name="retention",
    )(p, p, p, p, p, p, p, p, lg)
    return o_lat, o_ctx


def _swa_kernel(q_ref, kp_ref, kc_ref, kn_ref, kx_ref, vp_ref, vc_ref, vn_ref, vx_ref, sink_ref, o_ref):
    n = pl.program_id(2)
    nb = pl.num_programs(2)
    k2 = jnp.concatenate([kp_ref[...], kc_ref[...], kn_ref[...], kx_ref[...]], axis=0)
    v2 = jnp.concatenate([vp_ref[...], vc_ref[...], vn_ref[...], vx_ref[...]], axis=0)
    nk = k2.shape[0]
    v2 = jnp.concatenate([v2, jnp.ones((nk, LANES), BF16)], axis=1)
    qi = lax.broadcasted_iota(jnp.int32, (QB, nk), 0)
    kj = lax.broadcasted_iota(jnp.int32, (QB, nk), 1)
    band = (kj >= qi) & (kj <= qi + 2 * WINDOW)
    band = band & ((kj >= QB) | (n > 0)) & ((kj < 2 * QB) | (n < nb - 1))
    valid = band | (kj >= 3 * QB)
    bias = jnp.where(valid, 0.0, NEG_INF).astype(F32)
    bias = jnp.concatenate([bias] * SWA_G, axis=0)
    lane = lax.broadcasted_iota(jnp.int32, (QB, LANES), 1)
    lo = lane < SWA_DH
    qcols = [q_ref[:, c * LANES:(c + 1) * LANES].astype(F32) for c in range(4)]
    qswap = [pltpu.roll(x, SWA_DH, 1) for x in qcols]
    out_cols = [None] * 4
    for t in range(2):
        keep = lo if t == 0 else ~lo
        heads = [(2 * t + g // 2, g % 2) for g in range(SWA_G)]
        qs = jnp.concatenate(
            [jnp.where(keep, qcols[c] if hf == t else qswap[c], 0.0).astype(BF16) for c, hf in heads], axis=0)
        s = lax.dot_general(qs, k2, (((1,), (1,)), ((), ())), preferred_element_type=F32) + bias
        sink = jnp.concatenate(
            [jnp.broadcast_to(sink_ref[2 * c + hf][:, 0:1] * LOG2E, (QB, 1)) for c, hf in heads], axis=0)
        mx = jnp.maximum(jnp.max(s, -1, keepdims=True), sink)
        e = jnp.exp2(s - mx).astype(BF16)
        r = jnp.dot(e, v2, preferred_element_type=F32)
        o = r[:, :LANES] / (r[:, LANES:] + jnp.exp2(sink - mx))
        for g, (c, hf) in enumerate(heads):
            og = o[g * QB:(g + 1) * QB, :]
            if hf != t:
                og = pltpu.roll(og, SWA_DH, 1)
            og = jnp.where(lo if hf == 0 else ~lo, og, 0.0)
            out_cols[c] = og if out_cols[c] is None else out_cols[c] + og
    for c in range(4):
        o_ref[:, c * LANES:(c + 1) * LANES] = out_cols[c].astype(BF16)


def _swa_attention(qkv, sink):
    nqb = S // QB
    kcol = SWA_H * SWA_DH // LANES
    vcol = kcol + SWA_KV * SWA_DH // LANES
    ctx0 = TL // L
    sink_t = jnp.broadcast_to(sink.astype(F32)[:, None, None], (SWA_H, 1, LANES))

    def win(col0, d):
        def index_map(b, p, n):
            return (b * nqb + jnp.clip(n + d, 0, nqb - 1), col0 + p)
        return pl.BlockSpec((QB, LANES), index_map)

    ctx = lambda col0: pl.BlockSpec((L, LANES), lambda b, p, n: (ctx0 + b, col0 + p))
    return pl.pallas_call(
        _swa_kernel,
        grid=(B, SWA_KV // 2, nqb),
        in_specs=[
            pl.BlockSpec((QB, 4 * LANES), lambda b, p, n: (b * nqb + n, p)),
            win(kcol, -1), win(kcol, 0), win(kcol, 1), ctx(kcol),
            win(vcol, -1), win(vcol, 0), win(vcol, 1), ctx(vcol),
            pl.BlockSpec((8, 1, LANES), lambda b, p, n: (p, 0, 0)),
        ],
        out_specs=pl.BlockSpec((QB, 4 * LANES), lambda b, p, n: (b * nqb + n, p)),
        out_shape=jax.ShapeDtypeStruct((TL, D), BF16),
        compiler_params=_cp("arbitrary", "arbitrary", "arbitrary"),
        name="swa",
    )(qkv, qkv, qkv, qkv, qkv, qkv, qkv, qkv, qkv, sink_t)


@jax.jit
def _forward(x, c, ctx, c_ctx, ada_w, ada_b, ln_g, ln_b, mlp_w1, mlp_w2,
             lru_w_in, lru_conv_w, lru_conv_b, lru_gate_w, lru_gate_b, lru_lambda, lru_w_out,
             dif_w_qkv, dif_lambda, dif_subln, dif_w_out, ret_w_qkvg, ret_w_out,
             swa_w_qkv, swa_sink, swa_w_out):
    cvec = jnp.concatenate([c, c_ctx[None, :], jnp.zeros((NSEG - B - 1, D), F32)], 0)
    mods = _ada(cvec, ada_w, ada_b)
    tables = _rope_tables(TM)
    h, u = _modulate(x.reshape(TL, D), ctx.reshape(TC, D), mods, 0)
    w_outs = [lru_w_out[0], dif_w_out[0], ret_w_out[0], swa_w_out[0]]
    w1 = mlp_w1.reshape(DEPTH * D, D_FF)
    w2 = mlp_w2.reshape(DEPTH * D_FF, D)
    qscale = SWA_DH ** -0.5 * LOG2E
    for i in range(DEPTH):
        last = i == DEPTH - 1
        n_rows = TL if last else T
        if i == 0:
            p = _proj(u, lru_w_in[0], 1024)
            o, oc = _lru(p, lru_conv_w[0], lru_conv_b[0], lru_gate_w[0], lru_gate_b[0], lru_lambda[0])
        elif i == 1:
            p = _proj(u, dif_w_qkv[0], 1024, rope=(D, 2 * D, qscale, tables))
            o, oc = _dif_attention(p, dif_lambda[0], dif_subln[0], i)
        elif i == 2:
            p = _proj(u, ret_w_qkvg[0], 1024, qscale_cols=(D, RET_DK ** -0.5))
            o, oc = _retention(p)
        else:
            p = _proj(u, swa_w_qkv[0], 512, rope=(SWA_H * SWA_DH, (SWA_H + SWA_KV) * SWA_DH, qscale, tables))
            o, oc = _swa_attention(p, swa_sink[0]), None
        h, u2 = _outln(o, oc, _cast_bf16(w_outs[i]), h, mods, i, ln_g[i, 0:1], ln_b[i, 0:1])
        h, u = _mlp(u2, w1, w2, h, mods, i, ln_g[i, 1:2], ln_b[i, 1:2], n_rows)
    return h.reshape(B, S, D)


def kernel(x, c, ctx, c_ctx, ada_w, ada_b, ln_g, ln_b, mlp_w1, mlp_w2, lru_w_in, lru_conv_w, lru_conv_b,
           lru_gate_w, lru_gate_b, lru_lambda, lru_w_out, dif_w_qkv, dif_lambda, dif_subln, dif_w_out,
           ret_w_qkvg, ret_w_out, swa_w_qkv, swa_sink, swa_w_out):
    return _forward(x, c, ctx, c_ctx, ada_w, ada_b, ln_g, ln_b, mlp_w1, mlp_w2,
                    lru_w_in, lru_conv_w, lru_conv_b, lru_gate_w, lru_gate_b, lru_lambda, lru_w_out,
                    dif_w_qkv, dif_lambda, dif_subln, dif_w_out, ret_w_qkvg, ret_w_out,
                    swa_w_qkv, swa_sink, swa_w_out)
```

```python
import functools
import math

import jax
import jax.numpy as jnp
from jax import lax
from jax.experimental import pallas as pl
from jax.experimental.pallas import tpu as pltpu

F32 = jnp.float32
BF16 = jnp.bfloat16

D = 2048
B = 4
S = 2048
L = 256
DEPTH = 4
GRID_W = 64
TL = B * S
TC = B * L
T = TL + TC
NSEG = 8
ADA = 6
D_FF = 4 * D
ALPHA = (2 * DEPTH) ** 0.25
LN_EPS = 1e-5
ROPE_BASE = 10000.0
NEG_INF = -1e30

LRU_BW = 256
LRU_NB = D // LRU_BW
LRU_C = 8.0
DIF_H = 16
DIF_DV = 128
RET_H = 8
RET_DK = 256
RET_DV = 512
RET_CH = 512
SWA_H = 32
SWA_KV = 8
SWA_DH = 64
SWA_G = SWA_H // SWA_KV
WINDOW = 128
QB = 128

VMEM_LIMIT = 52 * 1024 * 1024
LANES = 128

TM = 1024
TM_LN = 512
TM_MLP = 1024
PROJ_SUB = 256
LN_SUB = 256
MLP_SUB = 1024
LOG2E = math.log2(math.e)


def _cp(*sem):
    return pltpu.CompilerParams(dimension_semantics=sem, vmem_limit_bytes=VMEM_LIMIT)


def _seg(i, tm):
    return jnp.minimum((i * tm) // S, B)


def _ada_kernel(c_ref, w_ref, b_ref, o_ref):
    c = c_ref[...]
    a = (c * jax.nn.sigmoid(c)).astype(BF16)
    o_ref[0] = jnp.dot(a, w_ref[0].astype(BF16), preferred_element_type=F32) + b_ref[0]


def _ada(cvec, ada_w, ada_b):
    tn = 1024
    out = pl.pallas_call(
        _ada_kernel,
        grid=(DEPTH, ADA * D // tn),
        in_specs=[
            pl.BlockSpec((NSEG, D), lambda l, j: (0, 0)),
            pl.BlockSpec((1, D, tn), lambda l, j: (l, 0, j)),
            pl.BlockSpec((1, 1, tn), lambda l, j: (l, 0, j)),
        ],
        out_specs=pl.BlockSpec((1, NSEG, tn), lambda l, j: (l, 0, j)),
        out_shape=jax.ShapeDtypeStruct((DEPTH, NSEG, ADA * D), F32),
        compiler_params=_cp("arbitrary", "arbitrary"),
        name="ada",
    )(cvec, ada_w, ada_b.reshape(DEPTH, 1, ADA * D))
    return out.reshape(DEPTH * NSEG * ADA, 1, D)


def _mod_spec(layer, chunk, tm, grid_pos):
    base = layer * NSEG * ADA + chunk

    def index_map(*g):
        return (base + _seg(g[grid_pos], tm) * ADA, 0, 0)

    return pl.BlockSpec((1, 1, D), index_map)


def _modulate_kernel(x_ref, ctx_ref, sc_ref, sh_ref, h_ref, u_ref, *, nlat):
    def emit(src_ref):
        v = src_ref[...]
        h_ref[...] = v
        u_ref[...] = (v * (1.0 + sc_ref[0]) + sh_ref[0]).astype(BF16)

    pl.when(pl.program_id(0) < nlat)(lambda: emit(x_ref))
    pl.when(pl.program_id(0) >= nlat)(lambda: emit(ctx_ref))


def _modulate(x, ctx, mods, layer):
    tm = TM_LN
    nlat = TL // tm
    return pl.pallas_call(
        functools.partial(_modulate_kernel, nlat=nlat),
        grid=(T // tm,),
        in_specs=[
            pl.BlockSpec((tm, D), lambda i: (jnp.minimum(i, nlat - 1), 0)),
            pl.BlockSpec((tm, D), lambda i: (jnp.maximum(i - nlat, 0), 0)),
            _mod_spec(layer, 1, tm, 0),
            _mod_spec(layer, 0, tm, 0),
        ],
        out_specs=[pl.BlockSpec((tm, D), lambda i: (i, 0))] * 2,
        out_shape=[jax.ShapeDtypeStruct((T, D), F32), jax.ShapeDtypeStruct((T, D), BF16)],
        compiler_params=_cp("arbitrary"),
        name="modulate",
    )(x, ctx, mods, mods)


def _cast_kernel(w_ref, o_ref):
    o_ref[...] = w_ref[...].astype(BF16)


CAST_BLOCK_ELEMS = 1 << 20


def _cast_bf16(w):
    r, c = w.shape
    tr = CAST_BLOCK_ELEMS // c
    return pl.pallas_call(
        _cast_kernel,
        grid=(r // tr,),
        in_specs=[pl.BlockSpec((tr, c), lambda i: (i, 0))],
        out_specs=pl.BlockSpec((tr, c), lambda i: (i, 0)),
        out_shape=jax.ShapeDtypeStruct((r, c), BF16),
        compiler_params=_cp("arbitrary"),
        name="cast",
    )(w)


def _rope_tables(tm):
    rows = S // GRID_W
    row = jnp.repeat(jnp.arange(rows, dtype=F32), GRID_W)
    col = jnp.tile(jnp.arange(GRID_W, dtype=F32), rows)
    half = SWA_DH // 2
    inv_freq = ROPE_BASE ** (-jnp.arange(0, half, 2, dtype=F32) / half)
    ang_r = row[:, None] * inv_freq[None, :]
    ang_c = col[:, None] * inv_freq[None, :]
    ang = jnp.concatenate([ang_r, ang_r, ang_c, ang_c], -1)
    cos, sin = jnp.cos(ang), jnp.sin(ang)
    first = (jnp.arange(SWA_DH) % 32) < 16
    sin_up = jnp.where(first[None, :], -sin, 0.0)
    sin_dn = jnp.where(first[None, :], 0.0, sin)

    def lay(t, ident):
        t = jnp.tile(t, (1, LANES // SWA_DH))
        return jnp.concatenate([t, jnp.full((tm, LANES), ident, F32)], 0)

    return lay(cos, 1.0), lay(sin_up, 0.0), lay(sin_dn, 0.0)


def _proj_kernel(a_ref, w_ref, *refs, rope, nq, nqk, qscale):
    if rope:
        cos_ref, up_ref, dn_ref, o_ref, wb_ref = refs
    else:
        o_ref, wb_ref = refs
    j = pl.program_id(0)
    i = pl.program_id(1)

    @pl.when(i == 0)
    def _():
        wb_ref[...] = w_ref[...].astype(BF16)

    tm, tn = o_ref.shape
    subs = [pl.ds(r * PROJ_SUB, PROJ_SUB) for r in range(tm // PROJ_SUB)]
    scale = jnp.where(j < nq, qscale, 1.0).astype(F32)

    def plain(scaled):
        for rows in subs:
            acc = jnp.dot(a_ref[rows, :], wb_ref[...], preferred_element_type=F32)
            o_ref[rows, :] = (acc * scale if scaled else acc).astype(o_ref.dtype)

    if not rope:
        plain(nq > 0)
        return

    @pl.when(j < nqk)
    def _():
        for rows in subs:
            acc = jnp.dot(a_ref[rows, :], wb_ref[...], preferred_element_type=F32)
            cos = cos_ref[rows, :] * scale
            up = up_ref[rows, :] * scale
            dn = dn_ref[rows, :] * scale
            for c in range(tn // LANES):
                x = acc[:, c * LANES:(c + 1) * LANES]
                y = x * cos + pltpu.roll(x, LANES - 16, 1) * up + pltpu.roll(x, 16, 1) * dn
                o_ref[rows, c * LANES:(c + 1) * LANES] = y.astype(o_ref.dtype)

    pl.when(j >= nqk)(lambda: plain(False))


def _proj(a, w, tn, rope=None, qscale_cols=None):
    n = w.shape[1]
    tm = TM
    in_specs = [
        pl.BlockSpec((tm, D), lambda j, i: (i, 0)),
        pl.BlockSpec((D, tn), lambda j, i: (0, j)),
    ]
    args = [a, w]
    kw = dict(rope=False, nq=0, nqk=0, qscale=1.0)
    if qscale_cols is not None:
        kw = dict(rope=False, nq=qscale_cols[0] // tn, nqk=0, qscale=qscale_cols[1])
    if rope is not None:
        nq_cols, nqk_cols, qscale, tables = rope
        nlat = TL // tm
        per_seq = S // tm

        def tmap(j, i):
            return (jnp.where(i < nlat, i % per_seq, per_seq), 0)

        in_specs += [pl.BlockSpec((tm, LANES), tmap)] * 3
        args += list(tables)
        kw = dict(rope=True, nq=nq_cols // tn, nqk=nqk_cols // tn, qscale=qscale)
    return pl.pallas_call(
        functools.partial(_proj_kernel, **kw),
        grid=(n // tn, T // tm),
        in_specs=in_specs,
        out_specs=pl.BlockSpec((tm, tn), lambda j, i: (i, j)),
        out_shape=jax.ShapeDtypeStruct((T, n), BF16),
        scratch_shapes=[pltpu.VMEM((D, tn), BF16)],
        compiler_params=_cp("arbitrary", "arbitrary"),
        name="proj",
    )(*args)


def _ln_epilogue(z, lng, lnb):
    mu = jnp.mean(z, -1, keepdims=True)
    zc = z - mu
    var = jnp.mean(zc * zc, -1, keepdims=True)
    return zc * lax.rsqrt(var + LN_EPS) * lng + lnb


def _residual_ln(rows, y, h_ref, g_ref, lng_ref, lnb_ref, sc_ref, sh_ref, hout_ref, uout_ref):
    hn = _ln_epilogue(ALPHA * h_ref[rows, :] + g_ref[0] * y, lng_ref[...], lnb_ref[...])
    hout_ref[rows, :] = hn
    if uout_ref is not None:
        uout_ref[rows, :] = (hn * (1.0 + sc_ref[0]) + sh_ref[0]).astype(BF16)


def _outln_kernel(*refs, nlat):
    if nlat is None:
        ol_ref, oc_ref = refs[0], None
        refs = refs[1:]
    else:
        ol_ref, oc_ref = refs[:2]
        refs = refs[2:]
    w_ref, h_ref, g_ref, lng_ref, lnb_ref, sc_ref, sh_ref, hout_ref, uout_ref = refs

    def body(o_ref):
        for r in range(o_ref.shape[0] // LN_SUB):
            rows = pl.ds(r * LN_SUB, LN_SUB)
            y = jnp.dot(o_ref[rows, :], w_ref[...], preferred_element_type=F32)
            _residual_ln(rows, y, h_ref, g_ref, lng_ref, lnb_ref, sc_ref, sh_ref, hout_ref, uout_ref)

    if nlat is None:
        body(ol_ref)
    else:
        pl.when(pl.program_id(0) < nlat)(lambda: body(ol_ref))
        pl.when(pl.program_id(0) >= nlat)(lambda: body(oc_ref))


def _outln(o_lat, o_ctx, w_bf16, h, mods, layer, lng, lnb):
    kin = o_lat.shape[1]
    tm = TM_LN * D // kin
    const = lambda i: (0, 0)
    nlat = TL // tm
    if o_ctx is None:
        n_rows = TL
        o_specs = [pl.BlockSpec((tm, kin), lambda i: (i, 0))]
        o_args = [o_lat]
    else:
        n_rows = T
        o_specs = [pl.BlockSpec((tm, kin), lambda i: (jnp.minimum(i, nlat - 1), 0)),
                   pl.BlockSpec((tm, kin), lambda i: (jnp.maximum(i - nlat, 0), 0))]
        o_args = [o_lat, o_ctx]
    return pl.pallas_call(
        functools.partial(_outln_kernel, nlat=None if o_ctx is None else nlat),
        grid=(n_rows // tm,),
        in_specs=o_specs + [
            pl.BlockSpec((kin, D), const, pipeline_mode=pl.Buffered(1)),
            pl.BlockSpec((tm, D), lambda i: (i, 0)),
            _mod_spec(layer, 2, tm, 0),
            pl.BlockSpec((1, D), const),
            pl.BlockSpec((1, D), const),
            _mod_spec(layer, 4, tm, 0),
            _mod_spec(layer, 3, tm, 0),
        ],
        out_specs=[pl.BlockSpec((tm, D), lambda i: (i, 0)), pl.BlockSpec((tm, D), lambda i: (i, 0))],
        out_shape=[jax.ShapeDtypeStruct((n_rows, D), F32), jax.ShapeDtypeStruct((n_rows, D), BF16)],
        compiler_params=_cp("arbitrary"),
        name="outln",
    )(*o_args, w_bf16, h, mods, lng, lnb, mods, mods)


def _mlp_kernel(u_ref, w1_ref, w2_ref, h_ref, g_ref, lng_ref, lnb_ref, *refs, emit_u):
    if emit_u:
        sc_ref, sh_ref, hout_ref, uout_ref = refs
    else:
        (hout_ref,) = refs
        sc_ref = sh_ref = uout_ref = None
    j = pl.program_id(1)
    tm = u_ref.shape[0]

    @pl.when(j == 0)
    def _():
        hout_ref[...] = jnp.zeros_like(hout_ref)

    subs = [pl.ds(r * MLP_SUB, MLP_SUB) for r in range(tm // MLP_SUB)]
    w1 = w1_ref[...].astype(BF16)
    w2 = w2_ref[...].astype(BF16)
    hids = []
    for rows in subs:
        hid = jnp.dot(u_ref[rows, :], w1, preferred_element_type=F32)
        hids.append(jnp.square(jnp.maximum(hid, 0.0)).astype(BF16))
    for rows, hid in zip(subs, hids):
        hout_ref[rows, :] += jnp.dot(hid, w2, preferred_element_type=F32)

    @pl.when(j == pl.num_programs(1) - 1)
    def _():
        for r in range(tm // LN_SUB):
            rows = pl.ds(r * LN_SUB, LN_SUB)
            _residual_ln(rows, hout_ref[rows, :], h_ref, g_ref, lng_ref, lnb_ref, sc_ref, sh_ref,
                         hout_ref, uout_ref)


def _mlp(u, w1, w2, h, mods, layer, lng, lnb, n_rows):
    tm = TM_MLP
    tf = 512
    emit_u = layer < DEPTH - 1
    const = lambda i, j: (0, 0)
    row = lambda i, j: (i, 0)
    nff = D_FF // tf
    once = pl.Buffered(1)
    in_specs = [
        pl.BlockSpec((tm, D), row, pipeline_mode=once),
        pl.BlockSpec((D, tf), lambda i, j: (layer, j)),
        pl.BlockSpec((tf, D), lambda i, j: (layer * nff + j, 0)),
        pl.BlockSpec((tm, D), row, pipeline_mode=once),
        _mod_spec(layer, 5, tm, 0),
        pl.BlockSpec((1, D), const),
        pl.BlockSpec((1, D), const),
    ]
    args = [u, w1, w2, h, mods, lng, lnb]
    out_specs = [pl.BlockSpec((tm, D), row, pipeline_mode=once)]
    out_shape = [jax.ShapeDtypeStruct((n_rows, D), F32)]
    if emit_u:
        in_specs += [_mod_spec(layer + 1, 1, tm, 0), _mod_spec(layer + 1, 0, tm, 0)]
        args += [mods, mods]
        out_specs.append(pl.BlockSpec((tm, D), row, pipeline_mode=once))
        out_shape.append(jax.ShapeDtypeStruct((n_rows, D), BF16))
    res = pl.pallas_call(
        functools.partial(_mlp_kernel, emit_u=emit_u),
        grid=(n_rows // tm, nff),
        in_specs=in_specs,
        out_specs=out_specs,
        out_shape=out_shape,
        compiler_params=_cp("arbitrary", "arbitrary"),
        name="mlp",
    )(*args)
    return (res[0], res[1]) if emit_u else (res[0], None)


NSUB = 8
LRU_CHUNK = 256
LRU_T = L + S
LRU_PAD = 3 * NSUB


def _interleave(t, n):
    nseq = t.shape[0] // n
    return t.reshape(nseq, NSUB, n // NSUB, t.shape[1]).transpose(0, 2, 1, 3).reshape(t.shape)


def _deinterleave(t, n):
    nseq = t.shape[0] // n
    return t.reshape(nseq, n // NSUB, NSUB, t.shape[1]).transpose(0, 2, 1, 3).reshape(t.shape)


def _prev_segment(tile):
    s = lax.broadcasted_iota(jnp.int32, tile.shape, 0)
    return jnp.where(s >= 1, pltpu.roll(tile, 1, 0), 0.0)


def _next_segment(tile):
    s = lax.broadcasted_iota(jnp.int32, tile.shape, 0)
    return jnp.where(s < NSUB - 1, pltpu.roll(tile, NSUB - 1, 0), 0.0)


def _lru_conv(r_ref, xp_ref, cw, cb):
    n = r_ref.shape[0]
    x = r_ref[...].astype(F32)
    xp_ref[0:NSUB, :] = _prev_segment(x[n - 2 * NSUB:n - NSUB, :])
    xp_ref[NSUB:2 * NSUB, :] = _prev_segment(x[n - NSUB:n, :])
    xp_ref[2 * NSUB:n + 2 * NSUB, :] = x
    xp_ref[n + 2 * NSUB:n + 3 * NSUB, :] = _next_segment(x[0:NSUB, :])
    y = cb + cw[2:3] * x
    y = y + cw[0:1] * xp_ref[0:n, :]
    y = y + cw[1:2] * xp_ref[NSUB:n + NSUB, :]
    y = y + cw[3:4] * xp_ref[3 * NSUB:n + 3 * NSUB, :]
    return y


def _lru_scan(af_ref, bf_ref, ab_ref, bb_ref, row0, n, init_f, init_b):
    k_steps = n // NSUB

    def body(k, carry):
        hf, pf, hb, pb = carry
        rf = pl.ds(pl.multiple_of(row0 + NSUB * k, NSUB), NSUB)
        rb = pl.ds(pl.multiple_of(row0 + NSUB * (k_steps - 1 - k), NSUB), NSUB)
        a = af_ref[rf, :]
        hf = a * hf + bf_ref[rf, :]
        pf = a * pf
        bf_ref[rf, :] = hf
        af_ref[rf, :] = pf
        a = ab_ref[rb, :]
        hb = a * hb + bb_ref[rb, :]
        pb = a * pb
        bb_ref[rb, :] = hb
        ab_ref[rb, :] = pb
        return hf, pf, hb, pb

    zero = jnp.zeros((NSUB, LRU_BW), F32)
    one = jnp.ones((NSUB, LRU_BW), F32)
    hf, pf, hb, pb = lax.fori_loop(0, k_steps, body, (zero, one, zero, one))
    h = init_f
    in_f = []
    for s in range(NSUB):
        in_f.append(h)
        h = hf[s:s + 1, :] + pf[s:s + 1, :] * h
    final_f = h
    h = init_b
    in_b = [None] * NSUB
    for s in reversed(range(NSUB)):
        in_b[s] = h
        h = hb[s:s + 1, :] + pb[s:s + 1, :] * h
    return jnp.concatenate(in_f, 0), jnp.concatenate(in_b, 0), final_f, h


def _lru_kernel(gl_ref, gc_ref, rl_ref, rc_ref, cw_ref, cb_ref, wg_ref, bg_ref, lam_ref,
                ol_ref, oc_ref, x_ref, xp_ref, af_ref, bf_ref, ab_ref, bb_ref):
    cw = cw_ref[...]
    cb = cb_ref[...]
    x_ref[0:L, :] = _lru_conv(rc_ref, xp_ref, cw, cb)
    x_ref[L:LRU_T, :] = _lru_conv(rl_ref, xp_ref, cw, cb)

    lam = lam_ref[...]
    decay = LRU_C * (jnp.maximum(-lam, 0.0) + jnp.log1p(jnp.exp(-jnp.abs(lam))))
    decay_l2 = -LOG2E * decay
    wg = wg_ref[0]
    bg = bg_ref[...]
    sigmoid = lambda v: 0.5 * jnp.tanh(0.5 * v) + 0.5

    for c in range(LRU_T // LRU_CHUNK):
        rows = pl.ds(c * LRU_CHUNK, LRU_CHUNK)
        xs = x_ref[rows, :]
        z = jnp.dot(xs.astype(BF16), wg, preferred_element_type=F32)
        for d, (a_ref, b_ref) in enumerate(((af_ref, bf_ref), (ab_ref, bb_ref))):
            r = sigmoid(z[:, (2 * d) * LRU_BW:(2 * d + 1) * LRU_BW] + bg[2 * d:2 * d + 1])
            g = sigmoid(z[:, (2 * d + 1) * LRU_BW:(2 * d + 2) * LRU_BW] + bg[2 * d + 1:2 * d + 2])
            a = jnp.exp2(r * decay_l2[d:d + 1])
            a_ref[rows, :] = a
            b_ref[rows, :] = jnp.sqrt(jnp.tanh(r * decay[d:d + 1]) * (a * a + 1.0)) * (g * xs)

    zero = jnp.zeros((1, LRU_BW), F32)
    refs = (af_ref, bf_ref, ab_ref, bb_ref)
    cf_in, cb_in, cf_fin, cb_fin = _lru_scan(*refs, 0, L, zero, zero)
    lf_in, lb_in, _, _ = _lru_scan(*refs, L, S, cf_fin, cb_fin)

    def emit(gate_ref, o_ref, row0, in_f, in_b):
        reps = LRU_CHUNK // NSUB
        in_f = jnp.concatenate([in_f] * reps, 0)
        in_b = jnp.concatenate([in_b] * reps, 0)
        for c in range(gate_ref.shape[0] // LRU_CHUNK):
            src = pl.ds(row0 + c * LRU_CHUNK, LRU_CHUNK)
            dst = pl.ds(c * LRU_CHUNK, LRU_CHUNK)
            h = (bf_ref[src, :] + af_ref[src, :] * in_f) + (bb_ref[src, :] + ab_ref[src, :] * in_b)
            o_ref[dst, :] = (jax.nn.gelu(gate_ref[dst, :].astype(F32)) * h).astype(BF16)

    emit(gc_ref, oc_ref, 0, cf_in, cb_in)
    emit(gl_ref, ol_ref, L, lf_in, lb_in)


def _lru(p, conv_w, conv_b, gate_w, gate_b, lam):
    wg = jnp.transpose(gate_w, (2, 3, 0, 1, 4)).reshape(LRU_NB, LRU_BW, 4 * LRU_BW).astype(BF16)
    bg = gate_b.reshape(4, D)
    ctx0 = TL // L
    lat = pl.BlockSpec((S, LRU_BW), lambda b, n: (b, n))
    ctx = pl.BlockSpec((L, LRU_BW), lambda b, n: (ctx0 + b, n))
    out_lat, out_ctx = pl.pallas_call(
        _lru_kernel,
        grid=(B, LRU_NB),
        in_specs=[
            lat, ctx,
            pl.BlockSpec((S, LRU_BW), lambda b, n: (b, LRU_NB + n)),
            pl.BlockSpec((L, LRU_BW), lambda b, n: (ctx0 + b, LRU_NB + n)),
            pl.BlockSpec((4, LRU_BW), lambda b, n: (0, n)),
            pl.BlockSpec((1, LRU_BW), lambda b, n: (0, n)),
            pl.BlockSpec((1, LRU_BW, 4 * LRU_BW), lambda b, n: (n, 0, 0)),
            pl.BlockSpec((4, LRU_BW), lambda b, n: (0, n)),
            pl.BlockSpec((2, LRU_BW), lambda b, n: (0, n)),
        ],
        out_specs=[lat, pl.BlockSpec((L, LRU_BW), lambda b, n: (b, n))],
        out_shape=[jax.ShapeDtypeStruct((TL, D), BF16), jax.ShapeDtypeStruct((TC, D), BF16)],
        scratch_shapes=[pltpu.VMEM((LRU_T, LRU_BW), F32), pltpu.VMEM((S + LRU_PAD, LRU_BW), F32)]
        + [pltpu.VMEM((LRU_T, LRU_BW), F32)] * 4,
        compiler_params=_cp("arbitrary", "arbitrary"),
        name="lru",
    )(p, p, p, p, conv_w, conv_b.reshape(1, D), wg, bg, lam)
    return out_lat, out_ctx


def _dif_kernel(*refs, n_kv, lam_init):
    q_ref = refs[0]
    kv = refs[1:1 + 2 * n_kv]
    lam_ref, sub_ref, o_ref, kcat_ref, vt_ref = refs[1 + 2 * n_kv:]

    @pl.when(pl.program_id(2) == 0)
    def _():
        off = 0
        for t in range(n_kv):
            n = kv[2 * t].shape[0]
            kcat_ref[off:off + n, :] = kv[2 * t][...]
            vt_ref[0:DIF_DV, off:off + n] = kv[2 * t + 1][...].astype(F32).T.astype(BF16)
            off += n
        vt_ref[DIF_DV:, :] = jnp.ones((vt_ref.shape[0] - DIF_DV, vt_ref.shape[1]), BF16)

    lam = lam_ref[...]
    lam_full = (jnp.exp(jnp.sum(lam[0:1] * lam[1:2], -1, keepdims=True))
                - jnp.exp(jnp.sum(lam[2:3] * lam[3:4], -1, keepdims=True)) + lam_init)
    q = q_ref[...]
    lane = lax.broadcasted_iota(jnp.int32, q.shape, 1)
    st = []
    for m in range(2):
        qm = jnp.where((lane >= 64 * m) & (lane < 64 * (m + 1)), q, jnp.zeros_like(q))
        st.append(lax.dot_general(kcat_ref[...], qm, (((1,), (1,)), ((), ())), preferred_element_type=F32))
    outs = []
    for s in st:
        e = jnp.exp2(s - jnp.max(s, 0, keepdims=True)).astype(BF16)
        r = jnp.dot(vt_ref[...], e, preferred_element_type=F32)
        outs.append(r[:DIF_DV, :] / r[DIF_DV:DIF_DV + 1, :])
    ot = outs[0] - lam_full * outs[1]
    yt = ot * lax.rsqrt(jnp.mean(ot * ot, 0, keepdims=True) + LN_EPS)
    o_ref[...] = (yt.T * sub_ref[...] * (1.0 - lam_init)).astype(BF16)


def _dif_attention(qkv, lam, subln, layer_idx):
    lam_init = 0.8 - 0.6 * math.exp(-0.3 * layer_idx)
    tq = 1024
    nh = DIF_H
    kern = functools.partial(_dif_kernel, lam_init=lam_init)
    small = [pl.BlockSpec((4, 64), lambda *g: (0, 0)), pl.BlockSpec((1, DIF_DV), lambda *g: (0, 0))]
    sub = subln.reshape(1, DIF_DV)
    ctx0 = TL // L
    sem = _cp("arbitrary", "arbitrary", "arbitrary")
    ones_rows = 16
    scratch = lambda nk: [pltpu.VMEM((nk, DIF_DV), BF16), pltpu.VMEM((DIF_DV + ones_rows, nk), BF16)]
    o_lat = pl.pallas_call(
        functools.partial(kern, n_kv=2),
        grid=(B, nh, S // tq),
        in_specs=[
            pl.BlockSpec((tq, DIF_DV), lambda b, h, i: (b * (S // tq) + i, h)),
            pl.BlockSpec((S, DIF_DV), lambda b, h, i: (b, nh + h)),
            pl.BlockSpec((S, DIF_DV), lambda b, h, i: (b, 2 * nh + h)),
            pl.BlockSpec((L, DIF_DV), lambda b, h, i: (ctx0 + b, nh + h)),
            pl.BlockSpec((L, DIF_DV), lambda b, h, i: (ctx0 + b, 2 * nh + h)),
        ] + small,
        out_specs=pl.BlockSpec((tq, DIF_DV), lambda b, h, i: (b * (S // tq) + i, h)),
        out_shape=jax.ShapeDtypeStruct((TL, D), BF16),
        scratch_shapes=scratch(S + L),
        compiler_params=sem,
        name="dif_lat",
    )(qkv, qkv, qkv, qkv, qkv, lam, sub)
    o_ctx = pl.pallas_call(
        functools.partial(kern, n_kv=1),
        grid=(B, nh, 1),
        in_specs=[
            pl.BlockSpec((L, DIF_DV), lambda b, h, i: (ctx0 + b, h)),
            pl.BlockSpec((L, DIF_DV), lambda b, h, i: (ctx0 + b, nh + h)),
            pl.BlockSpec((L, DIF_DV), lambda b, h, i: (ctx0 + b, 2 * nh + h)),
        ] + small,
        out_specs=pl.BlockSpec((L, DIF_DV), lambda b, h, i: (b, h)),
        out_shape=jax.ShapeDtypeStruct((TC, D), BF16),
        scratch_shapes=scratch(L),
        compiler_params=sem,
        name="dif_ctx",
    )(qkv, qkv, qkv, lam, sub)
    return o_lat, o_ctx


def _ret_kernel(ql_ref, qc_ref, kl_ref, kc_ref, vl_ref, vc_ref, gl_ref, gc_ref, lg_ref,
                ol_ref, oc_ref, rf_ref, rb_ref, il_ref, ic_ref):
    lg = lg_ref[0]
    lgf = lg[0:1, 0:1]
    lgb = lg[1:2, 0:1]

    def tables(C):
        row = lax.broadcasted_iota(jnp.int32, (C, C), 0).astype(F32)
        col = lax.broadcasted_iota(jnp.int32, (C, C), 1).astype(F32)
        rel = row - col
        pos = lax.broadcasted_iota(jnp.int32, (C, RET_DK), 0).astype(F32)
        return dict(
            decay=jnp.exp(jnp.abs(rel) * jnp.where(rel >= 0, lgf, lgb)),
            qdec_f=jnp.exp((pos + 1.0) * lgf),
            kdec_f=jnp.exp((C - 1.0 - pos) * lgf),
            qdec_b=jnp.exp((C - pos) * lgb),
            kdec_b=jnp.exp(pos * lgb),
            cdec_f=jnp.exp(C * lgf),
            cdec_b=jnp.exp(C * lgb),
        )

    tdot = lambda a, b: lax.dot_general(a, b, (((0,), (0,)), ((), ())), preferred_element_type=F32)

    def bwd_chunk(t, q_ref, k_ref, v_ref, i_ref, rows):
        q = q_ref[rows, :].astype(F32)
        k = k_ref[rows, :].astype(F32)
        v = v_ref[rows, :]
        i_ref[rows, :] = jnp.dot((q * t["qdec_b"]).astype(BF16), rb_ref[...].astype(BF16),
                                 preferred_element_type=F32)
        rb_ref[...] = t["cdec_b"] * rb_ref[...] + tdot((k * t["kdec_b"]).astype(BF16), v)

    def fwd_chunk(t, q_ref, k_ref, v_ref, g_ref, i_ref, o_ref, rows):
        qb = q_ref[rows, :]
        kb = k_ref[rows, :]
        v = v_ref[rows, :]
        q = qb.astype(F32)
        k = kb.astype(F32)
        s = lax.dot_general(qb, kb, (((1,), (1,)), ((), ())), preferred_element_type=F32) * t["decay"]
        o = jnp.dot(s.astype(BF16), v, preferred_element_type=F32)
        o = o + jnp.dot((q * t["qdec_f"]).astype(BF16), rf_ref[...].astype(BF16), preferred_element_type=F32)
        o = o + i_ref[rows, :]
        rf_ref[...] = t["cdec_f"] * rf_ref[...] + tdot((k * t["kdec_f"]).astype(BF16), v)
        o = o * lax.rsqrt(jnp.mean(o * o, -1, keepdims=True) + LN_EPS)
        g = g_ref[rows, :].astype(F32)
        o_ref[rows, :] = (g * jax.nn.sigmoid(g) * o).astype(BF16)

    rf_ref[...] = jnp.zeros_like(rf_ref)
    rb_ref[...] = jnp.zeros_like(rb_ref)
    tc = tables(L)
    tl = tables(RET_CH)
    lat_chunks = [pl.ds(n * RET_CH, RET_CH) for n in range(S // RET_CH)]
    all_ctx = pl.ds(0, L)

    bwd_chunk(tc, qc_ref, kc_ref, vc_ref, ic_ref, all_ctx)
    for rows in reversed(lat_chunks):
        bwd_chunk(tl, ql_ref, kl_ref, vl_ref, il_ref, rows)

    fwd_chunk(tc, qc_ref, kc_ref, vc_ref, gc_ref, ic_ref, oc_ref, all_ctx)
    for rows in lat_chunks:
        fwd_chunk(tl, ql_ref, kl_ref, vl_ref, gl_ref, il_ref, ol_ref, rows)


def _retention(p):
    log_gf = jnp.log1p(-jnp.exp2(-5.0 - jnp.arange(RET_H, dtype=F32)))
    lg = jnp.zeros((RET_H, 8, LANES), F32)
    lg = lg.at[:, 0, :].set(log_gf[:, None]).at[:, 1, :].set(log_gf[::-1][:, None])
    ctx0 = TL // L
    nk = D // RET_DK
    nv = 2 * D // RET_DV
    lat = lambda w, off: pl.BlockSpec((S, w), lambda b, h: (b, off + h))
    ctx = lambda w, off: pl.BlockSpec((L, w), lambda b, h: (ctx0 + b, off + h))
    o_lat, o_ctx = pl.pallas_call(
        _ret_kernel,
        grid=(B, RET_H),
        in_specs=[
            lat(RET_DK, 0), ctx(RET_DK, 0),
            lat(RET_DK, nk), ctx(RET_DK, nk),
            lat(RET_DV, nv), ctx(RET_DV, nv),
            lat(RET_DV, nv + RET_H), ctx(RET_DV, nv + RET_H),
            pl.BlockSpec((1, 8, LANES), lambda b, h: (h, 0, 0)),
        ],
        out_specs=[pl.BlockSpec((S, RET_DV), lambda b, h: (b, h)),
                   pl.BlockSpec((L, RET_DV), lambda b, h: (b, h))],
        out_shape=[jax.ShapeDtypeStruct((TL, 2 * D), BF16), jax.ShapeDtypeStruct((TC, 2 * D), BF16)],
        scratch_shapes=[
            pltpu.VMEM((RET_DK, RET_DV), F32), pltpu.VMEM((RET_DK, RET_DV), F32),
            pltpu.VMEM((S, RET_DV), F32), pltpu.VMEM((L, RET_DV), F32),
        ],
        compiler_params=_cp("arbitrary", "arbitrary"),
        name="retention",
    )(p, p, p, p, p, p, p, p, lg)
    return o_lat, o_ctx


def _swa_kernel(q_ref, kp_ref, kc_ref, kn_ref, kx_ref, vp_ref, vc_ref, vn_ref, vx_ref, sink_ref, o_ref):
    n = pl.program_id(2)
    nb = pl.num_programs(2)
    k2 = jnp.concatenate([kp_ref[...], kc_ref[...], kn_ref[...], kx_ref[...]], axis=0)
    v2 = jnp.concatenate([vp_ref[...], vc_ref[...], vn_ref[...], vx_ref[...]], axis=0)
    nk = k2.shape[0]
    v2 = jnp.concatenate([v2, jnp.ones((nk, LANES), BF16)], axis=1)
    qi = lax.broadcasted_iota(jnp.int32, (QB, nk), 0)
    kj = lax.broadcasted_iota(jnp.int32, (QB, nk), 1)
    band = (kj >= qi) & (kj <= qi + 2 * WINDOW)
    band = band & ((kj >= QB) | (n > 0)) & ((kj < 2 * QB) | (n < nb - 1))
    valid = band | (kj >= 3 * QB)
    bias = jnp.where(valid, 0.0, NEG_INF).astype(F32)
    bias = jnp.concatenate([bias] * SWA_G, axis=0)
    lane = lax.broadcasted_iota(jnp.int32, (QB, LANES), 1)
    lo = lane < SWA_DH
    qcols = [q_ref[:, c * LANES:(c + 1) * LANES].astype(F32) for c in range(4)]
    qswap = [pltpu.roll(x, SWA_DH, 1) for x in qcols]
    out_cols = [None] * 4
    for t in range(2):
        keep = lo if t == 0 else ~lo
        heads = [(2 * t + g // 2, g % 2) for g in range(SWA_G)]
        qs = jnp.concatenate(
            [jnp.where(keep, qcols[c] if hf == t else qswap[c], 0.0).astype(BF16) for c, hf in heads], axis=0)
        s = lax.dot_general(qs, k2, (((1,), (1,)), ((), ())), preferred_element_type=F32) + bias
        sink = jnp.concatenate(
            [jnp.broadcast_to(sink_ref[2 * c + hf][:, 0:1] * LOG2E, (QB, 1)) for c, hf in heads], axis=0)
        mx = jnp.maximum(jnp.max(s, -1, keepdims=True), sink)
        e = jnp.exp2(s - mx).astype(BF16)
        r = jnp.dot(e, v2, preferred_element_type=F32)
        o = r[:, :LANES] / (r[:, LANES:] + jnp.exp2(sink - mx))
        for g, (c, hf) in enumerate(heads):
            og = o[g * QB:(g + 1) * QB, :]
            if hf != t:
                og = pltpu.roll(og, SWA_DH, 1)
            og = jnp.where(lo if hf == 0 else ~lo, og, 0.0)
            out_cols[c] = og if out_cols[c] is None else out_cols[c] + og
    for c in range(4):
        o_ref[:, c * LANES:(c + 1) * LANES] = out_cols[c].astype(BF16)


def _swa_attention(qkv, sink):
    nqb = S // QB
    kcol = SWA_H * SWA_DH // LANES
    vcol = kcol + SWA_KV * SWA_DH // LANES
    ctx0 = TL // L
    sink_t = jnp.broadcast_to(sink.astype(F32)[:, None, None], (SWA_H, 1, LANES))

    def win(col0, d):
        def index_map(b, p, n):
            return (b * nqb + jnp.clip(n + d, 0, nqb - 1), col0 + p)
        return pl.BlockSpec((QB, LANES), index_map)

    ctx = lambda col0: pl.BlockSpec((L, LANES), lambda b, p, n: (ctx0 + b, col0 + p))
    return pl.pallas_call(
        _swa_kernel,
        grid=(B, SWA_KV // 2, nqb),
        in_specs=[
            pl.BlockSpec((QB, 4 * LANES), lambda b, p, n: (b * nqb + n, p)),
            win(kcol, -1), win(kcol, 0), win(kcol, 1), ctx(kcol),
            win(vcol, -1), win(vcol, 0), win(vcol, 1), ctx(vcol),
            pl.BlockSpec((8, 1, LANES), lambda b, p, n: (p, 0, 0)),
        ],
        out_specs=pl.BlockSpec((QB, 4 * LANES), lambda b, p, n: (b * nqb + n, p)),
        out_shape=jax.ShapeDtypeStruct((TL, D), BF16),
        compiler_params=_cp("arbitrary", "arbitrary", "arbitrary"),
        name="swa",
    )(qkv, qkv, qkv, qkv, qkv, qkv, qkv, qkv, qkv, sink_t)


@jax.jit
def _forward(x, c, ctx, c_ctx, ada_w, ada_b, ln_g, ln_b, mlp_w1, mlp_w2,
             lru_w_in, lru_conv_w, lru_conv_b, lru_gate_w, lru_gate_b, lru_lambda, lru_w_out,
             dif_w_qkv, dif_lambda, dif_subln, dif_w_out, ret_w_qkvg, ret_w_out,
             swa_w_qkv, swa_sink, swa_w_out):
    cvec = jnp.concatenate([c, c_ctx[None, :], jnp.zeros((NSEG - B - 1, D), F32)], 0)
    mods = _ada(cvec, ada_w, ada_b)
    tables = _rope_tables(TM)
    h, u = _modulate(x.reshape(TL, D), ctx.reshape(TC, D), mods, 0)
    w_outs = [lru_w_out[0], dif_w_out[0], ret_w_out[0], swa_w_out[0]]
    w1 = mlp_w1.reshape(DEPTH * D, D_FF)
    w2 = mlp_w2.reshape(DEPTH * D_FF, D)
    qscale = SWA_DH ** -0.5 * LOG2E
    for i in range(DEPTH):
        last = i == DEPTH - 1
        n_rows = TL if last else T
        if i == 0:
            ui = jnp.concatenate([_interleave(u[:TL], S), _interleave(u[TL:], L)], 0)
            p = _proj(ui, lru_w_in[0], 1024)
            o, oc = _lru(p, lru_conv_w[0], lru_conv_b[0], lru_gate_w[0], lru_gate_b[0], lru_lambda[0])
            o, oc = _deinterleave(o, S), _deinterleave(oc, L)
        elif i == 1:
            p = _proj(u, dif_w_qkv[0], 1024, rope=(D, 2 * D, qscale, tables))
            o, oc = _dif_attention(p, dif_lambda[0], dif_subln[0], i)
        elif i == 2:
            p = _proj(u, ret_w_qkvg[0], 1024, qscale_cols=(D, RET_DK ** -0.5))
            o, oc = _retention(p)
        else:
            p = _proj(u, swa_w_qkv[0], 512, rope=(SWA_H * SWA_DH, (SWA_H + SWA_KV) * SWA_DH, qscale, tables))
            o, oc = _swa_attention(p, swa_sink[0]), None
        h, u2 = _outln(o, oc, _cast_bf16(w_outs[i]), h, mods, i, ln_g[i, 0:1], ln_b[i, 0:1])
        h, u = _mlp(u2, w1, w2, h, mods, i, ln_g[i, 1:2], ln_b[i, 1:2], n_rows)
    return h.reshape(B, S, D)


def kernel(x, c, ctx, c_ctx, ada_w, ada_b, ln_g, ln_b, mlp_w1, mlp_w2, lru_w_in, lru_conv_w, lru_conv_b,
           lru_gate_w, lru_gate_b, lru_lambda, lru_w_out, dif_w_qkv, dif_lambda, dif_subln, dif_w_out,
           ret_w_qkvg, ret_w_out, swa_w_qkv, swa_sink, swa_w_out):
    return _forward(x, c, ctx, c_ctx, ada_w, ada_b, ln_g, ln_b, mlp_w1, mlp_w2,
                    lru_w_in, lru_conv_w, lru_conv_b, lru_gate_w, lru_gate_b, lru_lambda, lru_w_out,
                    dif_w_qkv, dif_lambda, dif_subln, dif_w_out, ret_w_qkvg, ret_w_out,
                    swa_w_qkv, swa_sink, swa_w_out)
```

```python
import functools
import math

import jax
import jax.numpy as jnp
from jax import lax
from jax.experimental import pallas as pl
from jax.experimental.pallas import tpu as pltpu

F32 = jnp.float32
BF16 = jnp.bfloat16

D = 2048
B = 4
S = 2048
L = 256
DEPTH = 4
GRID_W = 64
TL = B * S
TC = B * L
T = TL + TC
NSEG = 8
ADA = 6
D_FF = 4 * D
ALPHA = (2 * DEPTH) ** 0.25
LN_EPS = 1e-5
ROPE_BASE = 10000.0
NEG_INF = -1e30

LRU_BW = 256
LRU_NB = D // LRU_BW
LRU_C = 8.0
DIF_H = 16
DIF_DV = 128
RET_H = 8
RET_DK = 256
RET_DV = 512
RET_CH = 512
SWA_H = 32
SWA_KV = 8
SWA_DH = 64
SWA_G = SWA_H // SWA_KV
WINDOW = 128
QB = 128

VMEM_LIMIT = 56 * 1024 * 1024
LANES = 128

TM = 1024
TM_LN = 512
TM_MLP = 1024
PROJ_SUB = 256
LN_SUB = 256
MLP_SUB = 1024
LOG2E = math.log2(math.e)


def _cp(*sem):
    return pltpu.CompilerParams(dimension_semantics=sem, vmem_limit_bytes=VMEM_LIMIT)


def _seg(i, tm):
    return jnp.minimum((i * tm) // S, B)


def _ada_kernel(c_ref, w_ref, b_ref, o_ref):
    c = c_ref[...]
    a = (c * jax.nn.sigmoid(c)).astype(BF16)
    o_ref[0] = jnp.dot(a, w_ref[0].astype(BF16), preferred_element_type=F32) + b_ref[0]


def _ada(cvec, ada_w, ada_b):
    tn = 1024
    out = pl.pallas_call(
        _ada_kernel,
        grid=(DEPTH, ADA * D // tn),
        in_specs=[
            pl.BlockSpec((NSEG, D), lambda l, j: (0, 0)),
            pl.BlockSpec((1, D, tn), lambda l, j: (l, 0, j)),
            pl.BlockSpec((1, 1, tn), lambda l, j: (l, 0, j)),
        ],
        out_specs=pl.BlockSpec((1, NSEG, tn), lambda l, j: (l, 0, j)),
        out_shape=jax.ShapeDtypeStruct((DEPTH, NSEG, ADA * D), F32),
        compiler_params=_cp("arbitrary", "arbitrary"),
        name="ada",
    )(cvec, ada_w, ada_b.reshape(DEPTH, 1, ADA * D))
    return out.reshape(DEPTH * NSEG * ADA, 1, D)


def _mod_spec(layer, chunk, tm, grid_pos):
    base = layer * NSEG * ADA + chunk

    def index_map(*g):
        return (base + _seg(g[grid_pos], tm) * ADA, 0, 0)

    return pl.BlockSpec((1, 1, D), index_map)


def _modulate_kernel(x_ref, ctx_ref, sc_ref, sh_ref, h_ref, u_ref, *, nlat):
    def emit(src_ref):
        v = src_ref[...]
        h_ref[...] = v
        u_ref[...] = (v * (1.0 + sc_ref[0]) + sh_ref[0]).astype(BF16)

    pl.when(pl.program_id(0) < nlat)(lambda: emit(x_ref))
    pl.when(pl.program_id(0) >= nlat)(lambda: emit(ctx_ref))


def _modulate(x, ctx, mods, layer):
    tm = TM_LN
    nlat = TL // tm
    return pl.pallas_call(
        functools.partial(_modulate_kernel, nlat=nlat),
        grid=(T // tm,),
        in_specs=[
            pl.BlockSpec((tm, D), lambda i: (jnp.minimum(i, nlat - 1), 0)),
            pl.BlockSpec((tm, D), lambda i: (jnp.maximum(i - nlat, 0), 0)),
            _mod_spec(layer, 1, tm, 0),
            _mod_spec(layer, 0, tm, 0),
        ],
        out_specs=[pl.BlockSpec((tm, D), lambda i: (i, 0))] * 2,
        out_shape=[jax.ShapeDtypeStruct((T, D), F32), jax.ShapeDtypeStruct((T, D), BF16)],
        compiler_params=_cp("arbitrary"),
        name="modulate",
    )(x, ctx, mods, mods)


def _cast_kernel(w_ref, o_ref):
    o_ref[...] = w_ref[...].astype(BF16)


CAST_BLOCK_ELEMS = 1 << 20


def _cast_bf16(w):
    r, c = w.shape
    tr = CAST_BLOCK_ELEMS // c
    return pl.pallas_call(
        _cast_kernel,
        grid=(r // tr,),
        in_specs=[pl.BlockSpec((tr, c), lambda i: (i, 0))],
        out_specs=pl.BlockSpec((tr, c), lambda i: (i, 0)),
        out_shape=jax.ShapeDtypeStruct((r, c), BF16),
        compiler_params=_cp("arbitrary"),
        name="cast",
    )(w)


def _rope_tables(tm):
    rows = S // GRID_W
    row = jnp.repeat(jnp.arange(rows, dtype=F32), GRID_W)
    col = jnp.tile(jnp.arange(GRID_W, dtype=F32), rows)
    half = SWA_DH // 2
    inv_freq = ROPE_BASE ** (-jnp.arange(0, half, 2, dtype=F32) / half)
    ang_r = row[:, None] * inv_freq[None, :]
    ang_c = col[:, None] * inv_freq[None, :]
    ang = jnp.concatenate([ang_r, ang_r, ang_c, ang_c], -1)
    cos, sin = jnp.cos(ang), jnp.sin(ang)
    first = (jnp.arange(SWA_DH) % 32) < 16
    sin_up = jnp.where(first[None, :], -sin, 0.0)
    sin_dn = jnp.where(first[None, :], 0.0, sin)

    def lay(t, ident):
        t = jnp.tile(t, (1, LANES // SWA_DH))
        return jnp.concatenate([t, jnp.full((tm, LANES), ident, F32)], 0)

    return lay(cos, 1.0), lay(sin_up, 0.0), lay(sin_dn, 0.0)


def _proj_kernel(a_ref, w_ref, *refs, rope, nq, nqk, qscale, n_side):
    if rope:
        cos_ref, up_ref, dn_ref = refs[:3]
        refs = refs[3:]
    side_in = refs[:n_side]
    o_ref = refs[n_side]
    side_out = refs[n_side + 1:2 * n_side + 1]
    wb_ref = refs[2 * n_side + 1]
    j = pl.program_id(0)
    i = pl.program_id(1)

    @pl.when(i == 0)
    def _():
        wb_ref[...] = w_ref[...].astype(BF16)

    if n_side:
        @pl.when(j * pl.num_programs(1) + i < SIDE_BLOCKS)
        def _():
            for src, dst in zip(side_in, side_out):
                dst[...] = src[...].astype(BF16)

    tm, tn = o_ref.shape
    subs = [pl.ds(r * PROJ_SUB, PROJ_SUB) for r in range(tm // PROJ_SUB)]
    scale = jnp.where(j < nq, qscale, 1.0).astype(F32)

    def plain(scaled):
        for rows in subs:
            acc = jnp.dot(a_ref[rows, :], wb_ref[...], preferred_element_type=F32)
            o_ref[rows, :] = (acc * scale if scaled else acc).astype(o_ref.dtype)

    if not rope:
        plain(nq > 0)
        return

    @pl.when(j < nqk)
    def _():
        for rows in subs:
            acc = jnp.dot(a_ref[rows, :], wb_ref[...], preferred_element_type=F32)
            cos = cos_ref[rows, :] * scale
            up = up_ref[rows, :] * scale
            dn = dn_ref[rows, :] * scale
            for c in range(tn // LANES):
                x = acc[:, c * LANES:(c + 1) * LANES]
                y = x * cos + pltpu.roll(x, LANES - 16, 1) * up + pltpu.roll(x, 16, 1) * dn
                o_ref[rows, c * LANES:(c + 1) * LANES] = y.astype(o_ref.dtype)

    pl.when(j >= nqk)(lambda: plain(False))


SIDE_BLOCKS = 32


def _proj(a, w, tn, rope=None, qscale_cols=None, side=()):
    n = w.shape[1]
    tm = TM
    ni = T // tm
    assert (n // tn) * ni >= SIDE_BLOCKS
    in_specs = [
        pl.BlockSpec((tm, D), lambda j, i: (i, 0)),
        pl.BlockSpec((D, tn), lambda j, i: (0, j)),
    ]
    args = [a, w]
    kw = dict(rope=False, nq=0, nqk=0, qscale=1.0)
    if qscale_cols is not None:
        kw = dict(rope=False, nq=qscale_cols[0] // tn, nqk=0, qscale=qscale_cols[1])
    if rope is not None:
        nq_cols, nqk_cols, qscale, tables = rope
        nlat = TL // tm
        per_seq = S // tm

        def tmap(j, i):
            return (jnp.where(i < nlat, i % per_seq, per_seq), 0)

        in_specs += [pl.BlockSpec((tm, LANES), tmap)] * 3
        args += list(tables)
        kw = dict(rope=True, nq=nq_cols // tn, nqk=nqk_cols // tn, qscale=qscale)
    out_specs = [pl.BlockSpec((tm, tn), lambda j, i: (i, j))]
    out_shape = [jax.ShapeDtypeStruct((T, n), BF16)]
    for stacked, layer, rows in side:
        cols = stacked.shape[1]
        br = rows // SIDE_BLOCKS
        blk = lambda j, i: jnp.minimum(j * ni + i, SIDE_BLOCKS - 1)
        in_specs.append(pl.BlockSpec((br, cols), lambda j, i, layer=layer: (layer * SIDE_BLOCKS + blk(j, i), 0)))
        args.append(stacked)
        out_specs.append(pl.BlockSpec((br, cols), lambda j, i: (blk(j, i), 0)))
        out_shape.append(jax.ShapeDtypeStruct((rows, cols), BF16))
    res = pl.pallas_call(
        functools.partial(_proj_kernel, n_side=len(side), **kw),
        grid=(n // tn, ni),
        in_specs=in_specs,
        out_specs=out_specs,
        out_shape=out_shape,
        scratch_shapes=[pltpu.VMEM((D, tn), BF16)],
        compiler_params=_cp("arbitrary", "arbitrary"),
        name="proj",
    )(*args)
    return res[0], res[1:]


def _ln_epilogue(z, lng, lnb):
    mu = jnp.mean(z, -1, keepdims=True)
    zc = z - mu
    var = jnp.mean(zc * zc, -1, keepdims=True)
    return zc * lax.rsqrt(var + LN_EPS) * lng + lnb


def _residual_ln(rows, y, h_ref, g_ref, lng_ref, lnb_ref, sc_ref, sh_ref, hout_ref, uout_ref):
    hn = _ln_epilogue(ALPHA * h_ref[rows, :] + g_ref[0] * y, lng_ref[...], lnb_ref[...])
    hout_ref[rows, :] = hn
    if uout_ref is not None:
        uout_ref[rows, :] = (hn * (1.0 + sc_ref[0]) + sh_ref[0]).astype(BF16)


def _outln_kernel(*refs, nlat):
    if nlat is None:
        ol_ref, oc_ref = refs[0], None
        refs = refs[1:]
    else:
        ol_ref, oc_ref = refs[:2]
        refs = refs[2:]
    w_ref, h_ref, g_ref, lng_ref, lnb_ref, sc_ref, sh_ref, hout_ref, uout_ref = refs

    def body(o_ref):
        for r in range(o_ref.shape[0] // LN_SUB):
            rows = pl.ds(r * LN_SUB, LN_SUB)
            y = jnp.dot(o_ref[rows, :], w_ref[...], preferred_element_type=F32)
            _residual_ln(rows, y, h_ref, g_ref, lng_ref, lnb_ref, sc_ref, sh_ref, hout_ref, uout_ref)

    if nlat is None:
        body(ol_ref)
    else:
        pl.when(pl.program_id(0) < nlat)(lambda: body(ol_ref))
        pl.when(pl.program_id(0) >= nlat)(lambda: body(oc_ref))


def _outln(o_lat, o_ctx, w_bf16, h, mods, layer, lng, lnb):
    kin = o_lat.shape[1]
    tm = TM_LN * D // kin
    const = lambda i: (0, 0)
    nlat = TL // tm
    if o_ctx is None:
        n_rows = TL
        o_specs = [pl.BlockSpec((tm, kin), lambda i: (i, 0))]
        o_args = [o_lat]
    else:
        n_rows = T
        o_specs = [pl.BlockSpec((tm, kin), lambda i: (jnp.minimum(i, nlat - 1), 0)),
                   pl.BlockSpec((tm, kin), lambda i: (jnp.maximum(i - nlat, 0), 0))]
        o_args = [o_lat, o_ctx]
    return pl.pallas_call(
        functools.partial(_outln_kernel, nlat=None if o_ctx is None else nlat),
        grid=(n_rows // tm,),
        in_specs=o_specs + [
            pl.BlockSpec((kin, D), const, pipeline_mode=pl.Buffered(1)),
            pl.BlockSpec((tm, D), lambda i: (i, 0)),
            _mod_spec(layer, 2, tm, 0),
            pl.BlockSpec((1, D), const),
            pl.BlockSpec((1, D), const),
            _mod_spec(layer, 4, tm, 0),
            _mod_spec(layer, 3, tm, 0),
        ],
        out_specs=[pl.BlockSpec((tm, D), lambda i: (i, 0)), pl.BlockSpec((tm, D), lambda i: (i, 0))],
        out_shape=[jax.ShapeDtypeStruct((n_rows, D), F32), jax.ShapeDtypeStruct((n_rows, D), BF16)],
        compiler_params=_cp("arbitrary"),
        name="outln",
    )(*o_args, w_bf16, h, mods, lng, lnb, mods, mods)


def _mlp_kernel(u_ref, w1_ref, w2_ref, h_ref, g_ref, lng_ref, lnb_ref, *refs, emit_u):
    if emit_u:
        sc_ref, sh_ref, hout_ref, uout_ref = refs
    else:
        (hout_ref,) = refs
        sc_ref = sh_ref = uout_ref = None
    j = pl.program_id(1)
    tm = u_ref.shape[0]

    @pl.when(j == 0)
    def _():
        hout_ref[...] = jnp.zeros_like(hout_ref)

    subs = [pl.ds(r * MLP_SUB, MLP_SUB) for r in range(tm // MLP_SUB)]
    w1 = w1_ref[...]
    w2 = w2_ref[...]
    hids = []
    for rows in subs:
        hid = jnp.dot(u_ref[rows, :], w1, preferred_element_type=F32)
        hids.append(jnp.square(jnp.maximum(hid, 0.0)).astype(BF16))
    for rows, hid in zip(subs, hids):
        hout_ref[rows, :] += jnp.dot(hid, w2, preferred_element_type=F32)

    @pl.when(j == pl.num_programs(1) - 1)
    def _():
        for r in range(tm // LN_SUB):
            rows = pl.ds(r * LN_SUB, LN_SUB)
            _residual_ln(rows, hout_ref[rows, :], h_ref, g_ref, lng_ref, lnb_ref, sc_ref, sh_ref,
                         hout_ref, uout_ref)


def _mlp(u, w1, w2, h, mods, layer, lng, lnb, n_rows):
    tm = TM_MLP
    tf = 1024
    emit_u = layer < DEPTH - 1
    const = lambda i, j: (0, 0)
    row = lambda i, j: (i, 0)
    nff = D_FF // tf
    once = pl.Buffered(1)
    in_specs = [
        pl.BlockSpec((tm, D), row, pipeline_mode=once),
        pl.BlockSpec((D, tf), lambda i, j: (0, j)),
        pl.BlockSpec((tf, D), lambda i, j: (j, 0)),
        pl.BlockSpec((tm, D), row, pipeline_mode=once),
        _mod_spec(layer, 5, tm, 0),
        pl.BlockSpec((1, D), const),
        pl.BlockSpec((1, D), const),
    ]
    args = [u, w1, w2, h, mods, lng, lnb]
    out_specs = [pl.BlockSpec((tm, D), row, pipeline_mode=once)]
    out_shape = [jax.ShapeDtypeStruct((n_rows, D), F32)]
    if emit_u:
        in_specs += [_mod_spec(layer + 1, 1, tm, 0), _mod_spec(layer + 1, 0, tm, 0)]
        args += [mods, mods]
        out_specs.append(pl.BlockSpec((tm, D), row, pipeline_mode=once))
        out_shape.append(jax.ShapeDtypeStruct((n_rows, D), BF16))
    res = pl.pallas_call(
        functools.partial(_mlp_kernel, emit_u=emit_u),
        grid=(n_rows // tm, nff),
        in_specs=in_specs,
        out_specs=out_specs,
        out_shape=out_shape,
        compiler_params=_cp("arbitrary", "arbitrary"),
        name="mlp",
    )(*args)
    return (res[0], res[1]) if emit_u else (res[0], None)


NSUB = 8
LRU_CHUNK = 256
LRU_T = L + S
LRU_PAD = 3 * NSUB


def _interleave(t, n):
    nseq = t.shape[0] // n
    return t.reshape(nseq, NSUB, n // NSUB, t.shape[1]).transpose(0, 2, 1, 3).reshape(t.shape)


def _deinterleave(t, n):
    nseq = t.shape[0] // n
    return t.reshape(nseq, n // NSUB, NSUB, t.shape[1]).transpose(0, 2, 1, 3).reshape(t.shape)


def _prev_segment(tile):
    s = lax.broadcasted_iota(jnp.int32, tile.shape, 0)
    return jnp.where(s >= 1, pltpu.roll(tile, 1, 0), 0.0)


def _next_segment(tile):
    s = lax.broadcasted_iota(jnp.int32, tile.shape, 0)
    return jnp.where(s < NSUB - 1, pltpu.roll(tile, NSUB - 1, 0), 0.0)


def _lru_conv(r_ref, xp_ref, cw, cb):
    n = r_ref.shape[0]
    x = r_ref[...].astype(F32)
    xp_ref[0:NSUB, :] = _prev_segment(x[n - 2 * NSUB:n - NSUB, :])
    xp_ref[NSUB:2 * NSUB, :] = _prev_segment(x[n - NSUB:n, :])
    xp_ref[2 * NSUB:n + 2 * NSUB, :] = x
    xp_ref[n + 2 * NSUB:n + 3 * NSUB, :] = _next_segment(x[0:NSUB, :])
    y = cb + cw[2:3] * x
    y = y + cw[0:1] * xp_ref[0:n, :]
    y = y + cw[1:2] * xp_ref[NSUB:n + NSUB, :]
    y = y + cw[3:4] * xp_ref[3 * NSUB:n + 3 * NSUB, :]
    return y


def _lru_scan(af_ref, bf_ref, ab_ref, bb_ref, row0, n, init_f, init_b):
    k_steps = n // NSUB

    def body(k, carry):
        hf, pf, hb, pb = carry
        rf = pl.ds(pl.multiple_of(row0 + NSUB * k, NSUB), NSUB)
        rb = pl.ds(pl.multiple_of(row0 + NSUB * (k_steps - 1 - k), NSUB), NSUB)
        a = af_ref[rf, :]
        hf = a * hf + bf_ref[rf, :]
        pf = a * pf
        bf_ref[rf, :] = hf
        af_ref[rf, :] = pf
        a = ab_ref[rb, :]
        hb = a * hb + bb_ref[rb, :]
        pb = a * pb
        bb_ref[rb, :] = hb
        ab_ref[rb, :] = pb
        return hf, pf, hb, pb

    zero = jnp.zeros((NSUB, LRU_BW), F32)
    one = jnp.ones((NSUB, LRU_BW), F32)
    hf, pf, hb, pb = lax.fori_loop(0, k_steps, body, (zero, one, zero, one))
    h = init_f
    in_f = []
    for s in range(NSUB):
        in_f.append(h)
        h = hf[s:s + 1, :] + pf[s:s + 1, :] * h
    final_f = h
    h = init_b
    in_b = [None] * NSUB
    for s in reversed(range(NSUB)):
        in_b[s] = h
        h = hb[s:s + 1, :] + pb[s:s + 1, :] * h
    return jnp.concatenate(in_f, 0), jnp.concatenate(in_b, 0), final_f, h


def _lru_kernel(gl_ref, gc_ref, rl_ref, rc_ref, cw_ref, cb_ref, wg_ref, bg_ref, lam_ref,
                ol_ref, oc_ref, x_ref, xp_ref, af_ref, bf_ref, ab_ref, bb_ref):
    cw = cw_ref[...]
    cb = cb_ref[...]
    x_ref[0:L, :] = _lru_conv(rc_ref, xp_ref, cw, cb)
    x_ref[L:LRU_T, :] = _lru_conv(rl_ref, xp_ref, cw, cb)

    lam = lam_ref[...]
    decay = LRU_C * (jnp.maximum(-lam, 0.0) + jnp.log1p(jnp.exp(-jnp.abs(lam))))
    decay_l2 = -LOG2E * decay
    wg = wg_ref[0]
    bg = bg_ref[...]
    sigmoid = lambda v: 0.5 * jnp.tanh(0.5 * v) + 0.5

    for c in range(LRU_T // LRU_CHUNK):
        rows = pl.ds(c * LRU_CHUNK, LRU_CHUNK)
        xs = x_ref[rows, :]
        z = jnp.dot(xs.astype(BF16), wg, preferred_element_type=F32)
        for d, (a_ref, b_ref) in enumerate(((af_ref, bf_ref), (ab_ref, bb_ref))):
            r = sigmoid(z[:, (2 * d) * LRU_BW:(2 * d + 1) * LRU_BW] + bg[2 * d:2 * d + 1])
            g = sigmoid(z[:, (2 * d + 1) * LRU_BW:(2 * d + 2) * LRU_BW] + bg[2 * d + 1:2 * d + 2])
            a = jnp.exp2(r * decay_l2[d:d + 1])
            a_ref[rows, :] = a
            b_ref[rows, :] = jnp.sqrt(jnp.tanh(r * decay[d:d + 1]) * (a * a + 1.0)) * (g * xs)

    zero = jnp.zeros((1, LRU_BW), F32)
    refs = (af_ref, bf_ref, ab_ref, bb_ref)
    cf_in, cb_in, cf_fin, cb_fin = _lru_scan(*refs, 0, L, zero, zero)
    lf_in, lb_in, _, _ = _lru_scan(*refs, L, S, cf_fin, cb_fin)

    def emit(gate_ref, o_ref, row0, in_f, in_b):
        reps = LRU_CHUNK // NSUB
        in_f = jnp.concatenate([in_f] * reps, 0)
        in_b = jnp.concatenate([in_b] * reps, 0)
        for c in range(gate_ref.shape[0] // LRU_CHUNK):
            src = pl.ds(row0 + c * LRU_CHUNK, LRU_CHUNK)
            dst = pl.ds(c * LRU_CHUNK, LRU_CHUNK)
            h = (bf_ref[src, :] + af_ref[src, :] * in_f) + (bb_ref[src, :] + ab_ref[src, :] * in_b)
            o_ref[dst, :] = (jax.nn.gelu(gate_ref[dst, :].astype(F32)) * h).astype(BF16)

    emit(gc_ref, oc_ref, 0, cf_in, cb_in)
    emit(gl_ref, ol_ref, L, lf_in, lb_in)


def _lru(p, conv_w, conv_b, gate_w, gate_b, lam):
    wg = jnp.transpose(gate_w, (2, 3, 0, 1, 4)).reshape(LRU_NB, LRU_BW, 4 * LRU_BW).astype(BF16)
    bg = gate_b.reshape(4, D)
    ctx0 = TL // L
    lat = pl.BlockSpec((S, LRU_BW), lambda b, n: (b, n))
    ctx = pl.BlockSpec((L, LRU_BW), lambda b, n: (ctx0 + b, n))
    out_lat, out_ctx = pl.pallas_call(
        _lru_kernel,
        grid=(B, LRU_NB),
        in_specs=[
            lat, ctx,
            pl.BlockSpec((S, LRU_BW), lambda b, n: (b, LRU_NB + n)),
            pl.BlockSpec((L, LRU_BW), lambda b, n: (ctx0 + b, LRU_NB + n)),
            pl.BlockSpec((4, LRU_BW), lambda b, n: (0, n)),
            pl.BlockSpec((1, LRU_BW), lambda b, n: (0, n)),
            pl.BlockSpec((1, LRU_BW, 4 * LRU_BW), lambda b, n: (n, 0, 0)),
            pl.BlockSpec((4, LRU_BW), lambda b, n: (0, n)),
            pl.BlockSpec((2, LRU_BW), lambda b, n: (0, n)),
        ],
        out_specs=[lat, pl.BlockSpec((L, LRU_BW), lambda b, n: (b, n))],
        out_shape=[jax.ShapeDtypeStruct((TL, D), BF16), jax.ShapeDtypeStruct((TC, D), BF16)],
        scratch_shapes=[pltpu.VMEM((LRU_T, LRU_BW), F32), pltpu.VMEM((S + LRU_PAD, LRU_BW), F32)]
        + [pltpu.VMEM((LRU_T, LRU_BW), F32)] * 4,
        compiler_params=_cp("arbitrary", "arbitrary"),
        name="lru",
    )(p, p, p, p, conv_w, conv_b.reshape(1, D), wg, bg, lam)
    return out_lat, out_ctx


def _dif_kernel(*refs, n_kv, lam_init):
    q_ref = refs[0]
    kv = refs[1:1 + 2 * n_kv]
    lam_ref, sub_ref, o_ref, kcat_ref, vt_ref = refs[1 + 2 * n_kv:]

    @pl.when(pl.program_id(2) == 0)
    def _():
        off = 0
        for t in range(n_kv):
            n = kv[2 * t].shape[0]
            kcat_ref[off:off + n, :] = kv[2 * t][...]
            vt_ref[0:DIF_DV, off:off + n] = kv[2 * t + 1][...].astype(F32).T.astype(BF16)
            off += n
        vt_ref[DIF_DV:, :] = jnp.ones((vt_ref.shape[0] - DIF_DV, vt_ref.shape[1]), BF16)

    lam = lam_ref[...]
    lam_full = (jnp.exp(jnp.sum(lam[0:1] * lam[1:2], -1, keepdims=True))
                - jnp.exp(jnp.sum(lam[2:3] * lam[3:4], -1, keepdims=True)) + lam_init)
    q = q_ref[...]
    lane = lax.broadcasted_iota(jnp.int32, q.shape, 1)
    st = []
    for m in range(2):
        qm = jnp.where((lane >= 64 * m) & (lane < 64 * (m + 1)), q, jnp.zeros_like(q))
        st.append(lax.dot_general(kcat_ref[...], qm, (((1,), (1,)), ((), ())), preferred_element_type=F32))
    outs = []
    for s in st:
        e = jnp.exp2(s - jnp.max(s, 0, keepdims=True)).astype(BF16)
        r = jnp.dot(vt_ref[...], e, preferred_element_type=F32)
        outs.append(r[:DIF_DV, :] / r[DIF_DV:DIF_DV + 1, :])
    ot = outs[0] - lam_full * outs[1]
    yt = ot * lax.rsqrt(jnp.mean(ot * ot, 0, keepdims=True) + LN_EPS)
    o_ref[...] = (yt.T * sub_ref[...] * (1.0 - lam_init)).astype(BF16)


def _dif_attention(qkv, lam, subln, layer_idx):
    lam_init = 0.8 - 0.6 * math.exp(-0.3 * layer_idx)
    tq = 1024
    nh = DIF_H
    kern = functools.partial(_dif_kernel, lam_init=lam_init)
    small = [pl.BlockSpec((4, 64), lambda *g: (0, 0)), pl.BlockSpec((1, DIF_DV), lambda *g: (0, 0))]
    sub = subln.reshape(1, DIF_DV)
    ctx0 = TL // L
    sem = _cp("arbitrary", "arbitrary", "arbitrary")
    ones_rows = 16
    scratch = lambda nk: [pltpu.VMEM((nk, DIF_DV), BF16), pltpu.VMEM((DIF_DV + ones_rows, nk), BF16)]
    o_lat = pl.pallas_call(
        functools.partial(kern, n_kv=2),
        grid=(B, nh, S // tq),
        in_specs=[
            pl.BlockSpec((tq, DIF_DV), lambda b, h, i: (b * (S // tq) + i, h)),
            pl.BlockSpec((S, DIF_DV), lambda b, h, i: (b, nh + h)),
            pl.BlockSpec((S, DIF_DV), lambda b, h, i: (b, 2 * nh + h)),
            pl.BlockSpec((L, DIF_DV), lambda b, h, i: (ctx0 + b, nh + h)),
            pl.BlockSpec((L, DIF_DV), lambda b, h, i: (ctx0 + b, 2 * nh + h)),
        ] + small,
        out_specs=pl.BlockSpec((tq, DIF_DV), lambda b, h, i: (b * (S // tq) + i, h)),
        out_shape=jax.ShapeDtypeStruct((TL, D), BF16),
        scratch_shapes=scratch(S + L),
        compiler_params=sem,
        name="dif_lat",
    )(qkv, qkv, qkv, qkv, qkv, lam, sub)
    o_ctx = pl.pallas_call(
        functools.partial(kern, n_kv=1),
        grid=(B, nh, 1),
        in_specs=[
            pl.BlockSpec((L, DIF_DV), lambda b, h, i: (ctx0 + b, h)),
            pl.BlockSpec((L, DIF_DV), lambda b, h, i: (ctx0 + b, nh + h)),
            pl.BlockSpec((L, DIF_DV), lambda b, h, i: (ctx0 + b, 2 * nh + h)),
        ] + small,
        out_specs=pl.BlockSpec((L, DIF_DV), lambda b, h, i: (b, h)),
        out_shape=jax.ShapeDtypeStruct((TC, D), BF16),
        scratch_shapes=scratch(L),
        compiler_params=sem,
        name="dif_ctx",
    )(qkv, qkv, qkv, lam, sub)
    return o_lat, o_ctx


def _ret_kernel(ql_ref, qc_ref, kl_ref, kc_ref, vl_ref, vc_ref, gl_ref, gc_ref, lg_ref,
                ol_ref, oc_ref, rf_ref, rb_ref, il_ref, ic_ref):
    lg = lg_ref[0]
    lgf = lg[0:1, 0:1]
    lgb = lg[1:2, 0:1]

    def tables(C):
        row = lax.broadcasted_iota(jnp.int32, (C, C), 0).astype(F32)
        col = lax.broadcasted_iota(jnp.int32, (C, C), 1).astype(F32)
        rel = row - col
        pos = lax.broadcasted_iota(jnp.int32, (C, RET_DK), 0).astype(F32)
        return dict(
            decay=jnp.exp(jnp.abs(rel) * jnp.where(rel >= 0, lgf, lgb)),
            qdec_f=jnp.exp((pos + 1.0) * lgf),
            kdec_f=jnp.exp((C - 1.0 - pos) * lgf),
            qdec_b=jnp.exp((C - pos) * lgb),
            kdec_b=jnp.exp(pos * lgb),
            cdec_f=jnp.exp(C * lgf),
            cdec_b=jnp.exp(C * lgb),
        )

    tdot = lambda a, b: lax.dot_general(a, b, (((0,), (0,)), ((), ())), preferred_element_type=F32)

    def bwd_chunk(t, q_ref, k_ref, v_ref, i_ref, rows):
        q = q_ref[rows, :].astype(F32)
        k = k_ref[rows, :].astype(F32)
        v = v_ref[rows, :]
        i_ref[rows, :] = jnp.dot((q * t["qdec_b"]).astype(BF16), rb_ref[...].astype(BF16),
                                 preferred_element_type=F32)
        rb_ref[...] = t["cdec_b"] * rb_ref[...] + tdot((k * t["kdec_b"]).astype(BF16), v)

    def fwd_chunk(t, q_ref, k_ref, v_ref, g_ref, i_ref, o_ref, rows):
        qb = q_ref[rows, :]
        kb = k_ref[rows, :]
        v = v_ref[rows, :]
        q = qb.astype(F32)
        k = kb.astype(F32)
        s = lax.dot_general(qb, kb, (((1,), (1,)), ((), ())), preferred_element_type=F32) * t["decay"]
        o = jnp.dot(s.astype(BF16), v, preferred_element_type=F32)
        o = o + jnp.dot((q * t["qdec_f"]).astype(BF16), rf_ref[...].astype(BF16), preferred_element_type=F32)
        o = o + i_ref[rows, :]
        rf_ref[...] = t["cdec_f"] * rf_ref[...] + tdot((k * t["kdec_f"]).astype(BF16), v)
        o = o * lax.rsqrt(jnp.mean(o * o, -1, keepdims=True) + LN_EPS)
        g = g_ref[rows, :].astype(F32)
        o_ref[rows, :] = (g * jax.nn.sigmoid(g) * o).astype(BF16)

    rf_ref[...] = jnp.zeros_like(rf_ref)
    rb_ref[...] = jnp.zeros_like(rb_ref)
    tc = tables(L)
    tl = tables(RET_CH)
    lat_chunks = [pl.ds(n * RET_CH, RET_CH) for n in range(S // RET_CH)]
    all_ctx = pl.ds(0, L)

    bwd_chunk(tc, qc_ref, kc_ref, vc_ref, ic_ref, all_ctx)
    for rows in reversed(lat_chunks):
        bwd_chunk(tl, ql_ref, kl_ref, vl_ref, il_ref, rows)

    fwd_chunk(tc, qc_ref, kc_ref, vc_ref, gc_ref, ic_ref, oc_ref, all_ctx)
    for rows in lat_chunks:
        fwd_chunk(tl, ql_ref, kl_ref, vl_ref, gl_ref, il_ref, ol_ref, rows)


def _retention(p):
    log_gf = jnp.log1p(-jnp.exp2(-5.0 - jnp.arange(RET_H, dtype=F32)))
    lg = jnp.zeros((RET_H, 8, LANES), F32)
    lg = lg.at[:, 0, :].set(log_gf[:, None]).at[:, 1, :].set(log_gf[::-1][:, None])
    ctx0 = TL // L
    nk = D // RET_DK
    nv = 2 * D // RET_DV
    lat = lambda w, off: pl.BlockSpec((S, w), lambda b, h: (b, off + h))
    ctx = lambda w, off: pl.BlockSpec((L, w), lambda b, h: (ctx0 + b, off + h))
    o_lat, o_ctx = pl.pallas_call(
        _ret_kernel,
        grid=(B, RET_H),
        in_specs=[
            lat(RET_DK, 0), ctx(RET_DK, 0),
            lat(RET_DK, nk), ctx(RET_DK, nk),
            lat(RET_DV, nv), ctx(RET_DV, nv),
            lat(RET_DV, nv + RET_H), ctx(RET_DV, nv + RET_H),
            pl.BlockSpec((1, 8, LANES), lambda b, h: (h, 0, 0)),
        ],
        out_specs=[pl.BlockSpec((S, RET_DV), lambda b, h: (b, h)),
                   pl.BlockSpec((L, RET_DV), lambda b, h: (b, h))],
        out_shape=[jax.ShapeDtypeStruct((TL, 2 * D), BF16), jax.ShapeDtypeStruct((TC, 2 * D), BF16)],
        scratch_shapes=[
            pltpu.VMEM((RET_DK, RET_DV), F32), pltpu.VMEM((RET_DK, RET_DV), F32),
            pltpu.VMEM((S, RET_DV), F32), pltpu.VMEM((L, RET_DV), F32),
        ],
        compiler_params=_cp("arbitrary", "arbitrary"),
        name="retention",
    )(p, p, p, p, p, p, p, p, lg)
    return o_lat, o_ctx


def _swa_kernel(q_ref, kp_ref, kc_ref, kn_ref, kx_ref, vp_ref, vc_ref, vn_ref, vx_ref, sink_ref, o_ref):
    n = pl.program_id(2)
    nb = pl.num_programs(2)
    k2 = jnp.concatenate([kp_ref[...], kc_ref[...], kn_ref[...], kx_ref[...]], axis=0)
    v2 = jnp.concatenate([vp_ref[...], vc_ref[...], vn_ref[...], vx_ref[...]], axis=0)
    nk = k2.shape[0]
    v2 = jnp.concatenate([v2, jnp.ones((nk, LANES), BF16)], axis=1)
    qi = lax.broadcasted_iota(jnp.int32, (QB, nk), 0)
    kj = lax.broadcasted_iota(jnp.int32, (QB, nk), 1)
    band = (kj >= qi) & (kj <= qi + 2 * WINDOW)
    band = band & ((kj >= WINDOW) | (n > 0)) & ((kj < WINDOW + QB) | (n < nb - 1))
    valid = band | (kj >= 2 * WINDOW + QB)
    bias = jnp.where(valid, 0.0, NEG_INF).astype(F32)
    bias = jnp.concatenate([bias] * SWA_G, axis=0)
    lane = lax.broadcasted_iota(jnp.int32, (QB, LANES), 1)
    lo = lane < SWA_DH
    qcols = [q_ref[:, c * LANES:(c + 1) * LANES].astype(F32) for c in range(4)]
    qswap = [pltpu.roll(x, SWA_DH, 1) for x in qcols]
    out_cols = [None] * 4
    heads_of = lambda t: [(2 * t + g // 2, g % 2) for g in range(SWA_G)]
    scores = []
    for t in range(2):
        keep = lo if t == 0 else ~lo
        qs = jnp.concatenate(
            [jnp.where(keep, qcols[c] if hf == t else qswap[c], 0.0).astype(BF16) for c, hf in heads_of(t)],
            axis=0)
        scores.append(lax.dot_general(qs, k2, (((1,), (1,)), ((), ())), preferred_element_type=F32) + bias)
    for t, s in enumerate(scores):
        heads = heads_of(t)
        sink = jnp.concatenate(
            [jnp.broadcast_to(sink_ref[2 * c + hf][:, 0:1] * LOG2E, (QB, 1)) for c, hf in heads], axis=0)
        mx = jnp.maximum(jnp.max(s, -1, keepdims=True), sink)
        e = jnp.exp2(s - mx).astype(BF16)
        r = jnp.dot(e, v2, preferred_element_type=F32)
        o = r[:, :LANES] / (r[:, LANES:] + jnp.exp2(sink - mx))
        for g, (c, hf) in enumerate(heads):
            og = o[g * QB:(g + 1) * QB, :]
            if hf != t:
                og = pltpu.roll(og, SWA_DH, 1)
            og = jnp.where(lo if hf == 0 else ~lo, og, 0.0)
            out_cols[c] = og if out_cols[c] is None else out_cols[c] + og
    for c in range(4):
        o_ref[:, c * LANES:(c + 1) * LANES] = out_cols[c].astype(BF16)


def _swa_attention(qkv, sink):
    nqb = S // QB
    kcol = SWA_H * SWA_DH // LANES
    vcol = kcol + SWA_KV * SWA_DH // LANES
    ctx0 = TL // L
    sink_t = jnp.broadcast_to(sink.astype(F32)[:, None, None], (SWA_H, 1, LANES))

    per = QB // WINDOW
    nhb = S // WINDOW

    def halo(col0, after):
        def index_map(b, p, n):
            blk = (n + 1) * per if after else n * per - 1
            return (b * nhb + jnp.clip(blk, 0, nhb - 1), col0 + p)
        return pl.BlockSpec((WINDOW, LANES), index_map)

    cur = lambda col0: pl.BlockSpec((QB, LANES), lambda b, p, n: (b * nqb + n, col0 + p))
    ctx = lambda col0: pl.BlockSpec((L, LANES), lambda b, p, n: (ctx0 + b, col0 + p))
    return pl.pallas_call(
        _swa_kernel,
        grid=(B, SWA_KV // 2, nqb),
        in_specs=[
            pl.BlockSpec((QB, 4 * LANES), lambda b, p, n: (b * nqb + n, p)),
            halo(kcol, False), cur(kcol), halo(kcol, True), ctx(kcol),
            halo(vcol, False), cur(vcol), halo(vcol, True), ctx(vcol),
            pl.BlockSpec((8, 1, LANES), lambda b, p, n: (p, 0, 0)),
        ],
        out_specs=pl.BlockSpec((QB, 4 * LANES), lambda b, p, n: (b * nqb + n, p)),
        out_shape=jax.ShapeDtypeStruct((TL, D), BF16),
        compiler_params=_cp("arbitrary", "arbitrary", "arbitrary"),
        name="swa",
    )(qkv, qkv, qkv, qkv, qkv, qkv, qkv, qkv, qkv, sink_t)


@jax.jit
def _forward(x, c, ctx, c_ctx, ada_w, ada_b, ln_g, ln_b, mlp_w1, mlp_w2,
             lru_w_in, lru_conv_w, lru_conv_b, lru_gate_w, lru_gate_b, lru_lambda, lru_w_out,
             dif_w_qkv, dif_lambda, dif_subln, dif_w_out, ret_w_qkvg, ret_w_out,
             swa_w_qkv, swa_sink, swa_w_out):
    cvec = jnp.concatenate([c, c_ctx[None, :], jnp.zeros((NSEG - B - 1, D), F32)], 0)
    mods = _ada(cvec, ada_w, ada_b)
    tables = _rope_tables(TM)
    h, u = _modulate(x.reshape(TL, D), ctx.reshape(TC, D), mods, 0)
    w_outs = [lru_w_out[0], dif_w_out[0], ret_w_out[0], swa_w_out[0]]
    w1 = mlp_w1.reshape(DEPTH * D, D_FF)
    w2 = mlp_w2.reshape(DEPTH * D_FF, D)
    qscale = SWA_DH ** -0.5 * LOG2E
    for i in range(DEPTH):
        last = i == DEPTH - 1
        n_rows = TL if last else T
        side = ((w1, i, D), (w2, i, D_FF), (w_outs[i], 0, w_outs[i].shape[0]))
        if i == 0:
            ui = jnp.concatenate([_interleave(u[:TL], S), _interleave(u[TL:], L)], 0)
            p, wb = _proj(ui, lru_w_in[0], 1024, side=side)
            o, oc = _lru(p, lru_conv_w[0], lru_conv_b[0], lru_gate_w[0], lru_gate_b[0], lru_lambda[0])
            o, oc = _deinterleave(o, S), _deinterleave(oc, L)
        elif i == 1:
            p, wb = _proj(u, dif_w_qkv[0], 1024, rope=(D, 2 * D, qscale, tables), side=side)
            o, oc = _dif_attention(p, dif_lambda[0], dif_subln[0], i)
        elif i == 2:
            p, wb = _proj(u, ret_w_qkvg[0], 1024, qscale_cols=(D, RET_DK ** -0.5), side=side)
            o, oc = _retention(p)
        else:
            p, wb = _proj(u, swa_w_qkv[0], 512, rope=(SWA_H * SWA_DH, (SWA_H + SWA_KV) * SWA_DH, qscale, tables),
                          side=side)
            o, oc = _swa_attention(p, swa_sink[0]), None
        w1b, w2b, wob = wb
        h, u2 = _outln(o, oc, wob, h, mods, i, ln_g[i, 0:1], ln_b[i, 0:1])
        h, u = _mlp(u2, w1b, w2b, h, mods, i, ln_g[i, 1:2], ln_b[i, 1:2], n_rows)
    return h.reshape(B, S, D)


def kernel(x, c, ctx, c_ctx, ada_w, ada_b, ln_g, ln_b, mlp_w1, mlp_w2, lru_w_in, lru_conv_w, lru_conv_b,
           lru_gate_w, lru_gate_b, lru_lambda, lru_w_out, dif_w_qkv, dif_lambda, dif_subln, dif_w_out,
           ret_w_qkvg, ret_w_out, swa_w_qkv, swa_sink, swa_w_out):
    return _forward(x, c, ctx, c_ctx, ada_w, ada_b, ln_g, ln_b, mlp_w1, mlp_w2,
                    lru_w_in, lru_conv_w, lru_conv_b, lru_gate_w, lru_gate_b, lru_lambda, lru_w_out,
                    dif_w_qkv, dif_lambda, dif_subln, dif_w_out, ret_w_qkvg, ret_w_out,
                    swa_w_qkv, swa_sink, swa_w_out)
```

```python
import functools
import math

import jax
import jax.numpy as jnp
from jax import lax
from jax.experimental import pallas as pl
from jax.experimental.pallas import tpu as pltpu

F32 = jnp.float32
BF16 = jnp.bfloat16

D = 2048
B = 4
S = 2048
L = 256
DEPTH = 4
GRID_W = 64
TL = B * S
TC = B * L
T = TL + TC
NSEG = 8
ADA = 6
D_FF = 4 * D
ALPHA = (2 * DEPTH) ** 0.25
LN_EPS = 1e-5
ROPE_BASE = 10000.0
NEG_INF = -1e30

LRU_BW = 256
LRU_NB = D // LRU_BW
LRU_C = 8.0
DIF_H = 16
DIF_DV = 128
RET_H = 8
RET_DK = 256
RET_DV = 512
RET_CH = 512
SWA_H = 32
SWA_KV = 8
SWA_DH = 64
SWA_G = SWA_H // SWA_KV
WINDOW = 128
QB = 128

VMEM_LIMIT = 56 * 1024 * 1024
LANES = 128

TM = 1024
TM_LN = 512
TM_MLP = 1024
PROJ_SUB = 256
LN_SUB = 256
MLP_SUB = 1024
LOG2E = math.log2(math.e)


def _cp(*sem):
    return pltpu.CompilerParams(dimension_semantics=sem, vmem_limit_bytes=VMEM_LIMIT)


def _seg(i, tm):
    return jnp.minimum((i * tm) // S, B)


def _ada_kernel(c_ref, w_ref, b_ref, o_ref):
    c = c_ref[...]
    a = (c * jax.nn.sigmoid(c)).astype(BF16)
    o_ref[0] = jnp.dot(a, w_ref[0].astype(BF16), preferred_element_type=F32) + b_ref[0]


def _ada(cvec, ada_w, ada_b):
    tn = 1024
    out = pl.pallas_call(
        _ada_kernel,
        grid=(DEPTH, ADA * D // tn),
        in_specs=[
            pl.BlockSpec((NSEG, D), lambda l, j: (0, 0)),
            pl.BlockSpec((1, D, tn), lambda l, j: (l, 0, j)),
            pl.BlockSpec((1, 1, tn), lambda l, j: (l, 0, j)),
        ],
        out_specs=pl.BlockSpec((1, NSEG, tn), lambda l, j: (l, 0, j)),
        out_shape=jax.ShapeDtypeStruct((DEPTH, NSEG, ADA * D), F32),
        compiler_params=_cp("arbitrary", "arbitrary"),
        name="ada",
    )(cvec, ada_w, ada_b.reshape(DEPTH, 1, ADA * D))
    return out.reshape(DEPTH * NSEG * ADA, 1, D)


def _mod_spec(layer, chunk, tm, grid_pos):
    base = layer * NSEG * ADA + chunk

    def index_map(*g):
        return (base + _seg(g[grid_pos], tm) * ADA, 0, 0)

    return pl.BlockSpec((1, 1, D), index_map)


def _modulate_kernel(x_ref, ctx_ref, sc_ref, sh_ref, h_ref, u_ref, *, nlat):
    def emit(src_ref):
        v = src_ref[...]
        h_ref[...] = v
        u_ref[...] = (v * (1.0 + sc_ref[0]) + sh_ref[0]).astype(BF16)

    pl.when(pl.program_id(0) < nlat)(lambda: emit(x_ref))
    pl.when(pl.program_id(0) >= nlat)(lambda: emit(ctx_ref))


def _modulate(x, ctx, mods, layer):
    tm = TM_LN
    nlat = TL // tm
    return pl.pallas_call(
        functools.partial(_modulate_kernel, nlat=nlat),
        grid=(T // tm,),
        in_specs=[
            pl.BlockSpec((tm, D), lambda i: (jnp.minimum(i, nlat - 1), 0)),
            pl.BlockSpec((tm, D), lambda i: (jnp.maximum(i - nlat, 0), 0)),
            _mod_spec(layer, 1, tm, 0),
            _mod_spec(layer, 0, tm, 0),
        ],
        out_specs=[pl.BlockSpec((tm, D), lambda i: (i, 0))] * 2,
        out_shape=[jax.ShapeDtypeStruct((T, D), F32), jax.ShapeDtypeStruct((T, D), BF16)],
        compiler_params=_cp("arbitrary"),
        name="modulate",
    )(x, ctx, mods, mods)


SIDE_BLOCKS = 32


def _call_with_side(kernel_fn, *, grid, in_specs, out_specs, out_shape, scratch_shapes, args, side, name):
    n_in, n_out, n_side = len(in_specs), len(out_specs), len(side)
    strides = [math.prod(grid[k + 1:]) for k in range(len(grid))]
    assert math.prod(grid) >= SIDE_BLOCKS
    step_of = lambda g: sum(gk * st for gk, st in zip(g, strides))
    blk = lambda *g: jnp.minimum(step_of(g), SIDE_BLOCKS - 1)
    side_in, side_out, side_shape = [], [], []
    for stacked, layer, rows in side:
        cols = stacked.shape[1]
        br = rows // SIDE_BLOCKS
        side_in.append(pl.BlockSpec((br, cols), lambda *g, layer=layer: (layer * SIDE_BLOCKS + blk(*g), 0)))
        side_out.append(pl.BlockSpec((br, cols), lambda *g: (blk(*g), 0)))
        side_shape.append(jax.ShapeDtypeStruct((rows, cols), BF16))

    def wrapped(*refs):
        ins, refs = refs[:n_in], refs[n_in:]
        srcs, refs = refs[:n_side], refs[n_side:]
        outs, refs = refs[:n_out], refs[n_out:]
        dsts, scratch = refs[:n_side], refs[n_side:]

        @pl.when(step_of([pl.program_id(k) for k in range(len(grid))]) < SIDE_BLOCKS)
        def _():
            for src, dst in zip(srcs, dsts):
                dst[...] = src[...].astype(BF16)

        kernel_fn(*ins, *outs, *scratch)

    res = pl.pallas_call(
        wrapped,
        grid=grid,
        in_specs=list(in_specs) + side_in,
        out_specs=list(out_specs) + side_out,
        out_shape=list(out_shape) + side_shape,
        scratch_shapes=scratch_shapes,
        compiler_params=_cp(*["arbitrary"] * len(grid)),
        name=name,
    )(*args, *[s[0] for s in side])
    return res[:n_out], res[n_out:]


def _rope_tables(tm):
    rows = S // GRID_W
    row = jnp.repeat(jnp.arange(rows, dtype=F32), GRID_W)
    col = jnp.tile(jnp.arange(GRID_W, dtype=F32), rows)
    half = SWA_DH // 2
    inv_freq = ROPE_BASE ** (-jnp.arange(0, half, 2, dtype=F32) / half)
    ang_r = row[:, None] * inv_freq[None, :]
    ang_c = col[:, None] * inv_freq[None, :]
    ang = jnp.concatenate([ang_r, ang_r, ang_c, ang_c], -1)
    cos, sin = jnp.cos(ang), jnp.sin(ang)
    first = (jnp.arange(SWA_DH) % 32) < 16
    sin_up = jnp.where(first[None, :], -sin, 0.0)
    sin_dn = jnp.where(first[None, :], 0.0, sin)

    def lay(t, ident):
        t = jnp.tile(t, (1, LANES // SWA_DH))
        return jnp.concatenate([t, jnp.full((tm, LANES), ident, F32)], 0)

    return lay(cos, 1.0), lay(sin_up, 0.0), lay(sin_dn, 0.0)


def _proj_kernel(a_ref, w_ref, *refs, rope, nq, nqk, qscale):
    if rope:
        cos_ref, up_ref, dn_ref, o_ref, wb_ref = refs
    else:
        o_ref, wb_ref = refs
    j = pl.program_id(0)
    i = pl.program_id(1)

    @pl.when(i == 0)
    def _():
        wb_ref[...] = w_ref[...].astype(BF16)

    tm, tn = o_ref.shape
    subs = [pl.ds(r * PROJ_SUB, PROJ_SUB) for r in range(tm // PROJ_SUB)]
    scale = jnp.where(j < nq, qscale, 1.0).astype(F32)

    def plain(scaled):
        for rows in subs:
            acc = jnp.dot(a_ref[rows, :], wb_ref[...], preferred_element_type=F32)
            o_ref[rows, :] = (acc * scale if scaled else acc).astype(o_ref.dtype)

    if not rope:
        plain(nq > 0)
        return

    @pl.when(j < nqk)
    def _():
        for rows in subs:
            acc = jnp.dot(a_ref[rows, :], wb_ref[...], preferred_element_type=F32)
            cos = cos_ref[rows, :] * scale
            up = up_ref[rows, :] * scale
            dn = dn_ref[rows, :] * scale
            for c in range(tn // LANES):
                x = acc[:, c * LANES:(c + 1) * LANES]
                y = x * cos + pltpu.roll(x, LANES - 16, 1) * up + pltpu.roll(x, 16, 1) * dn
                o_ref[rows, c * LANES:(c + 1) * LANES] = y.astype(o_ref.dtype)

    pl.when(j >= nqk)(lambda: plain(False))


def _proj(a, w, tn, rope=None, qscale_cols=None):
    n = w.shape[1]
    tm = TM
    in_specs = [
        pl.BlockSpec((tm, D), lambda j, i: (i, 0)),
        pl.BlockSpec((D, tn), lambda j, i: (0, j)),
    ]
    args = [a, w]
    kw = dict(rope=False, nq=0, nqk=0, qscale=1.0)
    if qscale_cols is not None:
        kw = dict(rope=False, nq=qscale_cols[0] // tn, nqk=0, qscale=qscale_cols[1])
    if rope is not None:
        nq_cols, nqk_cols, qscale, tables = rope
        nlat = TL // tm
        per_seq = S // tm

        def tmap(j, i):
            return (jnp.where(i < nlat, i % per_seq, per_seq), 0)

        in_specs += [pl.BlockSpec((tm, LANES), tmap)] * 3
        args += list(tables)
        kw = dict(rope=True, nq=nq_cols // tn, nqk=nqk_cols // tn, qscale=qscale)
    return pl.pallas_call(
        functools.partial(_proj_kernel, **kw),
        grid=(n // tn, T // tm),
        in_specs=in_specs,
        out_specs=pl.BlockSpec((tm, tn), lambda j, i: (i, j)),
        out_shape=jax.ShapeDtypeStruct((T, n), BF16),
        scratch_shapes=[pltpu.VMEM((D, tn), BF16)],
        compiler_params=_cp("arbitrary", "arbitrary"),
        name="proj",
    )(*args)


def _ln_epilogue(z, lng, lnb):
    mu = jnp.mean(z, -1, keepdims=True)
    zc = z - mu
    var = jnp.mean(zc * zc, -1, keepdims=True)
    return zc * lax.rsqrt(var + LN_EPS) * lng + lnb


def _residual_ln(rows, y, h_ref, g_ref, lng_ref, lnb_ref, sc_ref, sh_ref, hout_ref, uout_ref):
    hn = _ln_epilogue(ALPHA * h_ref[rows, :] + g_ref[0] * y, lng_ref[...], lnb_ref[...])
    hout_ref[rows, :] = hn
    if uout_ref is not None:
        uout_ref[rows, :] = (hn * (1.0 + sc_ref[0]) + sh_ref[0]).astype(BF16)


def _outln_kernel(*refs, nlat):
    if nlat is None:
        ol_ref, oc_ref = refs[0], None
        refs = refs[1:]
    else:
        ol_ref, oc_ref = refs[:2]
        refs = refs[2:]
    w_ref, h_ref, g_ref, lng_ref, lnb_ref, sc_ref, sh_ref, hout_ref, uout_ref = refs

    def body(o_ref):
        for r in range(o_ref.shape[0] // LN_SUB):
            rows = pl.ds(r * LN_SUB, LN_SUB)
            y = jnp.dot(o_ref[rows, :], w_ref[...], preferred_element_type=F32)
            _residual_ln(rows, y, h_ref, g_ref, lng_ref, lnb_ref, sc_ref, sh_ref, hout_ref, uout_ref)

    if nlat is None:
        body(ol_ref)
    else:
        pl.when(pl.program_id(0) < nlat)(lambda: body(ol_ref))
        pl.when(pl.program_id(0) >= nlat)(lambda: body(oc_ref))


def _outln(o_lat, o_ctx, w_bf16, h, mods, layer, lng, lnb):
    kin = o_lat.shape[1]
    tm = TM_LN * D // kin
    const = lambda i: (0, 0)
    nlat = TL // tm
    if o_ctx is None:
        n_rows = TL
        o_specs = [pl.BlockSpec((tm, kin), lambda i: (i, 0))]
        o_args = [o_lat]
    else:
        n_rows = T
        o_specs = [pl.BlockSpec((tm, kin), lambda i: (jnp.minimum(i, nlat - 1), 0)),
                   pl.BlockSpec((tm, kin), lambda i: (jnp.maximum(i - nlat, 0), 0))]
        o_args = [o_lat, o_ctx]
    return pl.pallas_call(
        functools.partial(_outln_kernel, nlat=None if o_ctx is None else nlat),
        grid=(n_rows // tm,),
        in_specs=o_specs + [
            pl.BlockSpec((kin, D), const, pipeline_mode=pl.Buffered(1)),
            pl.BlockSpec((tm, D), lambda i: (i, 0)),
            _mod_spec(layer, 2, tm, 0),
            pl.BlockSpec((1, D), const),
            pl.BlockSpec((1, D), const),
            _mod_spec(layer, 4, tm, 0),
            _mod_spec(layer, 3, tm, 0),
        ],
        out_specs=[pl.BlockSpec((tm, D), lambda i: (i, 0)), pl.BlockSpec((tm, D), lambda i: (i, 0))],
        out_shape=[jax.ShapeDtypeStruct((n_rows, D), F32), jax.ShapeDtypeStruct((n_rows, D), BF16)],
        compiler_params=_cp("arbitrary"),
        name="outln",
    )(*o_args, w_bf16, h, mods, lng, lnb, mods, mods)


def _mlp_kernel(u_ref, w1_ref, w2_ref, h_ref, g_ref, lng_ref, lnb_ref, *refs, emit_u):
    if emit_u:
        sc_ref, sh_ref, hout_ref, uout_ref = refs
    else:
        (hout_ref,) = refs
        sc_ref = sh_ref = uout_ref = None
    j = pl.program_id(1)
    tm = u_ref.shape[0]

    @pl.when(j == 0)
    def _():
        hout_ref[...] = jnp.zeros_like(hout_ref)

    subs = [pl.ds(r * MLP_SUB, MLP_SUB) for r in range(tm // MLP_SUB)]
    w1 = w1_ref[...]
    w2 = w2_ref[...]
    hids = []
    for rows in subs:
        hid = jnp.dot(u_ref[rows, :], w1, preferred_element_type=F32)
        hids.append(jnp.square(jnp.maximum(hid, 0.0)).astype(BF16))
    for rows, hid in zip(subs, hids):
        hout_ref[rows, :] += jnp.dot(hid, w2, preferred_element_type=F32)

    @pl.when(j == pl.num_programs(1) - 1)
    def _():
        for r in range(tm // LN_SUB):
            rows = pl.ds(r * LN_SUB, LN_SUB)
            _residual_ln(rows, hout_ref[rows, :], h_ref, g_ref, lng_ref, lnb_ref, sc_ref, sh_ref,
                         hout_ref, uout_ref)


def _mlp(u, w1, w2, h, mods, layer, lng, lnb, n_rows):
    tm = TM_MLP
    tf = 1024
    emit_u = layer < DEPTH - 1
    const = lambda i, j: (0, 0)
    row = lambda i, j: (i, 0)
    nff = D_FF // tf
    once = pl.Buffered(1)
    in_specs = [
        pl.BlockSpec((tm, D), row, pipeline_mode=once),
        pl.BlockSpec((D, tf), lambda i, j: (0, j)),
        pl.BlockSpec((tf, D), lambda i, j: (j, 0)),
        pl.BlockSpec((tm, D), row, pipeline_mode=once),
        _mod_spec(layer, 5, tm, 0),
        pl.BlockSpec((1, D), const),
        pl.BlockSpec((1, D), const),
    ]
    args = [u, w1, w2, h, mods, lng, lnb]
    out_specs = [pl.BlockSpec((tm, D), row, pipeline_mode=once)]
    out_shape = [jax.ShapeDtypeStruct((n_rows, D), F32)]
    if emit_u:
        in_specs += [_mod_spec(layer + 1, 1, tm, 0), _mod_spec(layer + 1, 0, tm, 0)]
        args += [mods, mods]
        out_specs.append(pl.BlockSpec((tm, D), row, pipeline_mode=once))
        out_shape.append(jax.ShapeDtypeStruct((n_rows, D), BF16))
    res = pl.pallas_call(
        functools.partial(_mlp_kernel, emit_u=emit_u),
        grid=(n_rows // tm, nff),
        in_specs=in_specs,
        out_specs=out_specs,
        out_shape=out_shape,
        compiler_params=_cp("arbitrary", "arbitrary"),
        name="mlp",
    )(*args)
    return (res[0], res[1]) if emit_u else (res[0], None)


NSUB = 8
LRU_CHUNK = 256
LRU_T = L + S
LRU_PAD = 3 * NSUB


def _interleave(t, n):
    nseq = t.shape[0] // n
    return t.reshape(nseq, NSUB, n // NSUB, t.shape[1]).transpose(0, 2, 1, 3).reshape(t.shape)


def _deinterleave(t, n):
    nseq = t.shape[0] // n
    return t.reshape(nseq, n // NSUB, NSUB, t.shape[1]).transpose(0, 2, 1, 3).reshape(t.shape)


def _prev_segment(tile):
    s = lax.broadcasted_iota(jnp.int32, tile.shape, 0)
    return jnp.where(s >= 1, pltpu.roll(tile, 1, 0), 0.0)


def _next_segment(tile):
    s = lax.broadcasted_iota(jnp.int32, tile.shape, 0)
    return jnp.where(s < NSUB - 1, pltpu.roll(tile, NSUB - 1, 0), 0.0)


def _lru_conv(r_ref, xp_ref, cw, cb):
    n = r_ref.shape[0]
    x = r_ref[...].astype(F32)
    xp_ref[0:NSUB, :] = _prev_segment(x[n - 2 * NSUB:n - NSUB, :])
    xp_ref[NSUB:2 * NSUB, :] = _prev_segment(x[n - NSUB:n, :])
    xp_ref[2 * NSUB:n + 2 * NSUB, :] = x
    xp_ref[n + 2 * NSUB:n + 3 * NSUB, :] = _next_segment(x[0:NSUB, :])
    y = cb + cw[2:3] * x
    y = y + cw[0:1] * xp_ref[0:n, :]
    y = y + cw[1:2] * xp_ref[NSUB:n + NSUB, :]
    y = y + cw[3:4] * xp_ref[3 * NSUB:n + 3 * NSUB, :]
    return y


def _lru_scan(af_ref, bf_ref, ab_ref, bb_ref, row0, n, init_f, init_b):
    k_steps = n // NSUB

    def body(k, carry):
        hf, pf, hb, pb = carry
        rf = pl.ds(pl.multiple_of(row0 + NSUB * k, NSUB), NSUB)
        rb = pl.ds(pl.multiple_of(row0 + NSUB * (k_steps - 1 - k), NSUB), NSUB)
        a = af_ref[rf, :]
        hf = a * hf + bf_ref[rf, :]
        pf = a * pf
        bf_ref[rf, :] = hf
        af_ref[rf, :] = pf
        a = ab_ref[rb, :]
        hb = a * hb + bb_ref[rb, :]
        pb = a * pb
        bb_ref[rb, :] = hb
        ab_ref[rb, :] = pb
        return hf, pf, hb, pb

    zero = jnp.zeros((NSUB, LRU_BW), F32)
    one = jnp.ones((NSUB, LRU_BW), F32)
    hf, pf, hb, pb = lax.fori_loop(0, k_steps, body, (zero, one, zero, one))
    h = init_f
    in_f = []
    for s in range(NSUB):
        in_f.append(h)
        h = hf[s:s + 1, :] + pf[s:s + 1, :] * h
    final_f = h
    h = init_b
    in_b = [None] * NSUB
    for s in reversed(range(NSUB)):
        in_b[s] = h
        h = hb[s:s + 1, :] + pb[s:s + 1, :] * h
    return jnp.concatenate(in_f, 0), jnp.concatenate(in_b, 0), final_f, h


def _lru_kernel(gl_ref, gc_ref, rl_ref, rc_ref, cw_ref, cb_ref, wg_ref, bg_ref, lam_ref,
                ol_ref, oc_ref, x_ref, xp_ref, af_ref, bf_ref, ab_ref, bb_ref):
    cw = cw_ref[...]
    cb = cb_ref[...]
    x_ref[0:L, :] = _lru_conv(rc_ref, xp_ref, cw, cb)
    x_ref[L:LRU_T, :] = _lru_conv(rl_ref, xp_ref, cw, cb)

    lam = lam_ref[...]
    decay = LRU_C * (jnp.maximum(-lam, 0.0) + jnp.log1p(jnp.exp(-jnp.abs(lam))))
    decay_l2 = -LOG2E * decay
    wg = wg_ref[0]
    bg = bg_ref[...]
    sigmoid = lambda v: 0.5 * jnp.tanh(0.5 * v) + 0.5

    for c in range(LRU_T // LRU_CHUNK):
        rows = pl.ds(c * LRU_CHUNK, LRU_CHUNK)
        xs = x_ref[rows, :]
        z = jnp.dot(xs.astype(BF16), wg, preferred_element_type=F32)
        for d, (a_ref, b_ref) in enumerate(((af_ref, bf_ref), (ab_ref, bb_ref))):
            r = sigmoid(z[:, (2 * d) * LRU_BW:(2 * d + 1) * LRU_BW] + bg[2 * d:2 * d + 1])
            g = sigmoid(z[:, (2 * d + 1) * LRU_BW:(2 * d + 2) * LRU_BW] + bg[2 * d + 1:2 * d + 2])
            a = jnp.exp2(r * decay_l2[d:d + 1])
            a_ref[rows, :] = a
            b_ref[rows, :] = jnp.sqrt(jnp.tanh(r * decay[d:d + 1]) * (a * a + 1.0)) * (g * xs)

    zero = jnp.zeros((1, LRU_BW), F32)
    refs = (af_ref, bf_ref, ab_ref, bb_ref)
    cf_in, cb_in, cf_fin, cb_fin = _lru_scan(*refs, 0, L, zero, zero)
    lf_in, lb_in, _, _ = _lru_scan(*refs, L, S, cf_fin, cb_fin)

    def emit(gate_ref, o_ref, row0, in_f, in_b):
        reps = LRU_CHUNK // NSUB
        in_f = jnp.concatenate([in_f] * reps, 0)
        in_b = jnp.concatenate([in_b] * reps, 0)
        for c in range(gate_ref.shape[0] // LRU_CHUNK):
            src = pl.ds(row0 + c * LRU_CHUNK, LRU_CHUNK)
            dst = pl.ds(c * LRU_CHUNK, LRU_CHUNK)
            h = (bf_ref[src, :] + af_ref[src, :] * in_f) + (bb_ref[src, :] + ab_ref[src, :] * in_b)
            o_ref[dst, :] = (jax.nn.gelu(gate_ref[dst, :].astype(F32)) * h).astype(BF16)

    emit(gc_ref, oc_ref, 0, cf_in, cb_in)
    emit(gl_ref, ol_ref, L, lf_in, lb_in)


def _lru(p, conv_w, conv_b, gate_w, gate_b, lam, side):
    wg = jnp.transpose(gate_w, (2, 3, 0, 1, 4)).reshape(LRU_NB, LRU_BW, 4 * LRU_BW).astype(BF16)
    bg = gate_b.reshape(4, D)
    ctx0 = TL // L
    lat = pl.BlockSpec((S, LRU_BW), lambda b, n: (b, n))
    ctx = pl.BlockSpec((L, LRU_BW), lambda b, n: (ctx0 + b, n))
    return _call_with_side(
        _lru_kernel,
        grid=(B, LRU_NB),
        in_specs=[
            lat, ctx,
            pl.BlockSpec((S, LRU_BW), lambda b, n: (b, LRU_NB + n)),
            pl.BlockSpec((L, LRU_BW), lambda b, n: (ctx0 + b, LRU_NB + n)),
            pl.BlockSpec((4, LRU_BW), lambda b, n: (0, n)),
            pl.BlockSpec((1, LRU_BW), lambda b, n: (0, n)),
            pl.BlockSpec((1, LRU_BW, 4 * LRU_BW), lambda b, n: (n, 0, 0)),
            pl.BlockSpec((4, LRU_BW), lambda b, n: (0, n)),
            pl.BlockSpec((2, LRU_BW), lambda b, n: (0, n)),
        ],
        out_specs=[lat, pl.BlockSpec((L, LRU_BW), lambda b, n: (b, n))],
        out_shape=[jax.ShapeDtypeStruct((TL, D), BF16), jax.ShapeDtypeStruct((TC, D), BF16)],
        scratch_shapes=[pltpu.VMEM((LRU_T, LRU_BW), F32), pltpu.VMEM((S + LRU_PAD, LRU_BW), F32)]
        + [pltpu.VMEM((LRU_T, LRU_BW), F32)] * 4,
        args=(p, p, p, p, conv_w, conv_b.reshape(1, D), wg, bg, lam),
        side=side,
        name="lru",
    )


def _dif_kernel(*refs, n_kv, lam_init):
    q_ref = refs[0]
    kv = refs[1:1 + 2 * n_kv]
    lam_ref, sub_ref, o_ref, kcat_ref, vt_ref = refs[1 + 2 * n_kv:]

    @pl.when(pl.program_id(2) == 0)
    def _():
        off = 0
        for t in range(n_kv):
            n = kv[2 * t].shape[0]
            kcat_ref[off:off + n, :] = kv[2 * t][...]
            vt_ref[0:DIF_DV, off:off + n] = kv[2 * t + 1][...].astype(F32).T.astype(BF16)
            off += n
        vt_ref[DIF_DV:, :] = jnp.ones((vt_ref.shape[0] - DIF_DV, vt_ref.shape[1]), BF16)

    lam = lam_ref[...]
    lam_full = (jnp.exp(jnp.sum(lam[0:1] * lam[1:2], -1, keepdims=True))
                - jnp.exp(jnp.sum(lam[2:3] * lam[3:4], -1, keepdims=True)) + lam_init)
    q = q_ref[...]
    lane = lax.broadcasted_iota(jnp.int32, q.shape, 1)
    st = []
    for m in range(2):
        qm = jnp.where((lane >= 64 * m) & (lane < 64 * (m + 1)), q, jnp.zeros_like(q))
        st.append(lax.dot_general(kcat_ref[...], qm, (((1,), (1,)), ((), ())), preferred_element_type=F32))
    outs = []
    for s in st:
        e = jnp.exp2(s - jnp.max(s, 0, keepdims=True)).astype(BF16)
        r = jnp.dot(vt_ref[...], e, preferred_element_type=F32)
        outs.append(r[:DIF_DV, :] / r[DIF_DV:DIF_DV + 1, :])
    ot = outs[0] - lam_full * outs[1]
    yt = ot * lax.rsqrt(jnp.mean(ot * ot, 0, keepdims=True) + LN_EPS)
    o_ref[...] = (yt.T * sub_ref[...] * (1.0 - lam_init)).astype(BF16)


def _dif_attention(qkv, lam, subln, layer_idx, side):
    lam_init = 0.8 - 0.6 * math.exp(-0.3 * layer_idx)
    tq = 1024
    nh = DIF_H
    kern = functools.partial(_dif_kernel, lam_init=lam_init)
    small = [pl.BlockSpec((4, 64), lambda *g: (0, 0)), pl.BlockSpec((1, DIF_DV), lambda *g: (0, 0))]
    sub = subln.reshape(1, DIF_DV)
    ctx0 = TL // L
    sem = _cp("arbitrary", "arbitrary", "arbitrary")
    ones_rows = 16
    scratch = lambda nk: [pltpu.VMEM((nk, DIF_DV), BF16), pltpu.VMEM((DIF_DV + ones_rows, nk), BF16)]
    (o_lat,), wb = _call_with_side(
        functools.partial(kern, n_kv=2),
        grid=(B, nh, S // tq),
        in_specs=[
            pl.BlockSpec((tq, DIF_DV), lambda b, h, i: (b * (S // tq) + i, h)),
            pl.BlockSpec((S, DIF_DV), lambda b, h, i: (b, nh + h)),
            pl.BlockSpec((S, DIF_DV), lambda b, h, i: (b, 2 * nh + h)),
            pl.BlockSpec((L, DIF_DV), lambda b, h, i: (ctx0 + b, nh + h)),
            pl.BlockSpec((L, DIF_DV), lambda b, h, i: (ctx0 + b, 2 * nh + h)),
        ] + small,
        out_specs=[pl.BlockSpec((tq, DIF_DV), lambda b, h, i: (b * (S // tq) + i, h))],
        out_shape=[jax.ShapeDtypeStruct((TL, D), BF16)],
        scratch_shapes=scratch(S + L),
        args=(qkv, qkv, qkv, qkv, qkv, lam, sub),
        side=side,
        name="dif_lat",
    )
    o_ctx = pl.pallas_call(
        functools.partial(kern, n_kv=1),
        grid=(B, nh, 1),
        in_specs=[
            pl.BlockSpec((L, DIF_DV), lambda b, h, i: (ctx0 + b, h)),
            pl.BlockSpec((L, DIF_DV), lambda b, h, i: (ctx0 + b, nh + h)),
            pl.BlockSpec((L, DIF_DV), lambda b, h, i: (ctx0 + b, 2 * nh + h)),
        ] + small,
        out_specs=pl.BlockSpec((L, DIF_DV), lambda b, h, i: (b, h)),
        out_shape=jax.ShapeDtypeStruct((TC, D), BF16),
        scratch_shapes=scratch(L),
        compiler_params=sem,
        name="dif_ctx",
    )(qkv, qkv, qkv, lam, sub)
    return (o_lat, o_ctx), wb


def _ret_kernel(ql_ref, qc_ref, kl_ref, kc_ref, vl_ref, vc_ref, gl_ref, gc_ref, lg_ref,
                ol_ref, oc_ref, rf_ref, rb_ref, il_ref, ic_ref):
    lg = lg_ref[0]
    lgf = lg[0:1, 0:1]
    lgb = lg[1:2, 0:1]

    def tables(C):
        row = lax.broadcasted_iota(jnp.int32, (C, C), 0).astype(F32)
        col = lax.broadcasted_iota(jnp.int32, (C, C), 1).astype(F32)
        rel = row - col
        pos = lax.broadcasted_iota(jnp.int32, (C, RET_DK), 0).astype(F32)
        return dict(
            decay=jnp.exp(jnp.abs(rel) * jnp.where(rel >= 0, lgf, lgb)),
            qdec_f=jnp.exp((pos + 1.0) * lgf),
            kdec_f=jnp.exp((C - 1.0 - pos) * lgf),
            qdec_b=jnp.exp((C - pos) * lgb),
            kdec_b=jnp.exp(pos * lgb),
            cdec_f=jnp.exp(C * lgf),
            cdec_b=jnp.exp(C * lgb),
        )

    tdot = lambda a, b: lax.dot_general(a, b, (((0,), (0,)), ((), ())), preferred_element_type=F32)

    def bwd_chunk(t, q_ref, k_ref, v_ref, i_ref, rows):
        q = q_ref[rows, :].astype(F32)
        k = k_ref[rows, :].astype(F32)
        v = v_ref[rows, :]
        i_ref[rows, :] = jnp.dot((q * t["qdec_b"]).astype(BF16), rb_ref[...].astype(BF16),
                                 preferred_element_type=F32)
        rb_ref[...] = t["cdec_b"] * rb_ref[...] + tdot((k * t["kdec_b"]).astype(BF16), v)

    def fwd_chunk(t, q_ref, k_ref, v_ref, g_ref, i_ref, o_ref, rows):
        qb = q_ref[rows, :]
        kb = k_ref[rows, :]
        v = v_ref[rows, :]
        q = qb.astype(F32)
        k = kb.astype(F32)
        s = lax.dot_general(qb, kb, (((1,), (1,)), ((), ())), preferred_element_type=F32) * t["decay"]
        o = jnp.dot(s.astype(BF16), v, preferred_element_type=F32)
        o = o + jnp.dot((q * t["qdec_f"]).astype(BF16), rf_ref[...].astype(BF16), preferred_element_type=F32)
        o = o + i_ref[rows, :]
        rf_ref[...] = t["cdec_f"] * rf_ref[...] + tdot((k * t["kdec_f"]).astype(BF16), v)
        o = o * lax.rsqrt(jnp.mean(o * o, -1, keepdims=True) + LN_EPS)
        g = g_ref[rows, :].astype(F32)
        o_ref[rows, :] = (g * jax.nn.sigmoid(g) * o).astype(BF16)

    rf_ref[...] = jnp.zeros_like(rf_ref)
    rb_ref[...] = jnp.zeros_like(rb_ref)
    tc = tables(L)
    tl = tables(RET_CH)
    lat_chunks = [pl.ds(n * RET_CH, RET_CH) for n in range(S // RET_CH)]
    all_ctx = pl.ds(0, L)

    bwd_chunk(tc, qc_ref, kc_ref, vc_ref, ic_ref, all_ctx)
    for rows in reversed(lat_chunks):
        bwd_chunk(tl, ql_ref, kl_ref, vl_ref, il_ref, rows)

    fwd_chunk(tc, qc_ref, kc_ref, vc_ref, gc_ref, ic_ref, oc_ref, all_ctx)
    for rows in lat_chunks:
        fwd_chunk(tl, ql_ref, kl_ref, vl_ref, gl_ref, il_ref, ol_ref, rows)


def _retention(p, side):
    log_gf = jnp.log1p(-jnp.exp2(-5.0 - jnp.arange(RET_H, dtype=F32)))
    lg = jnp.zeros((RET_H, 8, LANES), F32)
    lg = lg.at[:, 0, :].set(log_gf[:, None]).at[:, 1, :].set(log_gf[::-1][:, None])
    ctx0 = TL // L
    nk = D // RET_DK
    nv = 2 * D // RET_DV
    lat = lambda w, off: pl.BlockSpec((S, w), lambda b, h: (b, off + h))
    ctx = lambda w, off: pl.BlockSpec((L, w), lambda b, h: (ctx0 + b, off + h))
    return _call_with_side(
        _ret_kernel,
        grid=(B, RET_H),
        in_specs=[
            lat(RET_DK, 0), ctx(RET_DK, 0),
            lat(RET_DK, nk), ctx(RET_DK, nk),
            lat(RET_DV, nv), ctx(RET_DV, nv),
            lat(RET_DV, nv + RET_H), ctx(RET_DV, nv + RET_H),
            pl.BlockSpec((1, 8, LANES), lambda b, h: (h, 0, 0)),
        ],
        out_specs=[pl.BlockSpec((S, RET_DV), lambda b, h: (b, h)),
                   pl.BlockSpec((L, RET_DV), lambda b, h: (b, h))],
        out_shape=[jax.ShapeDtypeStruct((TL, 2 * D), BF16), jax.ShapeDtypeStruct((TC, 2 * D), BF16)],
        scratch_shapes=[
            pltpu.VMEM((RET_DK, RET_DV), F32), pltpu.VMEM((RET_DK, RET_DV), F32),
            pltpu.VMEM((S, RET_DV), F32), pltpu.VMEM((L, RET_DV), F32),
        ],
        args=(p, p, p, p, p, p, p, p, lg),
        side=side,
        name="retention",
    )


def _swa_kernel(q_ref, kp_ref, kc_ref, kn_ref, kx_ref, vp_ref, vc_ref, vn_ref, vx_ref, sink_ref, o_ref):
    n = pl.program_id(2)
    nb = pl.num_programs(2)
    k2 = jnp.concatenate([kp_ref[...], kc_ref[...], kn_ref[...], kx_ref[...]], axis=0)
    v2 = jnp.concatenate([vp_ref[...], vc_ref[...], vn_ref[...], vx_ref[...]], axis=0)
    nk = k2.shape[0]
    v2 = jnp.concatenate([v2, jnp.ones((nk, LANES), BF16)], axis=1)
    qi = lax.broadcasted_iota(jnp.int32, (QB, nk), 0)
    kj = lax.broadcasted_iota(jnp.int32, (QB, nk), 1)
    band = (kj >= qi) & (kj <= qi + 2 * WINDOW)
    band = band & ((kj >= WINDOW) | (n > 0)) & ((kj < WINDOW + QB) | (n < nb - 1))
    valid = band | (kj >= 2 * WINDOW + QB)
    bias = jnp.where(valid, 0.0, NEG_INF).astype(F32)
    bias = jnp.concatenate([bias] * SWA_G, axis=0)
    lane = lax.broadcasted_iota(jnp.int32, (QB, LANES), 1)
    lo = lane < SWA_DH
    qcols = [q_ref[:, c * LANES:(c + 1) * LANES].astype(F32) for c in range(4)]
    qswap = [pltpu.roll(x, SWA_DH, 1) for x in qcols]
    out_cols = [None] * 4
    heads_of = lambda t: [(2 * t + g // 2, g % 2) for g in range(SWA_G)]
    scores = []
    for t in range(2):
        keep = lo if t == 0 else ~lo
        qs = jnp.concatenate(
            [jnp.where(keep, qcols[c] if hf == t else qswap[c], 0.0).astype(BF16) for c, hf in heads_of(t)],
            axis=0)
        scores.append(lax.dot_general(qs, k2, (((1,), (1,)), ((), ())), preferred_element_type=F32) + bias)
    for t, s in enumerate(scores):
        heads = heads_of(t)
        sink = jnp.concatenate(
            [jnp.broadcast_to(sink_ref[2 * c + hf][:, 0:1] * LOG2E, (QB, 1)) for c, hf in heads], axis=0)
        mx = jnp.maximum(jnp.max(s, -1, keepdims=True), sink)
        e = jnp.exp2(s - mx).astype(BF16)
        r = jnp.dot(e, v2, preferred_element_type=F32)
        o = r[:, :LANES] / (r[:, LANES:] + jnp.exp2(sink - mx))
        for g, (c, hf) in enumerate(heads):
            og = o[g * QB:(g + 1) * QB, :]
            if hf != t:
                og = pltpu.roll(og, SWA_DH, 1)
            og = jnp.where(lo if hf == 0 else ~lo, og, 0.0)
            out_cols[c] = og if out_cols[c] is None else out_cols[c] + og
    for c in range(4):
        o_ref[:, c * LANES:(c + 1) * LANES] = out_cols[c].astype(BF16)


def _swa_attention(qkv, sink, side):
    nqb = S // QB
    kcol = SWA_H * SWA_DH // LANES
    vcol = kcol + SWA_KV * SWA_DH // LANES
    ctx0 = TL // L
    sink_t = jnp.broadcast_to(sink.astype(F32)[:, None, None], (SWA_H, 1, LANES))

    per = QB // WINDOW
    nhb = S // WINDOW

    def halo(col0, after):
        def index_map(b, p, n):
            blk = (n + 1) * per if after else n * per - 1
            return (b * nhb + jnp.clip(blk, 0, nhb - 1), col0 + p)
        return pl.BlockSpec((WINDOW, LANES), index_map)

    cur = lambda col0: pl.BlockSpec((QB, LANES), lambda b, p, n: (b * nqb + n, col0 + p))
    ctx = lambda col0: pl.BlockSpec((L, LANES), lambda b, p, n: (ctx0 + b, col0 + p))
    (o_lat,), wb = _call_with_side(
        _swa_kernel,
        grid=(B, SWA_KV // 2, nqb),
        in_specs=[
            pl.BlockSpec((QB, 4 * LANES), lambda b, p, n: (b * nqb + n, p)),
            halo(kcol, False), cur(kcol), halo(kcol, True), ctx(kcol),
            halo(vcol, False), cur(vcol), halo(vcol, True), ctx(vcol),
            pl.BlockSpec((8, 1, LANES), lambda b, p, n: (p, 0, 0)),
        ],
        out_specs=[pl.BlockSpec((QB, 4 * LANES), lambda b, p, n: (b * nqb + n, p))],
        out_shape=[jax.ShapeDtypeStruct((TL, D), BF16)],
        scratch_shapes=[],
        args=(qkv, qkv, qkv, qkv, qkv, qkv, qkv, qkv, qkv, sink_t),
        side=side,
        name="swa",
    )
    return (o_lat, None), wb


@jax.jit
def _forward(x, c, ctx, c_ctx, ada_w, ada_b, ln_g, ln_b, mlp_w1, mlp_w2,
             lru_w_in, lru_conv_w, lru_conv_b, lru_gate_w, lru_gate_b, lru_lambda, lru_w_out,
             dif_w_qkv, dif_lambda, dif_subln, dif_w_out, ret_w_qkvg, ret_w_out,
             swa_w_qkv, swa_sink, swa_w_out):
    cvec = jnp.concatenate([c, c_ctx[None, :], jnp.zeros((NSEG - B - 1, D), F32)], 0)
    mods = _ada(cvec, ada_w, ada_b)
    tables = _rope_tables(TM)
    h, u = _modulate(x.reshape(TL, D), ctx.reshape(TC, D), mods, 0)
    w_outs = [lru_w_out[0], dif_w_out[0], ret_w_out[0], swa_w_out[0]]
    w1 = mlp_w1.reshape(DEPTH * D, D_FF)
    w2 = mlp_w2.reshape(DEPTH * D_FF, D)
    qscale = SWA_DH ** -0.5 * LOG2E
    for i in range(DEPTH):
        last = i == DEPTH - 1
        n_rows = TL if last else T
        side = ((w1, i, D), (w2, i, D_FF), (w_outs[i], 0, w_outs[i].shape[0]))
        if i == 0:
            ui = jnp.concatenate([_interleave(u[:TL], S), _interleave(u[TL:], L)], 0)
            p = _proj(ui, lru_w_in[0], 1024)
            (o, oc), wb = _lru(p, lru_conv_w[0], lru_conv_b[0], lru_gate_w[0], lru_gate_b[0], lru_lambda[0], side)
            o, oc = _deinterleave(o, S), _deinterleave(oc, L)
        elif i == 1:
            p = _proj(u, dif_w_qkv[0], 1024, rope=(D, 2 * D, qscale, tables))
            (o, oc), wb = _dif_attention(p, dif_lambda[0], dif_subln[0], i, side)
        elif i == 2:
            p = _proj(u, ret_w_qkvg[0], 1024, qscale_cols=(D, RET_DK ** -0.5))
            (o, oc), wb = _retention(p, side)
        else:
            p = _proj(u, swa_w_qkv[0], 512, rope=(SWA_H * SWA_DH, (SWA_H + SWA_KV) * SWA_DH, qscale, tables))
            (o, oc), wb = _swa_attention(p, swa_sink[0], side)
        w1b, w2b, wob = wb
        h, u2 = _outln(o, oc, wob, h, mods, i, ln_g[i, 0:1], ln_b[i, 0:1])
        h, u = _mlp(u2, w1b, w2b, h, mods, i, ln_g[i, 1:2], ln_b[i, 1:2], n_rows)
    return h.reshape(B, S, D)


def kernel(x, c, ctx, c_ctx, ada_w, ada_b, ln_g, ln_b, mlp_w1, mlp_w2, lru_w_in, lru_conv_w, lru_conv_b,
           lru_gate_w, lru_gate_b, lru_lambda, lru_w_out, dif_w_qkv, dif_lambda, dif_subln, dif_w_out,
           ret_w_qkvg, ret_w_out, swa_w_qkv, swa_sink, swa_w_out):
    return _forward(x, c, ctx, c_ctx, ada_w, ada_b, ln_g, ln_b, mlp_w1, mlp_w2,
                    lru_w_in, lru_conv_w, lru_conv_b, lru_gate_w, lru_gate_b, lru_lambda, lru_w_out,
                    dif_w_qkv, dif_lambda, dif_subln, dif_w_out, ret_w_qkvg, ret_w_out,
                    swa_w_qkv, swa_sink, swa_w_out)
```

```python
import functools
import math

import jax
import jax.numpy as jnp
from jax import lax
from jax.experimental import pallas as pl
from jax.experimental.pallas import tpu as pltpu

F32 = jnp.float32
BF16 = jnp.bfloat16

D = 2048
B = 4
S = 2048
L = 256
DEPTH = 4
GRID_W = 64
TL = B * S
TC = B * L
T = TL + TC
NSEG = 8
ADA = 6
D_FF = 4 * D
ALPHA = (2 * DEPTH) ** 0.25
LN_EPS = 1e-5
ROPE_BASE = 10000.0
NEG_INF = -1e30

LRU_BW = 256
LRU_NB = D // LRU_BW
LRU_C = 8.0
DIF_H = 16
DIF_DV = 128
RET_H = 8
RET_DK = 256
RET_DV = 512
RET_CH = 512
SWA_H = 32
SWA_KV = 8
SWA_DH = 64
SWA_G = SWA_H // SWA_KV
WINDOW = 128
QB = 128

VMEM_LIMIT = 56 * 1024 * 1024
LANES = 128

TM = 1024
TM_LN = 512
TM_MLP = 1024
PROJ_SUB = 256
LN_SUB = 256
MLP_SUB = 1024
LOG2E = math.log2(math.e)


def _cp(*sem):
    return pltpu.CompilerParams(dimension_semantics=sem, vmem_limit_bytes=VMEM_LIMIT)


def _seg(i, tm):
    return jnp.minimum((i * tm) // S, B)


def _ada_kernel(c_ref, w_ref, b_ref, o_ref):
    c = c_ref[...]
    a = (c * jax.nn.sigmoid(c)).astype(BF16)
    o_ref[0] = jnp.dot(a, w_ref[0].astype(BF16), preferred_element_type=F32) + b_ref[0]


def _ada(cvec, ada_w, ada_b):
    tn = 1024
    out = pl.pallas_call(
        _ada_kernel,
        grid=(DEPTH, ADA * D // tn),
        in_specs=[
            pl.BlockSpec((NSEG, D), lambda l, j: (0, 0)),
            pl.BlockSpec((1, D, tn), lambda l, j: (l, 0, j)),
            pl.BlockSpec((1, 1, tn), lambda l, j: (l, 0, j)),
        ],
        out_specs=pl.BlockSpec((1, NSEG, tn), lambda l, j: (l, 0, j)),
        out_shape=jax.ShapeDtypeStruct((DEPTH, NSEG, ADA * D), F32),
        compiler_params=_cp("arbitrary", "arbitrary"),
        name="ada",
    )(cvec, ada_w, ada_b.reshape(DEPTH, 1, ADA * D))
    return out.reshape(DEPTH * NSEG * ADA, 1, D)


def _mod_spec(layer, chunk, tm, grid_pos):
    base = layer * NSEG * ADA + chunk

    def index_map(*g):
        return (base + _seg(g[grid_pos], tm) * ADA, 0, 0)

    return pl.BlockSpec((1, 1, D), index_map)


def _modulate_kernel(x_ref, ctx_ref, sc_ref, sh_ref, h_ref, u_ref, *, nlat):
    def emit(src_ref):
        v = src_ref[...]
        h_ref[...] = v
        u_ref[...] = (v * (1.0 + sc_ref[0]) + sh_ref[0]).astype(BF16)

    pl.when(pl.program_id(0) < nlat)(lambda: emit(x_ref))
    pl.when(pl.program_id(0) >= nlat)(lambda: emit(ctx_ref))


def _modulate(x, ctx, mods, layer):
    tm = TM_LN
    nlat = TL // tm
    return pl.pallas_call(
        functools.partial(_modulate_kernel, nlat=nlat),
        grid=(T // tm,),
        in_specs=[
            pl.BlockSpec((tm, D), lambda i: (jnp.minimum(i, nlat - 1), 0)),
            pl.BlockSpec((tm, D), lambda i: (jnp.maximum(i - nlat, 0), 0)),
            _mod_spec(layer, 1, tm, 0),
            _mod_spec(layer, 0, tm, 0),
        ],
        out_specs=[pl.BlockSpec((tm, D), lambda i: (i, 0))] * 2,
        out_shape=[jax.ShapeDtypeStruct((T, D), F32), jax.ShapeDtypeStruct((T, D), BF16)],
        compiler_params=_cp("arbitrary"),
        name="modulate",
    )(x, ctx, mods, mods)


SIDE_BLOCKS_MAX = 128


def _call_with_side(kernel_fn, *, grid, in_specs, out_specs, out_shape, scratch_shapes, args, side, name):
    n_in, n_out, n_side = len(in_specs), len(out_specs), len(side)
    strides = [math.prod(grid[k + 1:]) for k in range(len(grid))]
    n_blocks = SIDE_BLOCKS_MAX
    while n_blocks > math.prod(grid):
        n_blocks //= 2
    step_of = lambda g: sum(gk * st for gk, st in zip(g, strides))
    blk = lambda *g: jnp.minimum(step_of(g), n_blocks - 1)
    side_in, side_out, side_shape = [], [], []
    for stacked, layer, rows in side:
        cols = stacked.shape[1]
        br = rows // n_blocks
        side_in.append(pl.BlockSpec((br, cols), lambda *g, layer=layer: (layer * n_blocks + blk(*g), 0)))
        side_out.append(pl.BlockSpec((br, cols), lambda *g: (blk(*g), 0)))
        side_shape.append(jax.ShapeDtypeStruct((rows, cols), BF16))

    def wrapped(*refs):
        ins, refs = refs[:n_in], refs[n_in:]
        srcs, refs = refs[:n_side], refs[n_side:]
        outs, refs = refs[:n_out], refs[n_out:]
        dsts, scratch = refs[:n_side], refs[n_side:]

        @pl.when(step_of([pl.program_id(k) for k in range(len(grid))]) < n_blocks)
        def _():
            for src, dst in zip(srcs, dsts):
                dst[...] = src[...].astype(BF16)

        kernel_fn(*ins, *outs, *scratch)

    res = pl.pallas_call(
        wrapped,
        grid=grid,
        in_specs=list(in_specs) + side_in,
        out_specs=list(out_specs) + side_out,
        out_shape=list(out_shape) + side_shape,
        scratch_shapes=scratch_shapes,
        compiler_params=_cp(*["arbitrary"] * len(grid)),
        name=name,
    )(*args, *[s[0] for s in side])
    return res[:n_out], res[n_out:]


def _rope_tables(tm):
    rows = S // GRID_W
    row = jnp.repeat(jnp.arange(rows, dtype=F32), GRID_W)
    col = jnp.tile(jnp.arange(GRID_W, dtype=F32), rows)
    half = SWA_DH // 2
    inv_freq = ROPE_BASE ** (-jnp.arange(0, half, 2, dtype=F32) / half)
    ang_r = row[:, None] * inv_freq[None, :]
    ang_c = col[:, None] * inv_freq[None, :]
    ang = jnp.concatenate([ang_r, ang_r, ang_c, ang_c], -1)
    cos, sin = jnp.cos(ang), jnp.sin(ang)
    first = (jnp.arange(SWA_DH) % 32) < 16
    sin_up = jnp.where(first[None, :], -sin, 0.0)
    sin_dn = jnp.where(first[None, :], 0.0, sin)

    def lay(t, ident):
        t = jnp.tile(t, (1, LANES // SWA_DH))
        return jnp.concatenate([t, jnp.full((tm, LANES), ident, F32)], 0)

    return lay(cos, 1.0), lay(sin_up, 0.0), lay(sin_dn, 0.0)


def _proj_kernel(a_ref, w_ref, *refs, rope, nq, nqk, qscale):
    if rope:
        cos_ref, up_ref, dn_ref, o_ref, wb_ref = refs
    else:
        o_ref, wb_ref = refs
    j = pl.program_id(0)
    i = pl.program_id(1)

    @pl.when(i == 0)
    def _():
        wb_ref[...] = w_ref[...].astype(BF16)

    tm, tn = o_ref.shape
    subs = [pl.ds(r * PROJ_SUB, PROJ_SUB) for r in range(tm // PROJ_SUB)]
    scale = jnp.where(j < nq, qscale, 1.0).astype(F32)

    def plain(scaled):
        for rows in subs:
            acc = jnp.dot(a_ref[rows, :], wb_ref[...], preferred_element_type=F32)
            o_ref[rows, :] = (acc * scale if scaled else acc).astype(o_ref.dtype)

    if not rope:
        plain(nq > 0)
        return

    @pl.when(j < nqk)
    def _():
        for rows in subs:
            acc = jnp.dot(a_ref[rows, :], wb_ref[...], preferred_element_type=F32)
            cos = cos_ref[rows, :] * scale
            up = up_ref[rows, :] * scale
            dn = dn_ref[rows, :] * scale
            for c in range(tn // LANES):
                x = acc[:, c * LANES:(c + 1) * LANES]
                y = x * cos + pltpu.roll(x, LANES - 16, 1) * up + pltpu.roll(x, 16, 1) * dn
                o_ref[rows, c * LANES:(c + 1) * LANES] = y.astype(o_ref.dtype)

    pl.when(j >= nqk)(lambda: plain(False))


def _proj(a, w, tn, rope=None, qscale_cols=None):
    n = w.shape[1]
    tm = TM
    in_specs = [
        pl.BlockSpec((tm, D), lambda j, i: (i, 0)),
        pl.BlockSpec((D, tn), lambda j, i: (0, j)),
    ]
    args = [a, w]
    kw = dict(rope=False, nq=0, nqk=0, qscale=1.0)
    if qscale_cols is not None:
        kw = dict(rope=False, nq=qscale_cols[0] // tn, nqk=0, qscale=qscale_cols[1])
    if rope is not None:
        nq_cols, nqk_cols, qscale, tables = rope
        nlat = TL // tm
        per_seq = S // tm

        def tmap(j, i):
            return (jnp.where(i < nlat, i % per_seq, per_seq), 0)

        in_specs += [pl.BlockSpec((tm, LANES), tmap)] * 3
        args += list(tables)
        kw = dict(rope=True, nq=nq_cols // tn, nqk=nqk_cols // tn, qscale=qscale)
    return pl.pallas_call(
        functools.partial(_proj_kernel, **kw),
        grid=(n // tn, T // tm),
        in_specs=in_specs,
        out_specs=pl.BlockSpec((tm, tn), lambda j, i: (i, j)),
        out_shape=jax.ShapeDtypeStruct((T, n), BF16),
        scratch_shapes=[pltpu.VMEM((D, tn), BF16)],
        compiler_params=_cp("arbitrary", "arbitrary"),
        name="proj",
    )(*args)


def _ln_epilogue(z, lng, lnb):
    mu = jnp.mean(z, -1, keepdims=True)
    zc = z - mu
    var = jnp.mean(zc * zc, -1, keepdims=True)
    return zc * lax.rsqrt(var + LN_EPS) * lng + lnb


def _residual_ln(rows, y, h_ref, g_ref, lng_ref, lnb_ref, sc_ref, sh_ref, hout_ref, uout_ref):
    hn = _ln_epilogue(ALPHA * h_ref[rows, :] + g_ref[0] * y, lng_ref[...], lnb_ref[...])
    hout_ref[rows, :] = hn
    if uout_ref is not None:
        uout_ref[rows, :] = (hn * (1.0 + sc_ref[0]) + sh_ref[0]).astype(BF16)


def _outln_kernel(*refs, nlat):
    if nlat is None:
        ol_ref, oc_ref = refs[0], None
        refs = refs[1:]
    else:
        ol_ref, oc_ref = refs[:2]
        refs = refs[2:]
    w_ref, h_ref, g_ref, lng_ref, lnb_ref, sc_ref, sh_ref, hout_ref, uout_ref = refs

    def body(o_ref):
        for r in range(o_ref.shape[0] // LN_SUB):
            rows = pl.ds(r * LN_SUB, LN_SUB)
            y = jnp.dot(o_ref[rows, :], w_ref[...], preferred_element_type=F32)
            _residual_ln(rows, y, h_ref, g_ref, lng_ref, lnb_ref, sc_ref, sh_ref, hout_ref, uout_ref)

    if nlat is None:
        body(ol_ref)
    else:
        pl.when(pl.program_id(0) < nlat)(lambda: body(ol_ref))
        pl.when(pl.program_id(0) >= nlat)(lambda: body(oc_ref))


def _outln(o_lat, o_ctx, w_bf16, h, mods, layer, lng, lnb):
    kin = o_lat.shape[1]
    tm = TM_LN * D // kin
    const = lambda i: (0, 0)
    nlat = TL // tm
    if o_ctx is None:
        n_rows = TL
        o_specs = [pl.BlockSpec((tm, kin), lambda i: (i, 0))]
        o_args = [o_lat]
    else:
        n_rows = T
        o_specs = [pl.BlockSpec((tm, kin), lambda i: (jnp.minimum(i, nlat - 1), 0)),
                   pl.BlockSpec((tm, kin), lambda i: (jnp.maximum(i - nlat, 0), 0))]
        o_args = [o_lat, o_ctx]
    return pl.pallas_call(
        functools.partial(_outln_kernel, nlat=None if o_ctx is None else nlat),
        grid=(n_rows // tm,),
        in_specs=o_specs + [
            pl.BlockSpec((kin, D), const, pipeline_mode=pl.Buffered(1)),
            pl.BlockSpec((tm, D), lambda i: (i, 0)),
            _mod_spec(layer, 2, tm, 0),
            pl.BlockSpec((1, D), const),
            pl.BlockSpec((1, D), const),
            _mod_spec(layer, 4, tm, 0),
            _mod_spec(layer, 3, tm, 0),
        ],
        out_specs=[pl.BlockSpec((tm, D), lambda i: (i, 0)), pl.BlockSpec((tm, D), lambda i: (i, 0))],
        out_shape=[jax.ShapeDtypeStruct((n_rows, D), F32), jax.ShapeDtypeStruct((n_rows, D), BF16)],
        compiler_params=_cp("arbitrary"),
        name="outln",
    )(*o_args, w_bf16, h, mods, lng, lnb, mods, mods)


def _mlp_kernel(u_ref, w1_ref, w2_ref, h_ref, g_ref, lng_ref, lnb_ref, *refs, emit_u):
    if emit_u:
        sc_ref, sh_ref, hout_ref, uout_ref = refs
    else:
        (hout_ref,) = refs
        sc_ref = sh_ref = uout_ref = None
    j = pl.program_id(1)
    tm = u_ref.shape[0]

    @pl.when(j == 0)
    def _():
        hout_ref[...] = jnp.zeros_like(hout_ref)

    subs = [pl.ds(r * MLP_SUB, MLP_SUB) for r in range(tm // MLP_SUB)]
    w1 = w1_ref[...]
    w2 = w2_ref[...]
    hids = []
    for rows in subs:
        hid = jnp.dot(u_ref[rows, :], w1, preferred_element_type=F32)
        hids.append(jnp.square(jnp.maximum(hid, 0.0)).astype(BF16))
    for rows, hid in zip(subs, hids):
        hout_ref[rows, :] += jnp.dot(hid, w2, preferred_element_type=F32)

    @pl.when(j == pl.num_programs(1) - 1)
    def _():
        for r in range(tm // LN_SUB):
            rows = pl.ds(r * LN_SUB, LN_SUB)
            _residual_ln(rows, hout_ref[rows, :], h_ref, g_ref, lng_ref, lnb_ref, sc_ref, sh_ref,
                         hout_ref, uout_ref)


def _mlp(u, w1, w2, h, mods, layer, lng, lnb, n_rows):
    tm = TM_MLP
    tf = 1024
    emit_u = layer < DEPTH - 1
    const = lambda i, j: (0, 0)
    row = lambda i, j: (i, 0)
    nff = D_FF // tf
    once = pl.Buffered(1)
    in_specs = [
        pl.BlockSpec((tm, D), row, pipeline_mode=once),
        pl.BlockSpec((D, tf), lambda i, j: (0, j)),
        pl.BlockSpec((tf, D), lambda i, j: (j, 0)),
        pl.BlockSpec((tm, D), row, pipeline_mode=once),
        _mod_spec(layer, 5, tm, 0),
        pl.BlockSpec((1, D), const),
        pl.BlockSpec((1, D), const),
    ]
    args = [u, w1, w2, h, mods, lng, lnb]
    out_specs = [pl.BlockSpec((tm, D), row, pipeline_mode=once)]
    out_shape = [jax.ShapeDtypeStruct((n_rows, D), F32)]
    if emit_u:
        in_specs += [_mod_spec(layer + 1, 1, tm, 0), _mod_spec(layer + 1, 0, tm, 0)]
        args += [mods, mods]
        out_specs.append(pl.BlockSpec((tm, D), row, pipeline_mode=once))
        out_shape.append(jax.ShapeDtypeStruct((n_rows, D), BF16))
    res = pl.pallas_call(
        functools.partial(_mlp_kernel, emit_u=emit_u),
        grid=(n_rows // tm, nff),
        in_specs=in_specs,
        out_specs=out_specs,
        out_shape=out_shape,
        compiler_params=_cp("arbitrary", "arbitrary"),
        name="mlp",
    )(*args)
    return (res[0], res[1]) if emit_u else (res[0], None)


NSUB = 8
LRU_CHUNK = 256
LRU_T = L + S
LRU_PAD = 3 * NSUB


def _interleave(t, n):
    nseq = t.shape[0] // n
    return t.reshape(nseq, NSUB, n // NSUB, t.shape[1]).transpose(0, 2, 1, 3).reshape(t.shape)


def _deinterleave(t, n):
    nseq = t.shape[0] // n
    return t.reshape(nseq, n // NSUB, NSUB, t.shape[1]).transpose(0, 2, 1, 3).reshape(t.shape)


def _prev_segment(tile):
    s = lax.broadcasted_iota(jnp.int32, tile.shape, 0)
    return jnp.where(s >= 1, pltpu.roll(tile, 1, 0), 0.0)


def _next_segment(tile):
    s = lax.broadcasted_iota(jnp.int32, tile.shape, 0)
    return jnp.where(s < NSUB - 1, pltpu.roll(tile, NSUB - 1, 0), 0.0)


def _lru_conv(r_ref, xp_ref, cw, cb):
    n = r_ref.shape[0]
    x = r_ref[...].astype(F32)
    xp_ref[0:NSUB, :] = _prev_segment(x[n - 2 * NSUB:n - NSUB, :])
    xp_ref[NSUB:2 * NSUB, :] = _prev_segment(x[n - NSUB:n, :])
    xp_ref[2 * NSUB:n + 2 * NSUB, :] = x
    xp_ref[n + 2 * NSUB:n + 3 * NSUB, :] = _next_segment(x[0:NSUB, :])
    y = cb + cw[2:3] * x
    y = y + cw[0:1] * xp_ref[0:n, :]
    y = y + cw[1:2] * xp_ref[NSUB:n + NSUB, :]
    y = y + cw[3:4] * xp_ref[3 * NSUB:n + 3 * NSUB, :]
    return y


def _lru_scan(af_ref, bf_ref, ab_ref, bb_ref, row0, n, init_f, init_b):
    k_steps = n // NSUB

    def body(k, carry):
        hf, pf, hb, pb = carry
        rf = pl.ds(pl.multiple_of(row0 + NSUB * k, NSUB), NSUB)
        rb = pl.ds(pl.multiple_of(row0 + NSUB * (k_steps - 1 - k), NSUB), NSUB)
        a = af_ref[rf, :]
        hf = a * hf + bf_ref[rf, :]
        pf = a * pf
        bf_ref[rf, :] = hf
        af_ref[rf, :] = pf
        a = ab_ref[rb, :]
        hb = a * hb + bb_ref[rb, :]
        pb = a * pb
        bb_ref[rb, :] = hb
        ab_ref[rb, :] = pb
        return hf, pf, hb, pb

    zero = jnp.zeros((NSUB, LRU_BW), F32)
    one = jnp.ones((NSUB, LRU_BW), F32)
    hf, pf, hb, pb = lax.fori_loop(0, k_steps, body, (zero, one, zero, one))
    h = init_f
    in_f = []
    for s in range(NSUB):
        in_f.append(h)
        h = hf[s:s + 1, :] + pf[s:s + 1, :] * h
    final_f = h
    h = init_b
    in_b = [None] * NSUB
    for s in reversed(range(NSUB)):
        in_b[s] = h
        h = hb[s:s + 1, :] + pb[s:s + 1, :] * h
    return jnp.concatenate(in_f, 0), jnp.concatenate(in_b, 0), final_f, h


def _lru_kernel(gl_ref, gc_ref, rl_ref, rc_ref, cw_ref, cb_ref, wg_ref, bg_ref, lam_ref,
                ol_ref, oc_ref, x_ref, xp_ref, af_ref, bf_ref, ab_ref, bb_ref):
    cw = cw_ref[...]
    cb = cb_ref[...]
    x_ref[0:L, :] = _lru_conv(rc_ref, xp_ref, cw, cb)
    x_ref[L:LRU_T, :] = _lru_conv(rl_ref, xp_ref, cw, cb)

    lam = lam_ref[...]
    decay = LRU_C * (jnp.maximum(-lam, 0.0) + jnp.log1p(jnp.exp(-jnp.abs(lam))))
    decay_l2 = -LOG2E * decay
    wg = wg_ref[0]
    bg = bg_ref[...]
    sigmoid = lambda v: 0.5 * jnp.tanh(0.5 * v) + 0.5

    for c in range(LRU_T // LRU_CHUNK):
        rows = pl.ds(c * LRU_CHUNK, LRU_CHUNK)
        xs = x_ref[rows, :]
        z = jnp.dot(xs.astype(BF16), wg, preferred_element_type=F32)
        for d, (a_ref, b_ref) in enumerate(((af_ref, bf_ref), (ab_ref, bb_ref))):
            r = sigmoid(z[:, (2 * d) * LRU_BW:(2 * d + 1) * LRU_BW] + bg[2 * d:2 * d + 1])
            g = sigmoid(z[:, (2 * d + 1) * LRU_BW:(2 * d + 2) * LRU_BW] + bg[2 * d + 1:2 * d + 2])
            a = jnp.exp2(r * decay_l2[d:d + 1])
            m = jnp.tanh(r * decay[d:d + 1]) * (a * a + 1.0)
            a_ref[rows, :] = a
            b_ref[rows, :] = jnp.where(m > 0.0, m * lax.rsqrt(m), 0.0) * (g * xs)

    zero = jnp.zeros((1, LRU_BW), F32)
    refs = (af_ref, bf_ref, ab_ref, bb_ref)
    cf_in, cb_in, cf_fin, cb_fin = _lru_scan(*refs, 0, L, zero, zero)
    lf_in, lb_in, _, _ = _lru_scan(*refs, L, S, cf_fin, cb_fin)

    def emit(gate_ref, o_ref, row0, in_f, in_b):
        reps = LRU_CHUNK // NSUB
        in_f = jnp.concatenate([in_f] * reps, 0)
        in_b = jnp.concatenate([in_b] * reps, 0)
        for c in range(gate_ref.shape[0] // LRU_CHUNK):
            src = pl.ds(row0 + c * LRU_CHUNK, LRU_CHUNK)
            dst = pl.ds(c * LRU_CHUNK, LRU_CHUNK)
            h = (bf_ref[src, :] + af_ref[src, :] * in_f) + (bb_ref[src, :] + ab_ref[src, :] * in_b)
            o_ref[dst, :] = (jax.nn.gelu(gate_ref[dst, :].astype(F32)) * h).astype(BF16)

    emit(gc_ref, oc_ref, 0, cf_in, cb_in)
    emit(gl_ref, ol_ref, L, lf_in, lb_in)


def _lru(p, conv_w, conv_b, gate_w, gate_b, lam, side):
    wg = jnp.transpose(gate_w, (2, 3, 0, 1, 4)).reshape(LRU_NB, LRU_BW, 4 * LRU_BW).astype(BF16)
    bg = gate_b.reshape(4, D)
    ctx0 = TL // L
    lat = pl.BlockSpec((S, LRU_BW), lambda b, n: (b, n))
    ctx = pl.BlockSpec((L, LRU_BW), lambda b, n: (ctx0 + b, n))
    return _call_with_side(
        _lru_kernel,
        grid=(B, LRU_NB),
        in_specs=[
            lat, ctx,
            pl.BlockSpec((S, LRU_BW), lambda b, n: (b, LRU_NB + n)),
            pl.BlockSpec((L, LRU_BW), lambda b, n: (ctx0 + b, LRU_NB + n)),
            pl.BlockSpec((4, LRU_BW), lambda b, n: (0, n)),
            pl.BlockSpec((1, LRU_BW), lambda b, n: (0, n)),
            pl.BlockSpec((1, LRU_BW, 4 * LRU_BW), lambda b, n: (n, 0, 0)),
            pl.BlockSpec((4, LRU_BW), lambda b, n: (0, n)),
            pl.BlockSpec((2, LRU_BW), lambda b, n: (0, n)),
        ],
        out_specs=[lat, pl.BlockSpec((L, LRU_BW), lambda b, n: (b, n))],
        out_shape=[jax.ShapeDtypeStruct((TL, D), BF16), jax.ShapeDtypeStruct((TC, D), BF16)],
        scratch_shapes=[pltpu.VMEM((LRU_T, LRU_BW), F32), pltpu.VMEM((S + LRU_PAD, LRU_BW), F32)]
        + [pltpu.VMEM((LRU_T, LRU_BW), F32)] * 4,
        args=(p, p, p, p, conv_w, conv_b.reshape(1, D), wg, bg, lam),
        side=side,
        name="lru",
    )


def _dif_kernel(*refs, n_kv, lam_init):
    q_ref = refs[0]
    kv = refs[1:1 + 2 * n_kv]
    lam_ref, sub_ref, o_ref, kcat_ref, vt_ref = refs[1 + 2 * n_kv:]

    @pl.when(pl.program_id(2) == 0)
    def _():
        off = 0
        for t in range(n_kv):
            n = kv[2 * t].shape[0]
            kcat_ref[off:off + n, :] = kv[2 * t][...]
            vt_ref[0:DIF_DV, off:off + n] = kv[2 * t + 1][...].astype(F32).T.astype(BF16)
            off += n
        vt_ref[DIF_DV:, :] = jnp.ones((vt_ref.shape[0] - DIF_DV, vt_ref.shape[1]), BF16)

    lam = lam_ref[...]
    lam_full = (jnp.exp(jnp.sum(lam[0:1] * lam[1:2], -1, keepdims=True))
                - jnp.exp(jnp.sum(lam[2:3] * lam[3:4], -1, keepdims=True)) + lam_init)
    q = q_ref[...]
    lane = lax.broadcasted_iota(jnp.int32, q.shape, 1)
    st = []
    for m in range(2):
        qm = jnp.where((lane >= 64 * m) & (lane < 64 * (m + 1)), q, jnp.zeros_like(q))
        st.append(lax.dot_general(kcat_ref[...], qm, (((1,), (1,)), ((), ())), preferred_element_type=F32))
    outs = []
    for s in st:
        e = jnp.exp2(s - jnp.max(s, 0, keepdims=True)).astype(BF16)
        r = jnp.dot(vt_ref[...], e, preferred_element_type=F32)
        outs.append(r[:DIF_DV, :] / r[DIF_DV:DIF_DV + 1, :])
    ot = outs[0] - lam_full * outs[1]
    yt = ot * lax.rsqrt(jnp.mean(ot * ot, 0, keepdims=True) + LN_EPS)
    o_ref[...] = (yt.T * sub_ref[...] * (1.0 - lam_init)).astype(BF16)


def _dif_attention(qkv, lam, subln, layer_idx, side):
    lam_init = 0.8 - 0.6 * math.exp(-0.3 * layer_idx)
    tq = 1024
    nh = DIF_H
    kern = functools.partial(_dif_kernel, lam_init=lam_init)
    small = [pl.BlockSpec((4, 64), lambda *g: (0, 0)), pl.BlockSpec((1, DIF_DV), lambda *g: (0, 0))]
    sub = subln.reshape(1, DIF_DV)
    ctx0 = TL // L
    sem = _cp("arbitrary", "arbitrary", "arbitrary")
    ones_rows = 16
    scratch = lambda nk: [pltpu.VMEM((nk, DIF_DV), BF16), pltpu.VMEM((DIF_DV + ones_rows, nk), BF16)]
    (o_lat,), wb = _call_with_side(
        functools.partial(kern, n_kv=2),
        grid=(B, nh, S // tq),
        in_specs=[
            pl.BlockSpec((tq, DIF_DV), lambda b, h, i: (b * (S // tq) + i, h)),
            pl.BlockSpec((S, DIF_DV), lambda b, h, i: (b, nh + h)),
            pl.BlockSpec((S, DIF_DV), lambda b, h, i: (b, 2 * nh + h)),
            pl.BlockSpec((L, DIF_DV), lambda b, h, i: (ctx0 + b, nh + h)),
            pl.BlockSpec((L, DIF_DV), lambda b, h, i: (ctx0 + b, 2 * nh + h)),
        ] + small,
        out_specs=[pl.BlockSpec((tq, DIF_DV), lambda b, h, i: (b * (S // tq) + i, h))],
        out_shape=[jax.ShapeDtypeStruct((TL, D), BF16)],
        scratch_shapes=scratch(S + L),
        args=(qkv, qkv, qkv, qkv, qkv, lam, sub),
        side=side,
        name="dif_lat",
    )
    o_ctx = pl.pallas_call(
        functools.partial(kern, n_kv=1),
        grid=(B, nh, 1),
        in_specs=[
            pl.BlockSpec((L, DIF_DV), lambda b, h, i: (ctx0 + b, h)),
            pl.BlockSpec((L, DIF_DV), lambda b, h, i: (ctx0 + b, nh + h)),
            pl.BlockSpec((L, DIF_DV), lambda b, h, i: (ctx0 + b, 2 * nh + h)),
        ] + small,
        out_specs=pl.BlockSpec((L, DIF_DV), lambda b, h, i: (b, h)),
        out_shape=jax.ShapeDtypeStruct((TC, D), BF16),
        scratch_shapes=scratch(L),
        compiler_params=sem,
        name="dif_ctx",
    )(qkv, qkv, qkv, lam, sub)
    return (o_lat, o_ctx), wb


def _ret_kernel(ql_ref, qc_ref, kl_ref, kc_ref, vl_ref, vc_ref, gl_ref, gc_ref, lg_ref,
                ol_ref, oc_ref, rf_ref, rb_ref, il_ref, ic_ref):
    lg = lg_ref[0]
    lgf = lg[0:1, 0:1]
    lgb = lg[1:2, 0:1]

    def tables(C):
        row = lax.broadcasted_iota(jnp.int32, (C, C), 0).astype(F32)
        col = lax.broadcasted_iota(jnp.int32, (C, C), 1).astype(F32)
        rel = row - col
        pos = lax.broadcasted_iota(jnp.int32, (C, RET_DK), 0).astype(F32)
        return dict(
            decay=jnp.exp(jnp.abs(rel) * jnp.where(rel >= 0, lgf, lgb)),
            qdec_f=jnp.exp((pos + 1.0) * lgf),
            kdec_f=jnp.exp((C - 1.0 - pos) * lgf),
            qdec_b=jnp.exp((C - pos) * lgb),
            kdec_b=jnp.exp(pos * lgb),
            cdec_f=jnp.exp(C * lgf),
            cdec_b=jnp.exp(C * lgb),
        )

    tdot = lambda a, b: lax.dot_general(a, b, (((0,), (0,)), ((), ())), preferred_element_type=F32)

    def bwd_chunk(t, q_ref, k_ref, v_ref, i_ref, rows):
        q = q_ref[rows, :].astype(F32)
        k = k_ref[rows, :].astype(F32)
        v = v_ref[rows, :]
        i_ref[rows, :] = jnp.dot((q * t["qdec_b"]).astype(BF16), rb_ref[...].astype(BF16),
                                 preferred_element_type=F32)
        rb_ref[...] = t["cdec_b"] * rb_ref[...] + tdot((k * t["kdec_b"]).astype(BF16), v)

    def fwd_chunk(t, q_ref, k_ref, v_ref, g_ref, i_ref, o_ref, rows):
        qb = q_ref[rows, :]
        kb = k_ref[rows, :]
        v = v_ref[rows, :]
        q = qb.astype(F32)
        k = kb.astype(F32)
        s = lax.dot_general(qb, kb, (((1,), (1,)), ((), ())), preferred_element_type=F32) * t["decay"]
        o = jnp.dot(s.astype(BF16), v, preferred_element_type=F32)
        o = o + jnp.dot((q * t["qdec_f"]).astype(BF16), rf_ref[...].astype(BF16), preferred_element_type=F32)
        o = o + i_ref[rows, :]
        rf_ref[...] = t["cdec_f"] * rf_ref[...] + tdot((k * t["kdec_f"]).astype(BF16), v)
        o = o * lax.rsqrt(jnp.mean(o * o, -1, keepdims=True) + LN_EPS)
        g = g_ref[rows, :].astype(F32)
        o_ref[rows, :] = (g * jax.nn.sigmoid(g) * o).astype(BF16)

    rf_ref[...] = jnp.zeros_like(rf_ref)
    rb_ref[...] = jnp.zeros_like(rb_ref)
    tc = tables(L)
    tl = tables(RET_CH)
    lat_chunks = [pl.ds(n * RET_CH, RET_CH) for n in range(S // RET_CH)]
    all_ctx = pl.ds(0, L)

    bwd_chunk(tc, qc_ref, kc_ref, vc_ref, ic_ref, all_ctx)
    for rows in reversed(lat_chunks):
        bwd_chunk(tl, ql_ref, kl_ref, vl_ref, il_ref, rows)

    fwd_chunk(tc, qc_ref, kc_ref, vc_ref, gc_ref, ic_ref, oc_ref, all_ctx)
    for rows in lat_chunks:
        fwd_chunk(tl, ql_ref, kl_ref, vl_ref, gl_ref, il_ref, ol_ref, rows)


def _retention(p, side):
    log_gf = jnp.log1p(-jnp.exp2(-5.0 - jnp.arange(RET_H, dtype=F32)))
    lg = jnp.zeros((RET_H, 8, LANES), F32)
    lg = lg.at[:, 0, :].set(log_gf[:, None]).at[:, 1, :].set(log_gf[::-1][:, None])
    ctx0 = TL // L
    nk = D // RET_DK
    nv = 2 * D // RET_DV
    lat = lambda w, off: pl.BlockSpec((S, w), lambda b, h: (b, off + h))
    ctx = lambda w, off: pl.BlockSpec((L, w), lambda b, h: (ctx0 + b, off + h))
    return _call_with_side(
        _ret_kernel,
        grid=(B, RET_H),
        in_specs=[
            lat(RET_DK, 0), ctx(RET_DK, 0),
            lat(RET_DK, nk), ctx(RET_DK, nk),
            lat(RET_DV, nv), ctx(RET_DV, nv),
            lat(RET_DV, nv + RET_H), ctx(RET_DV, nv + RET_H),
            pl.BlockSpec((1, 8, LANES), lambda b, h: (h, 0, 0)),
        ],
        out_specs=[pl.BlockSpec((S, RET_DV), lambda b, h: (b, h)),
                   pl.BlockSpec((L, RET_DV), lambda b, h: (b, h))],
        out_shape=[jax.ShapeDtypeStruct((TL, 2 * D), BF16), jax.ShapeDtypeStruct((TC, 2 * D), BF16)],
        scratch_shapes=[
            pltpu.VMEM((RET_DK, RET_DV), F32), pltpu.VMEM((RET_DK, RET_DV), F32),
            pltpu.VMEM((S, RET_DV), F32), pltpu.VMEM((L, RET_DV), F32),
        ],
        args=(p, p, p, p, p, p, p, p, lg),
        side=side,
        name="retention",
    )


def _swa_kernel(q_ref, kp_ref, kc_ref, kn_ref, kx_ref, vp_ref, vc_ref, vn_ref, vx_ref, sink_ref, o_ref):
    n = pl.program_id(2)
    nb = pl.num_programs(2)
    k2 = jnp.concatenate([kp_ref[...], kc_ref[...], kn_ref[...], kx_ref[...]], axis=0)
    v2 = jnp.concatenate([vp_ref[...], vc_ref[...], vn_ref[...], vx_ref[...]], axis=0)
    nk = k2.shape[0]
    v2 = jnp.concatenate([v2, jnp.ones((nk, LANES), BF16)], axis=1)
    qi = lax.broadcasted_iota(jnp.int32, (QB, nk), 0)
    kj = lax.broadcasted_iota(jnp.int32, (QB, nk), 1)
    band = (kj >= qi) & (kj <= qi + 2 * WINDOW)
    band = band & ((kj >= WINDOW) | (n > 0)) & ((kj < WINDOW + QB) | (n < nb - 1))
    valid = band | (kj >= 2 * WINDOW + QB)
    bias = jnp.where(valid, 0.0, NEG_INF).astype(F32)
    bias = jnp.concatenate([bias] * SWA_G, axis=0)
    lane = lax.broadcasted_iota(jnp.int32, (QB, LANES), 1)
    lo = lane < SWA_DH
    qcols = [q_ref[:, c * LANES:(c + 1) * LANES].astype(F32) for c in range(4)]
    qswap = [pltpu.roll(x, SWA_DH, 1) for x in qcols]
    out_cols = [None] * 4
    heads_of = lambda t: [(2 * t + g // 2, g % 2) for g in range(SWA_G)]
    scores = []
    for t in range(2):
        keep = lo if t == 0 else ~lo
        qs = jnp.concatenate(
            [jnp.where(keep, qcols[c] if hf == t else qswap[c], 0.0).astype(BF16) for c, hf in heads_of(t)],
            axis=0)
        scores.append(lax.dot_general(qs, k2, (((1,), (1,)), ((), ())), preferred_element_type=F32) + bias)
    for t, s in enumerate(scores):
        heads = heads_of(t)
        sink = jnp.concatenate(
            [jnp.broadcast_to(sink_ref[2 * c + hf][:, 0:1] * LOG2E, (QB, 1)) for c, hf in heads], axis=0)
        mx = jnp.maximum(jnp.max(s, -1, keepdims=True), sink)
        e = jnp.exp2(s - mx).astype(BF16)
        r = jnp.dot(e, v2, preferred_element_type=F32)
        o = r[:, :LANES] / (r[:, LANES:] + jnp.exp2(sink - mx))
        for g, (c, hf) in enumerate(heads):
            og = o[g * QB:(g + 1) * QB, :]
            if hf != t:
                og = pltpu.roll(og, SWA_DH, 1)
            og = jnp.where(lo if hf == 0 else ~lo, og, 0.0)
            out_cols[c] = og if out_cols[c] is None else out_cols[c] + og
    for c in range(4):
        o_ref[:, c * LANES:(c + 1) * LANES] = out_cols[c].astype(BF16)


def _swa_attention(qkv, sink, side):
    nqb = S // QB
    kcol = SWA_H * SWA_DH // LANES
    vcol = kcol + SWA_KV * SWA_DH // LANES
    ctx0 = TL // L
    sink_t = jnp.broadcast_to(sink.astype(F32)[:, None, None], (SWA_H, 1, LANES))

    per = QB // WINDOW
    nhb = S // WINDOW

    def halo(col0, after):
        def index_map(b, p, n):
            blk = (n + 1) * per if after else n * per - 1
            return (b * nhb + jnp.clip(blk, 0, nhb - 1), col0 + p)
        return pl.BlockSpec((WINDOW, LANES), index_map)

    cur = lambda col0: pl.BlockSpec((QB, LANES), lambda b, p, n: (b * nqb + n, col0 + p))
    ctx = lambda col0: pl.BlockSpec((L, LANES), lambda b, p, n: (ctx0 + b, col0 + p))
    (o_lat,), wb = _call_with_side(
        _swa_kernel,
        grid=(B, SWA_KV // 2, nqb),
        in_specs=[
            pl.BlockSpec((QB, 4 * LANES), lambda b, p, n: (b * nqb + n, p)),
            halo(kcol, False), cur(kcol), halo(kcol, True), ctx(kcol),
            halo(vcol, False), cur(vcol), halo(vcol, True), ctx(vcol),
            pl.BlockSpec((8, 1, LANES), lambda b, p, n: (p, 0, 0)),
        ],
        out_specs=[pl.BlockSpec((QB, 4 * LANES), lambda b, p, n: (b * nqb + n, p))],
        out_shape=[jax.ShapeDtypeStruct((TL, D), BF16)],
        scratch_shapes=[],
        args=(qkv, qkv, qkv, qkv, qkv, qkv, qkv, qkv, qkv, sink_t),
        side=side,
        name="swa",
    )
    return (o_lat, None), wb


@jax.jit
def _forward(x, c, ctx, c_ctx, ada_w, ada_b, ln_g, ln_b, mlp_w1, mlp_w2,
             lru_w_in, lru_conv_w, lru_conv_b, lru_gate_w, lru_gate_b, lru_lambda, lru_w_out,
             dif_w_qkv, dif_lambda, dif_subln, dif_w_out, ret_w_qkvg, ret_w_out,
             swa_w_qkv, swa_sink, swa_w_out):
    cvec = jnp.concatenate([c, c_ctx[None, :], jnp.zeros((NSEG - B - 1, D), F32)], 0)
    mods = _ada(cvec, ada_w, ada_b)
    tables = _rope_tables(TM)
    h, u = _modulate(x.reshape(TL, D), ctx.reshape(TC, D), mods, 0)
    w_outs = [lru_w_out[0], dif_w_out[0], ret_w_out[0], swa_w_out[0]]
    w1 = mlp_w1.reshape(DEPTH * D, D_FF)
    w2 = mlp_w2.reshape(DEPTH * D_FF, D)
    qscale = SWA_DH ** -0.5 * LOG2E
    for i in range(DEPTH):
        last = i == DEPTH - 1
        n_rows = TL if last else T
        side = ((w1, i, D), (w2, i, D_FF), (w_outs[i], 0, w_outs[i].shape[0]))
        if i == 0:
            ui = jnp.concatenate([_interleave(u[:TL], S), _interleave(u[TL:], L)], 0)
            p = _proj(ui, lru_w_in[0], 1024)
            (o, oc), wb = _lru(p, lru_conv_w[0], lru_conv_b[0], lru_gate_w[0], lru_gate_b[0], lru_lambda[0], side)
            o, oc = _deinterleave(o, S), _deinterleave(oc, L)
        elif i == 1:
            p = _proj(u, dif_w_qkv[0], 1024, rope=(D, 2 * D, qscale, tables))
            (o, oc), wb = _dif_attention(p, dif_lambda[0], dif_subln[0], i, side)
        elif i == 2:
            p = _proj(u, ret_w_qkvg[0], 1024, qscale_cols=(D, RET_DK ** -0.5))
            (o, oc), wb = _retention(p, side)
        else:
            p = _proj(u, swa_w_qkv[0], 512, rope=(SWA_H * SWA_DH, (SWA_H + SWA_KV) * SWA_DH, qscale, tables))
            (o, oc), wb = _swa_attention(p, swa_sink[0], side)
        w1b, w2b, wob = wb
        h, u2 = _outln(o, oc, wob, h, mods, i, ln_g[i, 0:1], ln_b[i, 0:1])
        h, u = _mlp(u2, w1b, w2b, h, mods, i, ln_g[i, 1:2], ln_b[i, 1:2], n_rows)
    return h.reshape(B, S, D)


def kernel(x, c, ctx, c_ctx, ada_w, ada_b, ln_g, ln_b, mlp_w1, mlp_w2, lru_w_in, lru_conv_w, lru_conv_b,
           lru_gate_w, lru_gate_b, lru_lambda, lru_w_out, dif_w_qkv, dif_lambda, dif_subln, dif_w_out,
           ret_w_qkvg, ret_w_out, swa_w_qkv, swa_sink, swa_w_out):
    return _forward(x, c, ctx, c_ctx, ada_w, ada_b, ln_g, ln_b, mlp_w1, mlp_w2,
                    lru_w_in, lru_conv_w, lru_conv_b, lru_gate_w, lru_gate_b, lru_lambda, lru_w_out,
                    dif_w_qkv, dif_lambda, dif_subln, dif_w_out, ret_w_qkvg, ret_w_out,
                    swa_w_qkv, swa_sink, swa_w_out)
```

```python
import functools
import math

import jax
import jax.numpy as jnp
from jax import lax
from jax.experimental import pallas as pl
from jax.experimental.pallas import tpu as pltpu

F32 = jnp.float32
BF16 = jnp.bfloat16

D = 2048
B = 4
S = 2048
L = 256
DEPTH = 4
GRID_W = 64
TL = B * S
TC = B * L
T = TL + TC
NSEG = 8
ADA = 6
D_FF = 4 * D
ALPHA = (2 * DEPTH) ** 0.25
LN_EPS = 1e-5
ROPE_BASE = 10000.0
NEG_INF = -1e30

LRU_BW = 256
LRU_NB = D // LRU_BW
LRU_C = 8.0
DIF_H = 16
DIF_DV = 128
RET_H = 8
RET_DK = 256
RET_DV = 512
RET_CH = 512
SWA_H = 32
SWA_KV = 8
SWA_DH = 64
SWA_G = SWA_H // SWA_KV
WINDOW = 128
QB = 128

VMEM_LIMIT = 56 * 1024 * 1024
LANES = 128

TM = 1024
TM_LN = 512
TM_MLP = 1024
PROJ_SUB = 256
LN_SUB = 256
MLP_SUB = 1024
LOG2E = math.log2(math.e)


def _cp(*sem):
    return pltpu.CompilerParams(dimension_semantics=sem, vmem_limit_bytes=VMEM_LIMIT)


def _seg(i, tm):
    return jnp.minimum((i * tm) // S, B)


def _ada_kernel(c_ref, w_ref, b_ref, o_ref):
    c = c_ref[...]
    a = (c * jax.nn.sigmoid(c)).astype(BF16)
    o_ref[0] = jnp.dot(a, w_ref[0].astype(BF16), preferred_element_type=F32) + b_ref[0]


def _ada(cvec, ada_w, ada_b):
    tn = 1024
    out = pl.pallas_call(
        _ada_kernel,
        grid=(DEPTH, ADA * D // tn),
        in_specs=[
            pl.BlockSpec((NSEG, D), lambda l, j: (0, 0)),
            pl.BlockSpec((1, D, tn), lambda l, j: (l, 0, j)),
            pl.BlockSpec((1, 1, tn), lambda l, j: (l, 0, j)),
        ],
        out_specs=pl.BlockSpec((1, NSEG, tn), lambda l, j: (l, 0, j)),
        out_shape=jax.ShapeDtypeStruct((DEPTH, NSEG, ADA * D), F32),
        compiler_params=_cp("arbitrary", "arbitrary"),
        name="ada",
    )(cvec, ada_w, ada_b.reshape(DEPTH, 1, ADA * D))
    return out.reshape(DEPTH * NSEG * ADA, 1, D)


def _mod_spec(layer, chunk, tm, grid_pos):
    base = layer * NSEG * ADA + chunk

    def index_map(*g):
        return (base + _seg(g[grid_pos], tm) * ADA, 0, 0)

    return pl.BlockSpec((1, 1, D), index_map)


def _modulate_kernel(x_ref, ctx_ref, sc_ref, sh_ref, h_ref, u_ref, *, nlat):
    def emit(src_ref):
        v = src_ref[...]
        h_ref[...] = v
        u_ref[...] = (v * (1.0 + sc_ref[0]) + sh_ref[0]).astype(BF16)

    pl.when(pl.program_id(0) < nlat)(lambda: emit(x_ref))
    pl.when(pl.program_id(0) >= nlat)(lambda: emit(ctx_ref))


def _modulate(x, ctx, mods, layer):
    tm = TM_LN
    nlat = TL // tm
    return pl.pallas_call(
        functools.partial(_modulate_kernel, nlat=nlat),
        grid=(T // tm,),
        in_specs=[
            pl.BlockSpec((tm, D), lambda i: (jnp.minimum(i, nlat - 1), 0)),
            pl.BlockSpec((tm, D), lambda i: (jnp.maximum(i - nlat, 0), 0)),
            _mod_spec(layer, 1, tm, 0),
            _mod_spec(layer, 0, tm, 0),
        ],
        out_specs=[pl.BlockSpec((tm, D), lambda i: (i, 0))] * 2,
        out_shape=[jax.ShapeDtypeStruct((T, D), F32), jax.ShapeDtypeStruct((T, D), BF16)],
        compiler_params=_cp("arbitrary"),
        name="modulate",
    )(x, ctx, mods, mods)


SIDE_BLOCKS_MAX = 128


def _call_with_side(kernel_fn, *, grid, in_specs, out_specs, out_shape, scratch_shapes, args, side, name):
    n_in, n_out, n_side = len(in_specs), len(out_specs), len(side)
    strides = [math.prod(grid[k + 1:]) for k in range(len(grid))]
    n_blocks = SIDE_BLOCKS_MAX
    while n_blocks > math.prod(grid):
        n_blocks //= 2
    step_of = lambda g: sum(gk * st for gk, st in zip(g, strides))
    blk = lambda *g: jnp.minimum(step_of(g), n_blocks - 1)
    side_in, side_out, side_shape = [], [], []
    for stacked, layer, rows in side:
        cols = stacked.shape[1]
        br = rows // n_blocks
        side_in.append(pl.BlockSpec((br, cols), lambda *g, layer=layer: (layer * n_blocks + blk(*g), 0)))
        side_out.append(pl.BlockSpec((br, cols), lambda *g: (blk(*g), 0)))
        side_shape.append(jax.ShapeDtypeStruct((rows, cols), BF16))

    def wrapped(*refs):
        ins, refs = refs[:n_in], refs[n_in:]
        srcs, refs = refs[:n_side], refs[n_side:]
        outs, refs = refs[:n_out], refs[n_out:]
        dsts, scratch = refs[:n_side], refs[n_side:]

        @pl.when(step_of([pl.program_id(k) for k in range(len(grid))]) < n_blocks)
        def _():
            for src, dst in zip(srcs, dsts):
                dst[...] = src[...].astype(BF16)

        kernel_fn(*ins, *outs, *scratch)

    res = pl.pallas_call(
        wrapped,
        grid=grid,
        in_specs=list(in_specs) + side_in,
        out_specs=list(out_specs) + side_out,
        out_shape=list(out_shape) + side_shape,
        scratch_shapes=scratch_shapes,
        compiler_params=_cp(*["arbitrary"] * len(grid)),
        name=name,
    )(*args, *[s[0] for s in side])
    return res[:n_out], res[n_out:]


def _rope_tables(tm):
    rows = S // GRID_W
    row = jnp.repeat(jnp.arange(rows, dtype=F32), GRID_W)
    col = jnp.tile(jnp.arange(GRID_W, dtype=F32), rows)
    half = SWA_DH // 2
    inv_freq = ROPE_BASE ** (-jnp.arange(0, half, 2, dtype=F32) / half)
    ang_r = row[:, None] * inv_freq[None, :]
    ang_c = col[:, None] * inv_freq[None, :]
    ang = jnp.concatenate([ang_r, ang_r, ang_c, ang_c], -1)
    cos, sin = jnp.cos(ang), jnp.sin(ang)
    first = (jnp.arange(SWA_DH) % 32) < 16
    sin_up = jnp.where(first[None, :], -sin, 0.0)
    sin_dn = jnp.where(first[None, :], 0.0, sin)

    def lay(t, ident):
        t = jnp.tile(t, (1, LANES // SWA_DH))
        return jnp.concatenate([t, jnp.full((tm, LANES), ident, F32)], 0)

    return lay(cos, 1.0), lay(sin_up, 0.0), lay(sin_dn, 0.0)


def _proj_kernel(a_ref, w_ref, *refs, rope, nq, nqk, qscale):
    if rope:
        cos_ref, up_ref, dn_ref, o_ref, wb_ref = refs
    else:
        o_ref, wb_ref = refs
    j = pl.program_id(0)
    i = pl.program_id(1)

    @pl.when(i == 0)
    def _():
        wb_ref[...] = w_ref[...].astype(BF16)

    tm, tn = o_ref.shape
    subs = [pl.ds(r * PROJ_SUB, PROJ_SUB) for r in range(tm // PROJ_SUB)]
    scale = jnp.where(j < nq, qscale, 1.0).astype(F32)

    def plain(scaled):
        for rows in subs:
            acc = jnp.dot(a_ref[rows, :], wb_ref[...], preferred_element_type=F32)
            o_ref[rows, :] = (acc * scale if scaled else acc).astype(o_ref.dtype)

    if not rope:
        plain(nq > 0)
        return

    @pl.when(j < nqk)
    def _():
        for rows in subs:
            acc = jnp.dot(a_ref[rows, :], wb_ref[...], preferred_element_type=F32)
            cos = cos_ref[rows, :] * scale
            up = up_ref[rows, :] * scale
            dn = dn_ref[rows, :] * scale
            for c in range(tn // LANES):
                x = acc[:, c * LANES:(c + 1) * LANES]
                y = x * cos + pltpu.roll(x, LANES - 16, 1) * up + pltpu.roll(x, 16, 1) * dn
                o_ref[rows, c * LANES:(c + 1) * LANES] = y.astype(o_ref.dtype)

    pl.when(j >= nqk)(lambda: plain(False))


def _proj(a, w, tn, rope=None, qscale_cols=None):
    n = w.shape[1]
    tm = TM
    in_specs = [
        pl.BlockSpec((tm, D), lambda j, i: (i, 0)),
        pl.BlockSpec((D, tn), lambda j, i: (0, j)),
    ]
    args = [a, w]
    kw = dict(rope=False, nq=0, nqk=0, qscale=1.0)
    if qscale_cols is not None:
        kw = dict(rope=False, nq=qscale_cols[0] // tn, nqk=0, qscale=qscale_cols[1])
    if rope is not None:
        nq_cols, nqk_cols, qscale, tables = rope
        nlat = TL // tm
        per_seq = S // tm

        def tmap(j, i):
            return (jnp.where(i < nlat, i % per_seq, per_seq), 0)

        in_specs += [pl.BlockSpec((tm, LANES), tmap)] * 3
        args += list(tables)
        kw = dict(rope=True, nq=nq_cols // tn, nqk=nqk_cols // tn, qscale=qscale)
    return pl.pallas_call(
        functools.partial(_proj_kernel, **kw),
        grid=(n // tn, T // tm),
        in_specs=in_specs,
        out_specs=pl.BlockSpec((tm, tn), lambda j, i: (i, j)),
        out_shape=jax.ShapeDtypeStruct((T, n), BF16),
        scratch_shapes=[pltpu.VMEM((D, tn), BF16)],
        compiler_params=_cp("arbitrary", "arbitrary"),
        name="proj",
    )(*args)


def _ln_epilogue(z, lng, lnb):
    mu = jnp.mean(z, -1, keepdims=True)
    zc = z - mu
    var = jnp.mean(zc * zc, -1, keepdims=True)
    return zc * lax.rsqrt(var + LN_EPS) * lng + lnb


def _residual_ln(rows, y, h_ref, g_ref, lng_ref, lnb_ref, sc_ref, sh_ref, hout_ref, uout_ref):
    hn = _ln_epilogue(ALPHA * h_ref[rows, :] + g_ref[0] * y, lng_ref[...], lnb_ref[...])
    hout_ref[rows, :] = hn
    if uout_ref is not None:
        uout_ref[rows, :] = (hn * (1.0 + sc_ref[0]) + sh_ref[0]).astype(BF16)


def _outln_kernel(*refs, nlat):
    if nlat is None:
        ol_ref, oc_ref = refs[0], None
        refs = refs[1:]
    else:
        ol_ref, oc_ref = refs[:2]
        refs = refs[2:]
    w_ref, h_ref, g_ref, lng_ref, lnb_ref, sc_ref, sh_ref, hout_ref, uout_ref = refs

    def body(o_ref):
        for r in range(o_ref.shape[0] // LN_SUB):
            rows = pl.ds(r * LN_SUB, LN_SUB)
            y = jnp.dot(o_ref[rows, :], w_ref[...], preferred_element_type=F32)
            _residual_ln(rows, y, h_ref, g_ref, lng_ref, lnb_ref, sc_ref, sh_ref, hout_ref, uout_ref)

    if nlat is None:
        body(ol_ref)
    else:
        pl.when(pl.program_id(0) < nlat)(lambda: body(ol_ref))
        pl.when(pl.program_id(0) >= nlat)(lambda: body(oc_ref))


def _outln(o_lat, o_ctx, w_bf16, h, mods, layer, lng, lnb):
    kin = o_lat.shape[1]
    tm = TM_LN * D // kin
    const = lambda i: (0, 0)
    nlat = TL // tm
    if o_ctx is None:
        n_rows = TL
        o_specs = [pl.BlockSpec((tm, kin), lambda i: (i, 0))]
        o_args = [o_lat]
    else:
        n_rows = T
        o_specs = [pl.BlockSpec((tm, kin), lambda i: (jnp.minimum(i, nlat - 1), 0)),
                   pl.BlockSpec((tm, kin), lambda i: (jnp.maximum(i - nlat, 0), 0))]
        o_args = [o_lat, o_ctx]
    return pl.pallas_call(
        functools.partial(_outln_kernel, nlat=None if o_ctx is None else nlat),
        grid=(n_rows // tm,),
        in_specs=o_specs + [
            pl.BlockSpec((kin, D), const, pipeline_mode=pl.Buffered(1)),
            pl.BlockSpec((tm, D), lambda i: (i, 0)),
            _mod_spec(layer, 2, tm, 0),
            pl.BlockSpec((1, D), const),
            pl.BlockSpec((1, D), const),
            _mod_spec(layer, 4, tm, 0),
            _mod_spec(layer, 3, tm, 0),
        ],
        out_specs=[pl.BlockSpec((tm, D), lambda i: (i, 0)), pl.BlockSpec((tm, D), lambda i: (i, 0))],
        out_shape=[jax.ShapeDtypeStruct((n_rows, D), F32), jax.ShapeDtypeStruct((n_rows, D), BF16)],
        compiler_params=_cp("arbitrary"),
        name="outln",
    )(*o_args, w_bf16, h, mods, lng, lnb, mods, mods)


def _mlp_kernel(u_ref, w1_ref, w2_ref, h_ref, g_ref, lng_ref, lnb_ref, *refs, emit_u):
    if emit_u:
        sc_ref, sh_ref, hout_ref, uout_ref = refs
    else:
        (hout_ref,) = refs
        sc_ref = sh_ref = uout_ref = None
    j = pl.program_id(1)
    tm = u_ref.shape[0]

    @pl.when(j == 0)
    def _():
        hout_ref[...] = jnp.zeros_like(hout_ref)

    subs = [pl.ds(r * LN_SUB, LN_SUB) for r in range(tm // LN_SUB)]

    def hidden():
        w1 = w1_ref[...]
        hids = []
        for rows in subs:
            hid = jnp.dot(u_ref[rows, :], w1, preferred_element_type=F32)
            hids.append(jnp.square(jnp.maximum(hid, 0.0)).astype(BF16))
        return hids

    last = pl.num_programs(1) - 1

    @pl.when(j < last)
    def _():
        w2 = w2_ref[...]
        for rows, hid in zip(subs, hidden()):
            hout_ref[rows, :] += jnp.dot(hid, w2, preferred_element_type=F32)

    @pl.when(j == last)
    def _():
        w2 = w2_ref[...]
        for rows, hid in zip(subs, hidden()):
            acc = hout_ref[rows, :] + jnp.dot(hid, w2, preferred_element_type=F32)
            _residual_ln(rows, acc, h_ref, g_ref, lng_ref, lnb_ref, sc_ref, sh_ref, hout_ref, uout_ref)


def _mlp(u, w1, w2, h, mods, layer, lng, lnb, n_rows):
    tm = TM_MLP
    tf = 1024
    emit_u = layer < DEPTH - 1
    const = lambda i, j: (0, 0)
    row = lambda i, j: (i, 0)
    nff = D_FF // tf
    once = pl.Buffered(1)
    in_specs = [
        pl.BlockSpec((tm, D), row, pipeline_mode=once),
        pl.BlockSpec((D, tf), lambda i, j: (0, j)),
        pl.BlockSpec((tf, D), lambda i, j: (j, 0)),
        pl.BlockSpec((tm, D), row, pipeline_mode=once),
        _mod_spec(layer, 5, tm, 0),
        pl.BlockSpec((1, D), const),
        pl.BlockSpec((1, D), const),
    ]
    args = [u, w1, w2, h, mods, lng, lnb]
    out_specs = [pl.BlockSpec((tm, D), row, pipeline_mode=once)]
    out_shape = [jax.ShapeDtypeStruct((n_rows, D), F32)]
    if emit_u:
        in_specs += [_mod_spec(layer + 1, 1, tm, 0), _mod_spec(layer + 1, 0, tm, 0)]
        args += [mods, mods]
        out_specs.append(pl.BlockSpec((tm, D), row, pipeline_mode=once))
        out_shape.append(jax.ShapeDtypeStruct((n_rows, D), BF16))
    res = pl.pallas_call(
        functools.partial(_mlp_kernel, emit_u=emit_u),
        grid=(n_rows // tm, nff),
        in_specs=in_specs,
        out_specs=out_specs,
        out_shape=out_shape,
        compiler_params=_cp("arbitrary", "arbitrary"),
        name="mlp",
    )(*args)
    return (res[0], res[1]) if emit_u else (res[0], None)


NSUB = 8
LRU_CHUNK = 256
LRU_T = L + S
LRU_PAD = 3 * NSUB


def _interleave(t, n):
    nseq = t.shape[0] // n
    return t.reshape(nseq, NSUB, n // NSUB, t.shape[1]).transpose(0, 2, 1, 3).reshape(t.shape)


def _deinterleave(t, n):
    nseq = t.shape[0] // n
    return t.reshape(nseq, n // NSUB, NSUB, t.shape[1]).transpose(0, 2, 1, 3).reshape(t.shape)


def _prev_segment(tile):
    s = lax.broadcasted_iota(jnp.int32, tile.shape, 0)
    return jnp.where(s >= 1, pltpu.roll(tile, 1, 0), 0.0)


def _next_segment(tile):
    s = lax.broadcasted_iota(jnp.int32, tile.shape, 0)
    return jnp.where(s < NSUB - 1, pltpu.roll(tile, NSUB - 1, 0), 0.0)


def _lru_conv(r_ref, xp_ref, cw, cb):
    n = r_ref.shape[0]
    x = r_ref[...].astype(F32)
    xp_ref[0:NSUB, :] = _prev_segment(x[n - 2 * NSUB:n - NSUB, :])
    xp_ref[NSUB:2 * NSUB, :] = _prev_segment(x[n - NSUB:n, :])
    xp_ref[2 * NSUB:n + 2 * NSUB, :] = x
    xp_ref[n + 2 * NSUB:n + 3 * NSUB, :] = _next_segment(x[0:NSUB, :])
    y = cb + cw[2:3] * x
    y = y + cw[0:1] * xp_ref[0:n, :]
    y = y + cw[1:2] * xp_ref[NSUB:n + NSUB, :]
    y = y + cw[3:4] * xp_ref[3 * NSUB:n + 3 * NSUB, :]
    return y


def _lru_scan(af_ref, bf_ref, ab_ref, bb_ref, row0, n, init_f, init_b):
    k_steps = n // NSUB

    def body(k, carry):
        hf, pf, hb, pb = carry
        rf = pl.ds(pl.multiple_of(row0 + NSUB * k, NSUB), NSUB)
        rb = pl.ds(pl.multiple_of(row0 + NSUB * (k_steps - 1 - k), NSUB), NSUB)
        a = af_ref[rf, :]
        hf = a * hf + bf_ref[rf, :]
        pf = a * pf
        bf_ref[rf, :] = hf
        af_ref[rf, :] = pf
        a = ab_ref[rb, :]
        hb = a * hb + bb_ref[rb, :]
        pb = a * pb
        bb_ref[rb, :] = hb
        ab_ref[rb, :] = pb
        return hf, pf, hb, pb

    zero = jnp.zeros((NSUB, LRU_BW), F32)
    one = jnp.ones((NSUB, LRU_BW), F32)
    hf, pf, hb, pb = lax.fori_loop(0, k_steps, body, (zero, one, zero, one))
    h = init_f
    in_f = []
    for s in range(NSUB):
        in_f.append(h)
        h = hf[s:s + 1, :] + pf[s:s + 1, :] * h
    final_f = h
    h = init_b
    in_b = [None] * NSUB
    for s in reversed(range(NSUB)):
        in_b[s] = h
        h = hb[s:s + 1, :] + pb[s:s + 1, :] * h
    return jnp.concatenate(in_f, 0), jnp.concatenate(in_b, 0), final_f, h


def _lru_kernel(gl_ref, gc_ref, rl_ref, rc_ref, cw_ref, cb_ref, wg_ref, bg_ref, lam_ref,
                ol_ref, oc_ref, x_ref, xp_ref, af_ref, bf_ref, ab_ref, bb_ref):
    cw = cw_ref[...]
    cb = cb_ref[...]
    x_ref[0:L, :] = _lru_conv(rc_ref, xp_ref, cw, cb)
    x_ref[L:LRU_T, :] = _lru_conv(rl_ref, xp_ref, cw, cb)

    lam = lam_ref[...]
    decay = LRU_C * (jnp.maximum(-lam, 0.0) + jnp.log1p(jnp.exp(-jnp.abs(lam))))
    decay_l2 = -LOG2E * decay
    wg = wg_ref[0]
    bg = bg_ref[...]
    sigmoid = lambda v: 0.5 * jnp.tanh(0.5 * v) + 0.5

    for c in range(LRU_T // LRU_CHUNK):
        rows = pl.ds(c * LRU_CHUNK, LRU_CHUNK)
        xs = x_ref[rows, :]
        z = jnp.dot(xs.astype(BF16), wg, preferred_element_type=F32)
        for d, (a_ref, b_ref) in enumerate(((af_ref, bf_ref), (ab_ref, bb_ref))):
            r = sigmoid(z[:, (2 * d) * LRU_BW:(2 * d + 1) * LRU_BW] + bg[2 * d:2 * d + 1])
            g = sigmoid(z[:, (2 * d + 1) * LRU_BW:(2 * d + 2) * LRU_BW] + bg[2 * d + 1:2 * d + 2])
            a = jnp.exp2(r * decay_l2[d:d + 1])
            m = jnp.tanh(r * decay[d:d + 1]) * (a * a + 1.0)
            a_ref[rows, :] = a
            b_ref[rows, :] = jnp.where(m > 0.0, m * lax.rsqrt(m), 0.0) * (g * xs)

    zero = jnp.zeros((1, LRU_BW), F32)
    refs = (af_ref, bf_ref, ab_ref, bb_ref)
    cf_in, cb_in, cf_fin, cb_fin = _lru_scan(*refs, 0, L, zero, zero)
    lf_in, lb_in, _, _ = _lru_scan(*refs, L, S, cf_fin, cb_fin)

    def emit(gate_ref, o_ref, row0, in_f, in_b):
        reps = LRU_CHUNK // NSUB
        in_f = jnp.concatenate([in_f] * reps, 0)
        in_b = jnp.concatenate([in_b] * reps, 0)
        for c in range(gate_ref.shape[0] // LRU_CHUNK):
            src = pl.ds(row0 + c * LRU_CHUNK, LRU_CHUNK)
            dst = pl.ds(c * LRU_CHUNK, LRU_CHUNK)
            h = (bf_ref[src, :] + af_ref[src, :] * in_f) + (bb_ref[src, :] + ab_ref[src, :] * in_b)
            o_ref[dst, :] = (jax.nn.gelu(gate_ref[dst, :].astype(F32)) * h).astype(BF16)

    emit(gc_ref, oc_ref, 0, cf_in, cb_in)
    emit(gl_ref, ol_ref, L, lf_in, lb_in)


def _lru(p, conv_w, conv_b, gate_w, gate_b, lam, side):
    wg = jnp.transpose(gate_w, (2, 3, 0, 1, 4)).reshape(LRU_NB, LRU_BW, 4 * LRU_BW).astype(BF16)
    bg = gate_b.reshape(4, D)
    ctx0 = TL // L
    lat = pl.BlockSpec((S, LRU_BW), lambda b, n: (b, n))
    ctx = pl.BlockSpec((L, LRU_BW), lambda b, n: (ctx0 + b, n))
    return _call_with_side(
        _lru_kernel,
        grid=(B, LRU_NB),
        in_specs=[
            lat, ctx,
            pl.BlockSpec((S, LRU_BW), lambda b, n: (b, LRU_NB + n)),
            pl.BlockSpec((L, LRU_BW), lambda b, n: (ctx0 + b, LRU_NB + n)),
            pl.BlockSpec((4, LRU_BW), lambda b, n: (0, n)),
            pl.BlockSpec((1, LRU_BW), lambda b, n: (0, n)),
            pl.BlockSpec((1, LRU_BW, 4 * LRU_BW), lambda b, n: (n, 0, 0)),
            pl.BlockSpec((4, LRU_BW), lambda b, n: (0, n)),
            pl.BlockSpec((2, LRU_BW), lambda b, n: (0, n)),
        ],
        out_specs=[lat, pl.BlockSpec((L, LRU_BW), lambda b, n: (b, n))],
        out_shape=[jax.ShapeDtypeStruct((TL, D), BF16), jax.ShapeDtypeStruct((TC, D), BF16)],
        scratch_shapes=[pltpu.VMEM((LRU_T, LRU_BW), F32), pltpu.VMEM((S + LRU_PAD, LRU_BW), F32)]
        + [pltpu.VMEM((LRU_T, LRU_BW), F32)] * 4,
        args=(p, p, p, p, conv_w, conv_b.reshape(1, D), wg, bg, lam),
        side=side,
        name="lru",
    )


def _dif_kernel(*refs, n_kv, lam_init):
    q_ref = refs[0]
    kv = refs[1:1 + 2 * n_kv]
    lam_ref, sub_ref, o_ref, kcat_ref, vt_ref = refs[1 + 2 * n_kv:]

    @pl.when(pl.program_id(2) == 0)
    def _():
        off = 0
        for t in range(n_kv):
            n = kv[2 * t].shape[0]
            kcat_ref[off:off + n, :] = kv[2 * t][...]
            vt_ref[0:DIF_DV, off:off + n] = kv[2 * t + 1][...].astype(F32).T.astype(BF16)
            off += n
        vt_ref[DIF_DV:, :] = jnp.ones((vt_ref.shape[0] - DIF_DV, vt_ref.shape[1]), BF16)

    lam = lam_ref[...]
    lam_full = (jnp.exp(jnp.sum(lam[0:1] * lam[1:2], -1, keepdims=True))
                - jnp.exp(jnp.sum(lam[2:3] * lam[3:4], -1, keepdims=True)) + lam_init)
    q = q_ref[...]
    lane = lax.broadcasted_iota(jnp.int32, q.shape, 1)
    st = []
    for m in range(2):
        qm = jnp.where((lane >= 64 * m) & (lane < 64 * (m + 1)), q, jnp.zeros_like(q))
        st.append(lax.dot_general(kcat_ref[...], qm, (((1,), (1,)), ((), ())), preferred_element_type=F32))
    outs = []
    for s in st:
        e = jnp.exp2(s - jnp.max(s, 0, keepdims=True)).astype(BF16)
        r = jnp.dot(vt_ref[...], e, preferred_element_type=F32)
        outs.append(r[:DIF_DV, :] / r[DIF_DV:DIF_DV + 1, :])
    ot = outs[0] - lam_full * outs[1]
    yt = ot * lax.rsqrt(jnp.mean(ot * ot, 0, keepdims=True) + LN_EPS)
    o_ref[...] = (yt.T * sub_ref[...] * (1.0 - lam_init)).astype(BF16)


def _dif_attention(qkv, lam, subln, layer_idx, side):
    lam_init = 0.8 - 0.6 * math.exp(-0.3 * layer_idx)
    tq = 1024
    nh = DIF_H
    kern = functools.partial(_dif_kernel, lam_init=lam_init)
    small = [pl.BlockSpec((4, 64), lambda *g: (0, 0)), pl.BlockSpec((1, DIF_DV), lambda *g: (0, 0))]
    sub = subln.reshape(1, DIF_DV)
    ctx0 = TL // L
    sem = _cp("arbitrary", "arbitrary", "arbitrary")
    ones_rows = 16
    scratch = lambda nk: [pltpu.VMEM((nk, DIF_DV), BF16), pltpu.VMEM((DIF_DV + ones_rows, nk), BF16)]
    (o_lat,), wb = _call_with_side(
        functools.partial(kern, n_kv=2),
        grid=(B, nh, S // tq),
        in_specs=[
            pl.BlockSpec((tq, DIF_DV), lambda b, h, i: (b * (S // tq) + i, h)),
            pl.BlockSpec((S, DIF_DV), lambda b, h, i: (b, nh + h)),
            pl.BlockSpec((S, DIF_DV), lambda b, h, i: (b, 2 * nh + h)),
            pl.BlockSpec((L, DIF_DV), lambda b, h, i: (ctx0 + b, nh + h)),
            pl.BlockSpec((L, DIF_DV), lambda b, h, i: (ctx0 + b, 2 * nh + h)),
        ] + small,
        out_specs=[pl.BlockSpec((tq, DIF_DV), lambda b, h, i: (b * (S // tq) + i, h))],
        out_shape=[jax.ShapeDtypeStruct((TL, D), BF16)],
        scratch_shapes=scratch(S + L),
        args=(qkv, qkv, qkv, qkv, qkv, lam, sub),
        side=side,
        name="dif_lat",
    )
    o_ctx = pl.pallas_call(
        functools.partial(kern, n_kv=1),
        grid=(B, nh, 1),
        in_specs=[
            pl.BlockSpec((L, DIF_DV), lambda b, h, i: (ctx0 + b, h)),
            pl.BlockSpec((L, DIF_DV), lambda b, h, i: (ctx0 + b, nh + h)),
            pl.BlockSpec((L, DIF_DV), lambda b, h, i: (ctx0 + b, 2 * nh + h)),
        ] + small,
        out_specs=pl.BlockSpec((L, DIF_DV), lambda b, h, i: (b, h)),
        out_shape=jax.ShapeDtypeStruct((TC, D), BF16),
        scratch_shapes=scratch(L),
        compiler_params=sem,
        name="dif_ctx",
    )(qkv, qkv, qkv, lam, sub)
    return (o_lat, o_ctx), wb


def _ret_kernel(ql_ref, qc_ref, kl_ref, kc_ref, vl_ref, vc_ref, gl_ref, gc_ref, lg_ref,
                ol_ref, oc_ref, rf_ref, rb_ref, il_ref, ic_ref):
    lg = lg_ref[0]
    lgf = lg[0:1, 0:1]
    lgb = lg[1:2, 0:1]

    def tables(C):
        row = lax.broadcasted_iota(jnp.int32, (C, C), 0).astype(F32)
        col = lax.broadcasted_iota(jnp.int32, (C, C), 1).astype(F32)
        rel = row - col
        pos = lax.broadcasted_iota(jnp.int32, (C, RET_DK), 0).astype(F32)
        return dict(
            decay=jnp.exp(jnp.abs(rel) * jnp.where(rel >= 0, lgf, lgb)),
            qdec_f=jnp.exp((pos + 1.0) * lgf),
            kdec_f=jnp.exp((C - 1.0 - pos) * lgf),
            qdec_b=jnp.exp((C - pos) * lgb),
            kdec_b=jnp.exp(pos * lgb),
            cdec_f=jnp.exp(C * lgf),
            cdec_b=jnp.exp(C * lgb),
        )

    tdot = lambda a, b: lax.dot_general(a, b, (((0,), (0,)), ((), ())), preferred_element_type=F32)

    def bwd_chunk(t, q_ref, k_ref, v_ref, i_ref, rows):
        q = q_ref[rows, :].astype(F32)
        k = k_ref[rows, :].astype(F32)
        v = v_ref[rows, :]
        i_ref[rows, :] = jnp.dot((q * t["qdec_b"]).astype(BF16), rb_ref[...].astype(BF16),
                                 preferred_element_type=F32)
        rb_ref[...] = t["cdec_b"] * rb_ref[...] + tdot((k * t["kdec_b"]).astype(BF16), v)

    def fwd_chunk(t, q_ref, k_ref, v_ref, g_ref, i_ref, o_ref, rows):
        qb = q_ref[rows, :]
        kb = k_ref[rows, :]
        v = v_ref[rows, :]
        q = qb.astype(F32)
        k = kb.astype(F32)
        s = lax.dot_general(qb, kb, (((1,), (1,)), ((), ())), preferred_element_type=F32) * t["decay"]
        o = jnp.dot(s.astype(BF16), v, preferred_element_type=F32)
        o = o + jnp.dot((q * t["qdec_f"]).astype(BF16), rf_ref[...].astype(BF16), preferred_element_type=F32)
        o = o + i_ref[rows, :]
        rf_ref[...] = t["cdec_f"] * rf_ref[...] + tdot((k * t["kdec_f"]).astype(BF16), v)
        o = o * lax.rsqrt(jnp.mean(o * o, -1, keepdims=True) + LN_EPS)
        g = g_ref[rows, :].astype(F32)
        o_ref[rows, :] = (g * jax.nn.sigmoid(g) * o).astype(BF16)

    rf_ref[...] = jnp.zeros_like(rf_ref)
    rb_ref[...] = jnp.zeros_like(rb_ref)
    tc = tables(L)
    tl = tables(RET_CH)
    lat_chunks = [pl.ds(n * RET_CH, RET_CH) for n in range(S // RET_CH)]
    all_ctx = pl.ds(0, L)

    bwd_chunk(tc, qc_ref, kc_ref, vc_ref, ic_ref, all_ctx)
    for rows in reversed(lat_chunks):
        bwd_chunk(tl, ql_ref, kl_ref, vl_ref, il_ref, rows)

    fwd_chunk(tc, qc_ref, kc_ref, vc_ref, gc_ref, ic_ref, oc_ref, all_ctx)
    for rows in lat_chunks:
        fwd_chunk(tl, ql_ref, kl_ref, vl_ref, gl_ref, il_ref, ol_ref, rows)


def _retention(p, side):
    log_gf = jnp.log1p(-jnp.exp2(-5.0 - jnp.arange(RET_H, dtype=F32)))
    lg = jnp.zeros((RET_H, 8, LANES), F32)
    lg = lg.at[:, 0, :].set(log_gf[:, None]).at[:, 1, :].set(log_gf[::-1][:, None])
    ctx0 = TL // L
    nk = D // RET_DK
    nv = 2 * D // RET_DV
    lat = lambda w, off: pl.BlockSpec((S, w), lambda b, h: (b, off + h))
    ctx = lambda w, off: pl.BlockSpec((L, w), lambda b, h: (ctx0 + b, off + h))
    return _call_with_side(
        _ret_kernel,
        grid=(B, RET_H),
        in_specs=[
            lat(RET_DK, 0), ctx(RET_DK, 0),
            lat(RET_DK, nk), ctx(RET_DK, nk),
            lat(RET_DV, nv), ctx(RET_DV, nv),
            lat(RET_DV, nv + RET_H), ctx(RET_DV, nv + RET_H),
            pl.BlockSpec((1, 8, LANES), lambda b, h: (h, 0, 0)),
        ],
        out_specs=[pl.BlockSpec((S, RET_DV), lambda b, h: (b, h)),
                   pl.BlockSpec((L, RET_DV), lambda b, h: (b, h))],
        out_shape=[jax.ShapeDtypeStruct((TL, 2 * D), BF16), jax.ShapeDtypeStruct((TC, 2 * D), BF16)],
        scratch_shapes=[
            pltpu.VMEM((RET_DK, RET_DV), F32), pltpu.VMEM((RET_DK, RET_DV), F32),
            pltpu.VMEM((S, RET_DV), F32), pltpu.VMEM((L, RET_DV), F32),
        ],
        args=(p, p, p, p, p, p, p, p, lg),
        side=side,
        name="retention",
    )


def _swa_kernel(q_ref, kp_ref, kc_ref, kn_ref, kx_ref, vp_ref, vc_ref, vn_ref, vx_ref, sink_ref, o_ref):
    n = pl.program_id(2)
    nb = pl.num_programs(2)
    k2 = jnp.concatenate([kp_ref[...], kc_ref[...], kn_ref[...], kx_ref[...]], axis=0)
    v2 = jnp.concatenate([vp_ref[...], vc_ref[...], vn_ref[...], vx_ref[...]], axis=0)
    nk = k2.shape[0]
    v2 = jnp.concatenate([v2, jnp.ones((nk, LANES), BF16)], axis=1)
    qi = lax.broadcasted_iota(jnp.int32, (QB, nk), 0)
    kj = lax.broadcasted_iota(jnp.int32, (QB, nk), 1)
    band = (kj >= qi) & (kj <= qi + 2 * WINDOW)
    band = band & ((kj >= WINDOW) | (n > 0)) & ((kj < WINDOW + QB) | (n < nb - 1))
    valid = band | (kj >= 2 * WINDOW + QB)
    bias = jnp.where(valid, 0.0, NEG_INF).astype(F32)
    bias = jnp.concatenate([bias] * SWA_G, axis=0)
    lane = lax.broadcasted_iota(jnp.int32, (QB, LANES), 1)
    lo = lane < SWA_DH
    qcols = [q_ref[:, c * LANES:(c + 1) * LANES].astype(F32) for c in range(4)]
    qswap = [pltpu.roll(x, SWA_DH, 1) for x in qcols]
    out_cols = [None] * 4
    heads_of = lambda t: [(2 * t + g // 2, g % 2) for g in range(SWA_G)]
    scores = []
    for t in range(2):
        keep = lo if t == 0 else ~lo
        qs = jnp.concatenate(
            [jnp.where(keep, qcols[c] if hf == t else qswap[c], 0.0).astype(BF16) for c, hf in heads_of(t)],
            axis=0)
        scores.append(lax.dot_general(qs, k2, (((1,), (1,)), ((), ())), preferred_element_type=F32) + bias)
    for t, s in enumerate(scores):
        heads = heads_of(t)
        sink = jnp.concatenate(
            [jnp.broadcast_to(sink_ref[2 * c + hf][:, 0:1] * LOG2E, (QB, 1)) for c, hf in heads], axis=0)
        mx = jnp.maximum(jnp.max(s, -1, keepdims=True), sink)
        e = jnp.exp2(s - mx).astype(BF16)
        r = jnp.dot(e, v2, preferred_element_type=F32)
        o = r[:, :LANES] / (r[:, LANES:] + jnp.exp2(sink - mx))
        for g, (c, hf) in enumerate(heads):
            og = o[g * QB:(g + 1) * QB, :]
            if hf != t:
                og = pltpu.roll(og, SWA_DH, 1)
            og = jnp.where(lo if hf == 0 else ~lo, og, 0.0)
            out_cols[c] = og if out_cols[c] is None else out_cols[c] + og
    for c in range(4):
        o_ref[:, c * LANES:(c + 1) * LANES] = out_cols[c].astype(BF16)


def _swa_attention(qkv, sink, side):
    nqb = S // QB
    kcol = SWA_H * SWA_DH // LANES
    vcol = kcol + SWA_KV * SWA_DH // LANES
    ctx0 = TL // L
    sink_t = jnp.broadcast_to(sink.astype(F32)[:, None, None], (SWA_H, 1, LANES))

    per = QB // WINDOW
    nhb = S // WINDOW

    def halo(col0, after):
        def index_map(b, p, n):
            blk = (n + 1) * per if after else n * per - 1
            return (b * nhb + jnp.clip(blk, 0, nhb - 1), col0 + p)
        return pl.BlockSpec((WINDOW, LANES), index_map)

    cur = lambda col0: pl.BlockSpec((QB, LANES), lambda b, p, n: (b * nqb + n, col0 + p))
    ctx = lambda col0: pl.BlockSpec((L, LANES), lambda b, p, n: (ctx0 + b, col0 + p))
    (o_lat,), wb = _call_with_side(
        _swa_kernel,
        grid=(B, SWA_KV // 2, nqb),
        in_specs=[
            pl.BlockSpec((QB, 4 * LANES), lambda b, p, n: (b * nqb + n, p)),
            halo(kcol, False), cur(kcol), halo(kcol, True), ctx(kcol),
            halo(vcol, False), cur(vcol), halo(vcol, True), ctx(vcol),
            pl.BlockSpec((8, 1, LANES), lambda b, p, n: (p, 0, 0)),
        ],
        out_specs=[pl.BlockSpec((QB, 4 * LANES), lambda b, p, n: (b * nqb + n, p))],
        out_shape=[jax.ShapeDtypeStruct((TL, D), BF16)],
        scratch_shapes=[],
        args=(qkv, qkv, qkv, qkv, qkv, qkv, qkv, qkv, qkv, sink_t),
        side=side,
        name="swa",
    )
    return (o_lat, None), wb


@jax.jit
def _forward(x, c, ctx, c_ctx, ada_w, ada_b, ln_g, ln_b, mlp_w1, mlp_w2,
             lru_w_in, lru_conv_w, lru_conv_b, lru_gate_w, lru_gate_b, lru_lambda, lru_w_out,
             dif_w_qkv, dif_lambda, dif_subln, dif_w_out, ret_w_qkvg, ret_w_out,
             swa_w_qkv, swa_sink, swa_w_out):
    cvec = jnp.concatenate([c, c_ctx[None, :], jnp.zeros((NSEG - B - 1, D), F32)], 0)
    mods = _ada(cvec, ada_w, ada_b)
    tables = _rope_tables(TM)
    h, u = _modulate(x.reshape(TL, D), ctx.reshape(TC, D), mods, 0)
    w_outs = [lru_w_out[0], dif_w_out[0], ret_w_out[0], swa_w_out[0]]
    w1 = mlp_w1.reshape(DEPTH * D, D_FF)
    w2 = mlp_w2.reshape(DEPTH * D_FF, D)
    qscale = SWA_DH ** -0.5 * LOG2E
    for i in range(DEPTH):
        last = i == DEPTH - 1
        n_rows = TL if last else T
        side = ((w1, i, D), (w2, i, D_FF), (w_outs[i], 0, w_outs[i].shape[0]))
        if i == 0:
            ui = jnp.concatenate([_interleave(u[:TL], S), _interleave(u[TL:], L)], 0)
            p = _proj(ui, lru_w_in[0], 1024)
            (o, oc), wb = _lru(p, lru_conv_w[0], lru_conv_b[0], lru_gate_w[0], lru_gate_b[0], lru_lambda[0], side)
            o, oc = _deinterleave(o, S), _deinterleave(oc, L)
        elif i == 1:
            p = _proj(u, dif_w_qkv[0], 1024, rope=(D, 2 * D, qscale, tables))
            (o, oc), wb = _dif_attention(p, dif_lambda[0], dif_subln[0], i, side)
        elif i == 2:
            p = _proj(u, ret_w_qkvg[0], 1024, qscale_cols=(D, RET_DK ** -0.5))
            (o, oc), wb = _retention(p, side)
        else:
            p = _proj(u, swa_w_qkv[0], 512, rope=(SWA_H * SWA_DH, (SWA_H + SWA_KV) * SWA_DH, qscale, tables))
            (o, oc), wb = _swa_attention(p, swa_sink[0], side)
        w1b, w2b, wob = wb
        h, u2 = _outln(o, oc, wob, h, mods, i, ln_g[i, 0:1], ln_b[i, 0:1])
        h, u = _mlp(u2, w1b, w2b, h, mods, i, ln_g[i, 1:2], ln_b[i, 1:2], n_rows)
    return h.reshape(B, S, D)


def kernel(x, c, ctx, c_ctx, ada_w, ada_b, ln_g, ln_b, mlp_w1, mlp_w2, lru_w_in, lru_conv_w, lru_conv_b,
           lru_gate_w, lru_gate_b, lru_lambda, lru_w_out, dif_w_qkv, dif_lambda, dif_subln, dif_w_out,
           ret_w_qkvg, ret_w_out, swa_w_qkv, swa_sink, swa_w_out):
    return _forward(x, c, ctx, c_ctx, ada_w, ada_b, ln_g, ln_b, mlp_w1, mlp_w2,
                    lru_w_in, lru_conv_w, lru_conv_b, lru_gate_w, lru_gate_b, lru_lambda, lru_w_out,
                    dif_w_qkv, dif_lambda, dif_subln, dif_w_out, ret_w_qkvg, ret_w_out,
                    swa_w_qkv, swa_sink, swa_w_out)
```

```python
import functools
import math

import jax
import jax.numpy as jnp
from jax import lax
from jax.experimental import pallas as pl
from jax.experimental.pallas import tpu as pltpu

F32 = jnp.float32
BF16 = jnp.bfloat16

D = 2048
B = 4
S = 2048
L = 256
DEPTH = 4
GRID_W = 64
TL = B * S
TC = B * L
T = TL + TC
NSEG = 8
ADA = 6
D_FF = 4 * D
ALPHA = (2 * DEPTH) ** 0.25
LN_EPS = 1e-5
ROPE_BASE = 10000.0
NEG_INF = -1e30

LRU_BW = 256
LRU_NB = D // LRU_BW
LRU_C = 8.0
DIF_H = 16
DIF_DV = 128
RET_H = 8
RET_DK = 256
RET_DV = 512
RET_CH = 512
SWA_H = 32
SWA_KV = 8
SWA_DH = 64
SWA_G = SWA_H // SWA_KV
WINDOW = 128
QB = 128

VMEM_LIMIT = 56 * 1024 * 1024
LANES = 128

TM = 1024
TM_LN = 512
TM_MLP = 1024
PROJ_SUB = 256
LN_SUB = 256
MLP_SUB = 1024
LOG2E = math.log2(math.e)


def _cp(*sem):
    return pltpu.CompilerParams(dimension_semantics=sem, vmem_limit_bytes=VMEM_LIMIT)


def _seg(i, tm):
    return jnp.minimum((i * tm) // S, B)


def _ada_kernel(c_ref, w_ref, b_ref, o_ref):
    c = c_ref[...]
    a = (c * jax.nn.sigmoid(c)).astype(BF16)
    o_ref[0] = jnp.dot(a, w_ref[0].astype(BF16), preferred_element_type=F32) + b_ref[0]


def _ada(cvec, ada_w, ada_b):
    tn = 1024
    out = pl.pallas_call(
        _ada_kernel,
        grid=(DEPTH, ADA * D // tn),
        in_specs=[
            pl.BlockSpec((NSEG, D), lambda l, j: (0, 0)),
            pl.BlockSpec((1, D, tn), lambda l, j: (l, 0, j)),
            pl.BlockSpec((1, 1, tn), lambda l, j: (l, 0, j)),
        ],
        out_specs=pl.BlockSpec((1, NSEG, tn), lambda l, j: (l, 0, j)),
        out_shape=jax.ShapeDtypeStruct((DEPTH, NSEG, ADA * D), F32),
        compiler_params=_cp("arbitrary", "arbitrary"),
        name="ada",
    )(cvec, ada_w, ada_b.reshape(DEPTH, 1, ADA * D))
    return out.reshape(DEPTH * NSEG * ADA, 1, D)


def _mod_spec(layer, chunk, tm, grid_pos):
    base = layer * NSEG * ADA + chunk

    def index_map(*g):
        return (base + _seg(g[grid_pos], tm) * ADA, 0, 0)

    return pl.BlockSpec((1, 1, D), index_map)


def _modulate_kernel(x_ref, ctx_ref, sc_ref, sh_ref, h_ref, u_ref, *, nlat):
    def emit(src_ref):
        v = src_ref[...]
        h_ref[...] = v
        u_ref[...] = (v * (1.0 + sc_ref[0]) + sh_ref[0]).astype(BF16)

    pl.when(pl.program_id(0) < nlat)(lambda: emit(x_ref))
    pl.when(pl.program_id(0) >= nlat)(lambda: emit(ctx_ref))


def _modulate(x, ctx, mods, layer):
    tm = TM_LN
    nlat = TL // tm
    return pl.pallas_call(
        functools.partial(_modulate_kernel, nlat=nlat),
        grid=(T // tm,),
        in_specs=[
            pl.BlockSpec((tm, D), lambda i: (jnp.minimum(i, nlat - 1), 0)),
            pl.BlockSpec((tm, D), lambda i: (jnp.maximum(i - nlat, 0), 0)),
            _mod_spec(layer, 1, tm, 0),
            _mod_spec(layer, 0, tm, 0),
        ],
        out_specs=[pl.BlockSpec((tm, D), lambda i: (i, 0))] * 2,
        out_shape=[jax.ShapeDtypeStruct((T, D), F32), jax.ShapeDtypeStruct((T, D), BF16)],
        compiler_params=_cp("arbitrary"),
        name="modulate",
    )(x, ctx, mods, mods)


SIDE_BLOCKS_MAX = 128


def _call_with_side(kernel_fn, *, grid, in_specs, out_specs, out_shape, scratch_shapes, args, side, name):
    n_in, n_out, n_side = len(in_specs), len(out_specs), len(side)
    strides = [math.prod(grid[k + 1:]) for k in range(len(grid))]
    n_blocks = SIDE_BLOCKS_MAX
    while n_blocks > math.prod(grid):
        n_blocks //= 2
    step_of = lambda g: sum(gk * st for gk, st in zip(g, strides))
    blk = lambda *g: jnp.minimum(step_of(g), n_blocks - 1)
    side_in, side_out, side_shape = [], [], []
    for stacked, layer, rows in side:
        cols = stacked.shape[1]
        br = rows // n_blocks
        side_in.append(pl.BlockSpec((br, cols), lambda *g, layer=layer: (layer * n_blocks + blk(*g), 0)))
        side_out.append(pl.BlockSpec((br, cols), lambda *g: (blk(*g), 0)))
        side_shape.append(jax.ShapeDtypeStruct((rows, cols), BF16))

    def wrapped(*refs):
        ins, refs = refs[:n_in], refs[n_in:]
        srcs, refs = refs[:n_side], refs[n_side:]
        outs, refs = refs[:n_out], refs[n_out:]
        dsts, scratch = refs[:n_side], refs[n_side:]

        @pl.when(step_of([pl.program_id(k) for k in range(len(grid))]) < n_blocks)
        def _():
            for src, dst in zip(srcs, dsts):
                dst[...] = src[...].astype(BF16)

        kernel_fn(*ins, *outs, *scratch)

    res = pl.pallas_call(
        wrapped,
        grid=grid,
        in_specs=list(in_specs) + side_in,
        out_specs=list(out_specs) + side_out,
        out_shape=list(out_shape) + side_shape,
        scratch_shapes=scratch_shapes,
        compiler_params=_cp(*["arbitrary"] * len(grid)),
        name=name,
    )(*args, *[s[0] for s in side])
    return res[:n_out], res[n_out:]


def _rope_tables(tm):
    rows = S // GRID_W
    row = jnp.repeat(jnp.arange(rows, dtype=F32), GRID_W)
    col = jnp.tile(jnp.arange(GRID_W, dtype=F32), rows)
    half = SWA_DH // 2
    inv_freq = ROPE_BASE ** (-jnp.arange(0, half, 2, dtype=F32) / half)
    ang_r = row[:, None] * inv_freq[None, :]
    ang_c = col[:, None] * inv_freq[None, :]
    ang = jnp.concatenate([ang_r, ang_r, ang_c, ang_c], -1)
    cos, sin = jnp.cos(ang), jnp.sin(ang)
    first = (jnp.arange(SWA_DH) % 32) < 16
    sin_up = jnp.where(first[None, :], -sin, 0.0)
    sin_dn = jnp.where(first[None, :], 0.0, sin)

    def lay(t, ident):
        t = jnp.tile(t, (1, LANES // SWA_DH))
        return jnp.concatenate([t, jnp.full((tm, LANES), ident, F32)], 0)

    return lay(cos, 1.0), lay(sin_up, 0.0), lay(sin_dn, 0.0)


def _proj_kernel(a_ref, w_ref, *refs, rope, nq, nqk, qscale):
    if rope:
        cos_ref, up_ref, dn_ref, o_ref, wb_ref = refs
    else:
        o_ref, wb_ref = refs
    j = pl.program_id(0)
    i = pl.program_id(1)

    @pl.when(i == 0)
    def _():
        wb_ref[...] = w_ref[...].astype(BF16)

    tm, tn = o_ref.shape
    subs = [pl.ds(r * PROJ_SUB, PROJ_SUB) for r in range(tm // PROJ_SUB)]
    scale = jnp.where(j < nq, qscale, 1.0).astype(F32)

    def plain(scaled):
        for rows in subs:
            acc = jnp.dot(a_ref[rows, :], wb_ref[...], preferred_element_type=F32)
            o_ref[rows, :] = (acc * scale if scaled else acc).astype(o_ref.dtype)

    if not rope:
        plain(nq > 0)
        return

    @pl.when(j < nqk)
    def _():
        for rows in subs:
            acc = jnp.dot(a_ref[rows, :], wb_ref[...], preferred_element_type=F32)
            cos = cos_ref[rows, :] * scale
            up = up_ref[rows, :] * scale
            dn = dn_ref[rows, :] * scale
            for c in range(tn // LANES):
                x = acc[:, c * LANES:(c + 1) * LANES]
                y = x * cos + pltpu.roll(x, LANES - 16, 1) * up + pltpu.roll(x, 16, 1) * dn
                o_ref[rows, c * LANES:(c + 1) * LANES] = y.astype(o_ref.dtype)

    pl.when(j >= nqk)(lambda: plain(False))


def _proj(a, w, tn, rope=None, qscale_cols=None):
    n = w.shape[1]
    tm = TM
    in_specs = [
        pl.BlockSpec((tm, D), lambda j, i: (i, 0)),
        pl.BlockSpec((D, tn), lambda j, i: (0, j)),
    ]
    args = [a, w]
    kw = dict(rope=False, nq=0, nqk=0, qscale=1.0)
    if qscale_cols is not None:
        kw = dict(rope=False, nq=qscale_cols[0] // tn, nqk=0, qscale=qscale_cols[1])
    if rope is not None:
        nq_cols, nqk_cols, qscale, tables = rope
        nlat = TL // tm
        per_seq = S // tm

        def tmap(j, i):
            return (jnp.where(i < nlat, i % per_seq, per_seq), 0)

        in_specs += [pl.BlockSpec((tm, LANES), tmap)] * 3
        args += list(tables)
        kw = dict(rope=True, nq=nq_cols // tn, nqk=nqk_cols // tn, qscale=qscale)
    return pl.pallas_call(
        functools.partial(_proj_kernel, **kw),
        grid=(n // tn, T // tm),
        in_specs=in_specs,
        out_specs=pl.BlockSpec((tm, tn), lambda j, i: (i, j)),
        out_shape=jax.ShapeDtypeStruct((T, n), BF16),
        scratch_shapes=[pltpu.VMEM((D, tn), BF16)],
        compiler_params=_cp("arbitrary", "arbitrary"),
        name="proj",
    )(*args)


def _ln_epilogue(z, lng, lnb):
    mu = jnp.mean(z, -1, keepdims=True)
    zc = z - mu
    var = jnp.mean(zc * zc, -1, keepdims=True)
    return zc * lax.rsqrt(var + LN_EPS) * lng + lnb


def _residual_ln(rows, y, h_ref, g_ref, lng_ref, lnb_ref, sc_ref, sh_ref, hout_ref, uout_ref):
    hn = _ln_epilogue(ALPHA * h_ref[rows, :] + g_ref[0] * y, lng_ref[...], lnb_ref[...])
    hout_ref[rows, :] = hn
    if uout_ref is not None:
        uout_ref[rows, :] = (hn * (1.0 + sc_ref[0]) + sh_ref[0]).astype(BF16)


def _outln_kernel(*refs, nlat):
    if nlat is None:
        ol_ref, oc_ref = refs[0], None
        refs = refs[1:]
    else:
        ol_ref, oc_ref = refs[:2]
        refs = refs[2:]
    w_ref, h_ref, g_ref, lng_ref, lnb_ref, sc_ref, sh_ref, hout_ref, uout_ref = refs

    def body(o_ref):
        for r in range(o_ref.shape[0] // LN_SUB):
            rows = pl.ds(r * LN_SUB, LN_SUB)
            y = jnp.dot(o_ref[rows, :], w_ref[...], preferred_element_type=F32)
            _residual_ln(rows, y, h_ref, g_ref, lng_ref, lnb_ref, sc_ref, sh_ref, hout_ref, uout_ref)

    if nlat is None:
        body(ol_ref)
    else:
        pl.when(pl.program_id(0) < nlat)(lambda: body(ol_ref))
        pl.when(pl.program_id(0) >= nlat)(lambda: body(oc_ref))


def _outln(o_lat, o_ctx, w_bf16, h, mods, layer, lng, lnb):
    kin = o_lat.shape[1]
    tm = TM_LN * D // kin
    const = lambda i: (0, 0)
    nlat = TL // tm
    if o_ctx is None:
        n_rows = TL
        o_specs = [pl.BlockSpec((tm, kin), lambda i: (i, 0))]
        o_args = [o_lat]
    else:
        n_rows = T
        o_specs = [pl.BlockSpec((tm, kin), lambda i: (jnp.minimum(i, nlat - 1), 0)),
                   pl.BlockSpec((tm, kin), lambda i: (jnp.maximum(i - nlat, 0), 0))]
        o_args = [o_lat, o_ctx]
    return pl.pallas_call(
        functools.partial(_outln_kernel, nlat=None if o_ctx is None else nlat),
        grid=(n_rows // tm,),
        in_specs=o_specs + [
            pl.BlockSpec((kin, D), const, pipeline_mode=pl.Buffered(1)),
            pl.BlockSpec((tm, D), lambda i: (i, 0)),
            _mod_spec(layer, 2, tm, 0),
            pl.BlockSpec((1, D), const),
            pl.BlockSpec((1, D), const),
            _mod_spec(layer, 4, tm, 0),
            _mod_spec(layer, 3, tm, 0),
        ],
        out_specs=[pl.BlockSpec((tm, D), lambda i: (i, 0)), pl.BlockSpec((tm, D), lambda i: (i, 0))],
        out_shape=[jax.ShapeDtypeStruct((n_rows, D), F32), jax.ShapeDtypeStruct((n_rows, D), BF16)],
        compiler_params=_cp("arbitrary"),
        name="outln",
    )(*o_args, w_bf16, h, mods, lng, lnb, mods, mods)


def _mlp_kernel(u_ref, w1_ref, w2_ref, h_ref, g_ref, lng_ref, lnb_ref, *refs, emit_u):
    if emit_u:
        sc_ref, sh_ref, hout_ref, uout_ref = refs
    else:
        (hout_ref,) = refs
        sc_ref = sh_ref = uout_ref = None
    j = pl.program_id(1)
    tm = u_ref.shape[0]

    subs = [pl.ds(r * LN_SUB, LN_SUB) for r in range(tm // LN_SUB)]

    def hidden():
        w1 = w1_ref[...]
        hids = []
        for rows in subs:
            hid = jnp.dot(u_ref[rows, :], w1, preferred_element_type=F32)
            hids.append(jnp.square(jnp.maximum(hid, 0.0)).astype(BF16))
        return hids

    last = pl.num_programs(1) - 1

    @pl.when(j == 0)
    def _():
        w2 = w2_ref[...]
        for rows, hid in zip(subs, hidden()):
            hout_ref[rows, :] = jnp.dot(hid, w2, preferred_element_type=F32)

    @pl.when((j > 0) & (j < last))
    def _():
        w2 = w2_ref[...]
        for rows, hid in zip(subs, hidden()):
            hout_ref[rows, :] += jnp.dot(hid, w2, preferred_element_type=F32)

    @pl.when(j == last)
    def _():
        w2 = w2_ref[...]
        for rows, hid in zip(subs, hidden()):
            acc = hout_ref[rows, :] + jnp.dot(hid, w2, preferred_element_type=F32)
            _residual_ln(rows, acc, h_ref, g_ref, lng_ref, lnb_ref, sc_ref, sh_ref, hout_ref, uout_ref)


def _mlp(u, w1, w2, h, mods, layer, lng, lnb, n_rows):
    tm = TM_MLP
    tf = 1024
    emit_u = layer < DEPTH - 1
    const = lambda i, j: (0, 0)
    row = lambda i, j: (i, 0)
    nff = D_FF // tf
    once = pl.Buffered(1)
    in_specs = [
        pl.BlockSpec((tm, D), row, pipeline_mode=once),
        pl.BlockSpec((D, tf), lambda i, j: (0, j)),
        pl.BlockSpec((tf, D), lambda i, j: (j, 0)),
        pl.BlockSpec((tm, D), row, pipeline_mode=once),
        _mod_spec(layer, 5, tm, 0),
        pl.BlockSpec((1, D), const),
        pl.BlockSpec((1, D), const),
    ]
    args = [u, w1, w2, h, mods, lng, lnb]
    out_specs = [pl.BlockSpec((tm, D), row, pipeline_mode=once)]
    out_shape = [jax.ShapeDtypeStruct((n_rows, D), F32)]
    if emit_u:
        in_specs += [_mod_spec(layer + 1, 1, tm, 0), _mod_spec(layer + 1, 0, tm, 0)]
        args += [mods, mods]
        out_specs.append(pl.BlockSpec((tm, D), row, pipeline_mode=once))
        out_shape.append(jax.ShapeDtypeStruct((n_rows, D), BF16))
    res = pl.pallas_call(
        functools.partial(_mlp_kernel, emit_u=emit_u),
        grid=(n_rows // tm, nff),
        in_specs=in_specs,
        out_specs=out_specs,
        out_shape=out_shape,
        compiler_params=_cp("arbitrary", "arbitrary"),
        name="mlp",
    )(*args)
    return (res[0], res[1]) if emit_u else (res[0], None)


NSUB = 8
LRU_CHUNK = 256
LRU_T = L + S
LRU_PAD = 3 * NSUB


def _interleave(t, n):
    nseq = t.shape[0] // n
    return t.reshape(nseq, NSUB, n // NSUB, t.shape[1]).transpose(0, 2, 1, 3).reshape(t.shape)


def _deinterleave(t, n):
    nseq = t.shape[0] // n
    return t.reshape(nseq, n // NSUB, NSUB, t.shape[1]).transpose(0, 2, 1, 3).reshape(t.shape)


def _prev_segment(tile):
    s = lax.broadcasted_iota(jnp.int32, tile.shape, 0)
    return jnp.where(s >= 1, pltpu.roll(tile, 1, 0), 0.0)


def _next_segment(tile):
    s = lax.broadcasted_iota(jnp.int32, tile.shape, 0)
    return jnp.where(s < NSUB - 1, pltpu.roll(tile, NSUB - 1, 0), 0.0)


def _lru_conv(r_ref, xp_ref, cw, cb):
    n = r_ref.shape[0]
    x = r_ref[...].astype(F32)
    xp_ref[0:NSUB, :] = _prev_segment(x[n - 2 * NSUB:n - NSUB, :])
    xp_ref[NSUB:2 * NSUB, :] = _prev_segment(x[n - NSUB:n, :])
    xp_ref[2 * NSUB:n + 2 * NSUB, :] = x
    xp_ref[n + 2 * NSUB:n + 3 * NSUB, :] = _next_segment(x[0:NSUB, :])
    y = cb + cw[2:3] * x
    y = y + cw[0:1] * xp_ref[0:n, :]
    y = y + cw[1:2] * xp_ref[NSUB:n + NSUB, :]
    y = y + cw[3:4] * xp_ref[3 * NSUB:n + 3 * NSUB, :]
    return y


def _lru_scan(af_ref, bf_ref, ab_ref, bb_ref, row0, n, init_f, init_b):
    k_steps = n // NSUB

    def body(k, carry):
        hf, pf, hb, pb = carry
        rf = pl.ds(pl.multiple_of(row0 + NSUB * k, NSUB), NSUB)
        rb = pl.ds(pl.multiple_of(row0 + NSUB * (k_steps - 1 - k), NSUB), NSUB)
        a = af_ref[rf, :]
        hf = a * hf + bf_ref[rf, :]
        pf = a * pf
        bf_ref[rf, :] = hf
        af_ref[rf, :] = pf
        a = ab_ref[rb, :]
        hb = a * hb + bb_ref[rb, :]
        pb = a * pb
        bb_ref[rb, :] = hb
        ab_ref[rb, :] = pb
        return hf, pf, hb, pb

    zero = jnp.zeros((NSUB, LRU_BW), F32)
    one = jnp.ones((NSUB, LRU_BW), F32)
    hf, pf, hb, pb = lax.fori_loop(0, k_steps, body, (zero, one, zero, one))
    h = init_f
    in_f = []
    for s in range(NSUB):
        in_f.append(h)
        h = hf[s:s + 1, :] + pf[s:s + 1, :] * h
    final_f = h
    h = init_b
    in_b = [None] * NSUB
    for s in reversed(range(NSUB)):
        in_b[s] = h
        h = hb[s:s + 1, :] + pb[s:s + 1, :] * h
    return jnp.concatenate(in_f, 0), jnp.concatenate(in_b, 0), final_f, h


def _lru_kernel(gl_ref, gc_ref, rl_ref, rc_ref, cw_ref, cb_ref, wg_ref, bg_ref, lam_ref,
                ol_ref, oc_ref, x_ref, xp_ref, af_ref, bf_ref, ab_ref, bb_ref):
    cw = cw_ref[...]
    cb = cb_ref[...]
    x_ref[0:L, :] = _lru_conv(rc_ref, xp_ref, cw, cb)
    x_ref[L:LRU_T, :] = _lru_conv(rl_ref, xp_ref, cw, cb)

    lam = lam_ref[...]
    decay = LRU_C * (jnp.maximum(-lam, 0.0) + jnp.log1p(jnp.exp(-jnp.abs(lam))))
    half_decay = 0.5 * decay
    half_decay_l2 = -LOG2E * half_decay
    wg = wg_ref[0]
    bg = bg_ref[...]

    for c in range(LRU_T // LRU_CHUNK):
        rows = pl.ds(c * LRU_CHUNK, LRU_CHUNK)
        xs = x_ref[rows, :]
        xh = 0.5 * xs
        z = jnp.dot(xs.astype(BF16), wg, preferred_element_type=F32)
        for d, (a_ref, b_ref) in enumerate(((af_ref, bf_ref), (ab_ref, bb_ref))):
            tr = jnp.tanh(z[:, (2 * d) * LRU_BW:(2 * d + 1) * LRU_BW] + bg[2 * d:2 * d + 1])
            tg = jnp.tanh(z[:, (2 * d + 1) * LRU_BW:(2 * d + 2) * LRU_BW] + bg[2 * d + 1:2 * d + 2])
            hl = half_decay_l2[d:d + 1]
            hd = half_decay[d:d + 1]
            a = jnp.exp2(tr * hl + hl)
            m = jnp.tanh(tr * hd + hd) * (a * a + 1.0)
            a_ref[rows, :] = a
            b_ref[rows, :] = jnp.where(m > 0.0, m * lax.rsqrt(m), 0.0) * (xh * tg + xh)

    zero = jnp.zeros((1, LRU_BW), F32)
    refs = (af_ref, bf_ref, ab_ref, bb_ref)
    cf_in, cb_in, cf_fin, cb_fin = _lru_scan(*refs, 0, L, zero, zero)
    lf_in, lb_in, _, _ = _lru_scan(*refs, L, S, cf_fin, cb_fin)

    def emit(gate_ref, o_ref, row0, in_f, in_b):
        reps = LRU_CHUNK // NSUB
        in_f = jnp.concatenate([in_f] * reps, 0)
        in_b = jnp.concatenate([in_b] * reps, 0)
        for c in range(gate_ref.shape[0] // LRU_CHUNK):
            src = pl.ds(row0 + c * LRU_CHUNK, LRU_CHUNK)
            dst = pl.ds(c * LRU_CHUNK, LRU_CHUNK)
            h = (bf_ref[src, :] + af_ref[src, :] * in_f) + (bb_ref[src, :] + ab_ref[src, :] * in_b)
            o_ref[dst, :] = (jax.nn.gelu(gate_ref[dst, :].astype(F32)) * h).astype(BF16)

    emit(gc_ref, oc_ref, 0, cf_in, cb_in)
    emit(gl_ref, ol_ref, L, lf_in, lb_in)


def _lru(p, conv_w, conv_b, gate_w, gate_b, lam, side):
    wg = (0.5 * jnp.transpose(gate_w, (2, 3, 0, 1, 4))).reshape(LRU_NB, LRU_BW, 4 * LRU_BW).astype(BF16)
    bg = 0.5 * gate_b.reshape(4, D)
    ctx0 = TL // L
    lat = pl.BlockSpec((S, LRU_BW), lambda b, n: (b, n))
    ctx = pl.BlockSpec((L, LRU_BW), lambda b, n: (ctx0 + b, n))
    return _call_with_side(
        _lru_kernel,
        grid=(B, LRU_NB),
        in_specs=[
            lat, ctx,
            pl.BlockSpec((S, LRU_BW), lambda b, n: (b, LRU_NB + n)),
            pl.BlockSpec((L, LRU_BW), lambda b, n: (ctx0 + b, LRU_NB + n)),
            pl.BlockSpec((4, LRU_BW), lambda b, n: (0, n)),
            pl.BlockSpec((1, LRU_BW), lambda b, n: (0, n)),
            pl.BlockSpec((1, LRU_BW, 4 * LRU_BW), lambda b, n: (n, 0, 0)),
            pl.BlockSpec((4, LRU_BW), lambda b, n: (0, n)),
            pl.BlockSpec((2, LRU_BW), lambda b, n: (0, n)),
        ],
        out_specs=[lat, pl.BlockSpec((L, LRU_BW), lambda b, n: (b, n))],
        out_shape=[jax.ShapeDtypeStruct((TL, D), BF16), jax.ShapeDtypeStruct((TC, D), BF16)],
        scratch_shapes=[pltpu.VMEM((LRU_T, LRU_BW), F32), pltpu.VMEM((S + LRU_PAD, LRU_BW), F32)]
        + [pltpu.VMEM((LRU_T, LRU_BW), F32)] * 4,
        args=(p, p, p, p, conv_w, conv_b.reshape(1, D), wg, bg, lam),
        side=side,
        name="lru",
    )


def _dif_kernel(*refs, n_kv, lam_init):
    q_ref = refs[0]
    kv = refs[1:1 + 2 * n_kv]
    lam_ref, sub_ref, o_ref, kcat_ref, vt_ref = refs[1 + 2 * n_kv:]

    @pl.when(pl.program_id(2) == 0)
    def _():
        off = 0
        for t in range(n_kv):
            n = kv[2 * t].shape[0]
            kcat_ref[off:off + n, :] = kv[2 * t][...]
            vt_ref[0:DIF_DV, off:off + n] = kv[2 * t + 1][...].astype(F32).T.astype(BF16)
            off += n
        vt_ref[DIF_DV:, :] = jnp.ones((vt_ref.shape[0] - DIF_DV, vt_ref.shape[1]), BF16)

    lam = lam_ref[...]
    lam_full = (jnp.exp(jnp.sum(lam[0:1] * lam[1:2], -1, keepdims=True))
                - jnp.exp(jnp.sum(lam[2:3] * lam[3:4], -1, keepdims=True)) + lam_init)
    q = q_ref[...]
    lane = lax.broadcasted_iota(jnp.int32, q.shape, 1)
    st = []
    for m in range(2):
        qm = jnp.where((lane >= 64 * m) & (lane < 64 * (m + 1)), q, jnp.zeros_like(q))
        st.append(lax.dot_general(kcat_ref[...], qm, (((1,), (1,)), ((), ())), preferred_element_type=F32))
    outs = []
    for s in st:
        e = jnp.exp2(s - jnp.max(s, 0, keepdims=True)).astype(BF16)
        r = jnp.dot(vt_ref[...], e, preferred_element_type=F32)
        outs.append(r[:DIF_DV, :] / r[DIF_DV:DIF_DV + 1, :])
    ot = outs[0] - lam_full * outs[1]
    yt = ot * lax.rsqrt(jnp.mean(ot * ot, 0, keepdims=True) + LN_EPS)
    o_ref[...] = (yt.T * sub_ref[...] * (1.0 - lam_init)).astype(BF16)


def _dif_attention(qkv, lam, subln, layer_idx, side):
    lam_init = 0.8 - 0.6 * math.exp(-0.3 * layer_idx)
    tq = 1024
    nh = DIF_H
    kern = functools.partial(_dif_kernel, lam_init=lam_init)
    small = [pl.BlockSpec((4, 64), lambda *g: (0, 0)), pl.BlockSpec((1, DIF_DV), lambda *g: (0, 0))]
    sub = subln.reshape(1, DIF_DV)
    ctx0 = TL // L
    sem = _cp("arbitrary", "arbitrary", "arbitrary")
    ones_rows = 16
    scratch = lambda nk: [pltpu.VMEM((nk, DIF_DV), BF16), pltpu.VMEM((DIF_DV + ones_rows, nk), BF16)]
    (o_lat,), wb = _call_with_side(
        functools.partial(kern, n_kv=2),
        grid=(B, nh, S // tq),
        in_specs=[
            pl.BlockSpec((tq, DIF_DV), lambda b, h, i: (b * (S // tq) + i, h)),
            pl.BlockSpec((S, DIF_DV), lambda b, h, i: (b, nh + h)),
            pl.BlockSpec((S, DIF_DV), lambda b, h, i: (b, 2 * nh + h)),
            pl.BlockSpec((L, DIF_DV), lambda b, h, i: (ctx0 + b, nh + h)),
            pl.BlockSpec((L, DIF_DV), lambda b, h, i: (ctx0 + b, 2 * nh + h)),
        ] + small,
        out_specs=[pl.BlockSpec((tq, DIF_DV), lambda b, h, i: (b * (S // tq) + i, h))],
        out_shape=[jax.ShapeDtypeStruct((TL, D), BF16)],
        scratch_shapes=scratch(S + L),
        args=(qkv, qkv, qkv, qkv, qkv, lam, sub),
        side=side,
        name="dif_lat",
    )
    o_ctx = pl.pallas_call(
        functools.partial(kern, n_kv=1),
        grid=(B, nh, 1),
        in_specs=[
            pl.BlockSpec((L, DIF_DV), lambda b, h, i: (ctx0 + b, h)),
            pl.BlockSpec((L, DIF_DV), lambda b, h, i: (ctx0 + b, nh + h)),
            pl.BlockSpec((L, DIF_DV), lambda b, h, i: (ctx0 + b, 2 * nh + h)),
        ] + small,
        out_specs=pl.BlockSpec((L, DIF_DV), lambda b, h, i: (b, h)),
        out_shape=jax.ShapeDtypeStruct((TC, D), BF16),
        scratch_shapes=scratch(L),
        compiler_params=sem,
        name="dif_ctx",
    )(qkv, qkv, qkv, lam, sub)
    return (o_lat, o_ctx), wb


def _ret_kernel(ql_ref, qc_ref, kl_ref, kc_ref, vl_ref, vc_ref, gl_ref, gc_ref, lg_ref,
                ol_ref, oc_ref, rf_ref, rb_ref, il_ref, ic_ref):
    lg = lg_ref[0]
    lgf = lg[0:1, 0:1]
    lgb = lg[1:2, 0:1]

    def tables(C):
        row = lax.broadcasted_iota(jnp.int32, (C, C), 0).astype(F32)
        col = lax.broadcasted_iota(jnp.int32, (C, C), 1).astype(F32)
        rel = row - col
        pos = lax.broadcasted_iota(jnp.int32, (C, RET_DK), 0).astype(F32)
        return dict(
            decay=jnp.exp(jnp.abs(rel) * jnp.where(rel >= 0, lgf, lgb)),
            qdec_f=jnp.exp((pos + 1.0) * lgf),
            kdec_f=jnp.exp((C - 1.0 - pos) * lgf),
            qdec_b=jnp.exp((C - pos) * lgb),
            kdec_b=jnp.exp(pos * lgb),
            cdec_f=jnp.exp(C * lgf),
            cdec_b=jnp.exp(C * lgb),
        )

    tdot = lambda a, b: lax.dot_general(a, b, (((0,), (0,)), ((), ())), preferred_element_type=F32)

    def bwd_chunk(t, q_ref, k_ref, v_ref, i_ref, rows):
        q = q_ref[rows, :].astype(F32)
        k = k_ref[rows, :].astype(F32)
        v = v_ref[rows, :]
        i_ref[rows, :] = jnp.dot((q * t["qdec_b"]).astype(BF16), rb_ref[...].astype(BF16),
                                 preferred_element_type=F32)
        rb_ref[...] = t["cdec_b"] * rb_ref[...] + tdot((k * t["kdec_b"]).astype(BF16), v)

    def fwd_chunk(t, q_ref, k_ref, v_ref, g_ref, i_ref, o_ref, rows):
        qb = q_ref[rows, :]
        kb = k_ref[rows, :]
        v = v_ref[rows, :]
        q = qb.astype(F32)
        k = kb.astype(F32)
        s = lax.dot_general(qb, kb, (((1,), (1,)), ((), ())), preferred_element_type=F32) * t["decay"]
        o = jnp.dot(s.astype(BF16), v, preferred_element_type=F32)
        o = o + jnp.dot((q * t["qdec_f"]).astype(BF16), rf_ref[...].astype(BF16), preferred_element_type=F32)
        o = o + i_ref[rows, :]
        rf_ref[...] = t["cdec_f"] * rf_ref[...] + tdot((k * t["kdec_f"]).astype(BF16), v)
        o = o * lax.rsqrt(jnp.mean(o * o, -1, keepdims=True) + LN_EPS)
        g = g_ref[rows, :].astype(F32)
        o_ref[rows, :] = (g * jax.nn.sigmoid(g) * o).astype(BF16)

    rf_ref[...] = jnp.zeros_like(rf_ref)
    rb_ref[...] = jnp.zeros_like(rb_ref)
    tc = tables(L)
    tl = tables(RET_CH)
    lat_chunks = [pl.ds(n * RET_CH, RET_CH) for n in range(S // RET_CH)]
    all_ctx = pl.ds(0, L)

    bwd_chunk(tc, qc_ref, kc_ref, vc_ref, ic_ref, all_ctx)
    for rows in reversed(lat_chunks):
        bwd_chunk(tl, ql_ref, kl_ref, vl_ref, il_ref, rows)

    fwd_chunk(tc, qc_ref, kc_ref, vc_ref, gc_ref, ic_ref, oc_ref, all_ctx)
    for rows in lat_chunks:
        fwd_chunk(tl, ql_ref, kl_ref, vl_ref, gl_ref, il_ref, ol_ref, rows)


def _retention(p, side):
    log_gf = jnp.log1p(-jnp.exp2(-5.0 - jnp.arange(RET_H, dtype=F32)))
    lg = jnp.zeros((RET_H, 8, LANES), F32)
    lg = lg.at[:, 0, :].set(log_gf[:, None]).at[:, 1, :].set(log_gf[::-1][:, None])
    ctx0 = TL // L
    nk = D // RET_DK
    nv = 2 * D // RET_DV
    lat = lambda w, off: pl.BlockSpec((S, w), lambda b, h: (b, off + h))
    ctx = lambda w, off: pl.BlockSpec((L, w), lambda b, h: (ctx0 + b, off + h))
    return _call_with_side(
        _ret_kernel,
        grid=(B, RET_H),
        in_specs=[
            lat(RET_DK, 0), ctx(RET_DK, 0),
            lat(RET_DK, nk), ctx(RET_DK, nk),
            lat(RET_DV, nv), ctx(RET_DV, nv),
            lat(RET_DV, nv + RET_H), ctx(RET_DV, nv + RET_H),
            pl.BlockSpec((1, 8, LANES), lambda b, h: (h, 0, 0)),
        ],
        out_specs=[pl.BlockSpec((S, RET_DV), lambda b, h: (b, h)),
                   pl.BlockSpec((L, RET_DV), lambda b, h: (b, h))],
        out_shape=[jax.ShapeDtypeStruct((TL, 2 * D), BF16), jax.ShapeDtypeStruct((TC, 2 * D), BF16)],
        scratch_shapes=[
            pltpu.VMEM((RET_DK, RET_DV), F32), pltpu.VMEM((RET_DK, RET_DV), F32),
            pltpu.VMEM((S, RET_DV), F32), pltpu.VMEM((L, RET_DV), F32),
        ],
        args=(p, p, p, p, p, p, p, p, lg),
        side=side,
        name="retention",
    )


def _swa_kernel(q_ref, kp_ref, kc_ref, kn_ref, kx_ref, vp_ref, vc_ref, vn_ref, vx_ref, sink_ref, o_ref):
    n = pl.program_id(2)
    nb = pl.num_programs(2)
    k2 = jnp.concatenate([kp_ref[...], kc_ref[...], kn_ref[...], kx_ref[...]], axis=0)
    v2 = jnp.concatenate([vp_ref[...], vc_ref[...], vn_ref[...], vx_ref[...]], axis=0)
    nk = k2.shape[0]
    v2 = jnp.concatenate([v2, jnp.ones((nk, LANES), BF16)], axis=1)
    qi = lax.broadcasted_iota(jnp.int32, (QB, nk), 0)
    kj = lax.broadcasted_iota(jnp.int32, (QB, nk), 1)
    band = (kj >= qi) & (kj <= qi + 2 * WINDOW)
    band = band & ((kj >= WINDOW) | (n > 0)) & ((kj < WINDOW + QB) | (n < nb - 1))
    valid = band | (kj >= 2 * WINDOW + QB)
    bias = jnp.where(valid, 0.0, NEG_INF).astype(F32)
    bias = jnp.concatenate([bias] * SWA_G, axis=0)
    lane = lax.broadcasted_iota(jnp.int32, (QB, LANES), 1)
    lo = lane < SWA_DH
    qcols = [q_ref[:, c * LANES:(c + 1) * LANES].astype(F32) for c in range(4)]
    qswap = [pltpu.roll(x, SWA_DH, 1) for x in qcols]
    out_cols = [None] * 4
    heads_of = lambda t: [(2 * t + g // 2, g % 2) for g in range(SWA_G)]
    scores = []
    for t in range(2):
        keep = lo if t == 0 else ~lo
        qs = jnp.concatenate(
            [jnp.where(keep, qcols[c] if hf == t else qswap[c], 0.0).astype(BF16) for c, hf in heads_of(t)],
            axis=0)
        scores.append(lax.dot_general(qs, k2, (((1,), (1,)), ((), ())), preferred_element_type=F32) + bias)
    for t, s in enumerate(scores):
        heads = heads_of(t)
        sink = jnp.concatenate(
            [jnp.broadcast_to(sink_ref[2 * c + hf][:, 0:1] * LOG2E, (QB, 1)) for c, hf in heads], axis=0)
        mx = jnp.maximum(jnp.max(s, -1, keepdims=True), sink)
        e = jnp.exp2(s - mx).astype(BF16)
        r = jnp.dot(e, v2, preferred_element_type=F32)
        o = r[:, :LANES] / (r[:, LANES:] + jnp.exp2(sink - mx))
        for g, (c, hf) in enumerate(heads):
            og = o[g * QB:(g + 1) * QB, :]
            if hf != t:
                og = pltpu.roll(og, SWA_DH, 1)
            og = jnp.where(lo if hf == 0 else ~lo, og, 0.0)
            out_cols[c] = og if out_cols[c] is None else out_cols[c] + og
    for c in range(4):
        o_ref[:, c * LANES:(c + 1) * LANES] = out_cols[c].astype(BF16)


def _swa_attention(qkv, sink, side):
    nqb = S // QB
    kcol = SWA_H * SWA_DH // LANES
    vcol = kcol + SWA_KV * SWA_DH // LANES
    ctx0 = TL // L
    sink_t = jnp.broadcast_to(sink.astype(F32)[:, None, None], (SWA_H, 1, LANES))

    per = QB // WINDOW
    nhb = S // WINDOW

    def halo(col0, after):
        def index_map(b, p, n):
            blk = (n + 1) * per if after else n * per - 1
            return (b * nhb + jnp.clip(blk, 0, nhb - 1), col0 + p)
        return pl.BlockSpec((WINDOW, LANES), index_map)

    cur = lambda col0: pl.BlockSpec((QB, LANES), lambda b, p, n: (b * nqb + n, col0 + p))
    ctx = lambda col0: pl.BlockSpec((L, LANES), lambda b, p, n: (ctx0 + b, col0 + p))
    (o_lat,), wb = _call_with_side(
        _swa_kernel,
        grid=(B, SWA_KV // 2, nqb),
        in_specs=[
            pl.BlockSpec((QB, 4 * LANES), lambda b, p, n: (b * nqb + n, p)),
            halo(kcol, False), cur(kcol), halo(kcol, True), ctx(kcol),
            halo(vcol, False), cur(vcol), halo(vcol, True), ctx(vcol),
            pl.BlockSpec((8, 1, LANES), lambda b, p, n: (p, 0, 0)),
        ],
        out_specs=[pl.BlockSpec((QB, 4 * LANES), lambda b, p, n: (b * nqb + n, p))],
        out_shape=[jax.ShapeDtypeStruct((TL, D), BF16)],
        scratch_shapes=[],
        args=(qkv, qkv, qkv, qkv, qkv, qkv, qkv, qkv, qkv, sink_t),
        side=side,
        name="swa",
    )
    return (o_lat, None), wb


@jax.jit
def _forward(x, c, ctx, c_ctx, ada_w, ada_b, ln_g, ln_b, mlp_w1, mlp_w2,
             lru_w_in, lru_conv_w, lru_conv_b, lru_gate_w, lru_gate_b, lru_lambda, lru_w_out,
             dif_w_qkv, dif_lambda, dif_subln, dif_w_out, ret_w_qkvg, ret_w_out,
             swa_w_qkv, swa_sink, swa_w_out):
    cvec = jnp.concatenate([c, c_ctx[None, :], jnp.zeros((NSEG - B - 1, D), F32)], 0)
    mods = _ada(cvec, ada_w, ada_b)
    tables = _rope_tables(TM)
    h, u = _modulate(x.reshape(TL, D), ctx.reshape(TC, D), mods, 0)
    w_outs = [lru_w_out[0], dif_w_out[0], ret_w_out[0], swa_w_out[0]]
    w1 = mlp_w1.reshape(DEPTH * D, D_FF)
    w2 = mlp_w2.reshape(DEPTH * D_FF, D)
    qscale = SWA_DH ** -0.5 * LOG2E
    for i in range(DEPTH):
        last = i == DEPTH - 1
        n_rows = TL if last else T
        side = ((w1, i, D), (w2, i, D_FF), (w_outs[i], 0, w_outs[i].shape[0]))
        if i == 0:
            ui = jnp.concatenate([_interleave(u[:TL], S), _interleave(u[TL:], L)], 0)
            p = _proj(ui, lru_w_in[0], 1024)
            (o, oc), wb = _lru(p, lru_conv_w[0], lru_conv_b[0], lru_gate_w[0], lru_gate_b[0], lru_lambda[0], side)
            o, oc = _deinterleave(o, S), _deinterleave(oc, L)
        elif i == 1:
            p = _proj(u, dif_w_qkv[0], 1024, rope=(D, 2 * D, qscale, tables))
            (o, oc), wb = _dif_attention(p, dif_lambda[0], dif_subln[0], i, side)
        elif i == 2:
            p = _proj(u, ret_w_qkvg[0], 1024, qscale_cols=(D, RET_DK ** -0.5))
            (o, oc), wb = _retention(p, side)
        else:
            p = _proj(u, swa_w_qkv[0], 512, rope=(SWA_H * SWA_DH, (SWA_H + SWA_KV) * SWA_DH, qscale, tables))
            (o, oc), wb = _swa_attention(p, swa_sink[0], side)
        w1b, w2b, wob = wb
        h, u2 = _outln(o, oc, wob, h, mods, i, ln_g[i, 0:1], ln_b[i, 0:1])
        h, u = _mlp(u2, w1b, w2b, h, mods, i, ln_g[i, 1:2], ln_b[i, 1:2], n_rows)
    return h.reshape(B, S, D)


def kernel(x, c, ctx, c_ctx, ada_w, ada_b, ln_g, ln_b, mlp_w1, mlp_w2, lru_w_in, lru_conv_w, lru_conv_b,
           lru_gate_w, lru_gate_b, lru_lambda, lru_w_out, dif_w_qkv, dif_lambda, dif_subln, dif_w_out,
           ret_w_qkvg, ret_w_out, swa_w_qkv, swa_sink, swa_w_out):
    return _forward(x, c, ctx, c_ctx, ada_w, ada_b, ln_g, ln_b, mlp_w1, mlp_w2,
                    lru_w_in, lru_conv_w, lru_conv_b, lru_gate_w, lru_gate_b, lru_lambda, lru_w_out,
                    dif_w_qkv, dif_lambda, dif_subln, dif_w_out, ret_w_qkvg, ret_w_out,
                    swa_w_qkv, swa_sink, swa_w_out)
```

```python
import functools
import math

import jax
import jax.numpy as jnp
from jax import lax
from jax.experimental import pallas as pl
from jax.experimental.pallas import tpu as pltpu

F32 = jnp.float32
BF16 = jnp.bfloat16

D = 2048
B = 4
S = 2048
L = 256
DEPTH = 4
GRID_W = 64
TL = B * S
TC = B * L
T = TL + TC
NSEG = 8
ADA = 6
D_FF = 4 * D
ALPHA = (2 * DEPTH) ** 0.25
LN_EPS = 1e-5
ROPE_BASE = 10000.0
NEG_INF = -1e30

LRU_BW = 256
LRU_NB = D // LRU_BW
LRU_C = 8.0
DIF_H = 16
DIF_DV = 128
RET_H = 8
RET_DK = 256
RET_DV = 512
RET_CH = 512
SWA_H = 32
SWA_KV = 8
SWA_DH = 64
SWA_G = SWA_H // SWA_KV
WINDOW = 128
QB = 128

VMEM_LIMIT = 56 * 1024 * 1024
LANES = 128

TM = 1024
TM_LN = 512
TM_MLP = 1024
PROJ_SUB = 128
LN_SUB = 256
MLP_SUB = 1024
LOG2E = math.log2(math.e)


def _cp(*sem):
    return pltpu.CompilerParams(dimension_semantics=sem, vmem_limit_bytes=VMEM_LIMIT)


def _seg(i, tm):
    return jnp.minimum((i * tm) // S, B)


def _ada_kernel(c_ref, w_ref, b_ref, o_ref):
    c = c_ref[...]
    a = (c * jax.nn.sigmoid(c)).astype(BF16)
    o_ref[0] = jnp.dot(a, w_ref[0].astype(BF16), preferred_element_type=F32) + b_ref[0]


def _ada(cvec, ada_w, ada_b):
    tn = 1024
    out = pl.pallas_call(
        _ada_kernel,
        grid=(DEPTH, ADA * D // tn),
        in_specs=[
            pl.BlockSpec((NSEG, D), lambda l, j: (0, 0)),
            pl.BlockSpec((1, D, tn), lambda l, j: (l, 0, j)),
            pl.BlockSpec((1, 1, tn), lambda l, j: (l, 0, j)),
        ],
        out_specs=pl.BlockSpec((1, NSEG, tn), lambda l, j: (l, 0, j)),
        out_shape=jax.ShapeDtypeStruct((DEPTH, NSEG, ADA * D), F32),
        compiler_params=_cp("arbitrary", "arbitrary"),
        name="ada",
    )(cvec, ada_w, ada_b.reshape(DEPTH, 1, ADA * D))
    return out.reshape(DEPTH * NSEG * ADA, 1, D)


def _mod_spec(layer, chunk, tm, grid_pos):
    base = layer * NSEG * ADA + chunk

    def index_map(*g):
        return (base + _seg(g[grid_pos], tm) * ADA, 0, 0)

    return pl.BlockSpec((1, 1, D), index_map)


def _modulate_kernel(x_ref, ctx_ref, sc_ref, sh_ref, h_ref, u_ref, *, nlat):
    def emit(src_ref):
        v = src_ref[...]
        h_ref[...] = v
        u_ref[...] = (v * (1.0 + sc_ref[0]) + sh_ref[0]).astype(BF16)

    pl.when(pl.program_id(0) < nlat)(lambda: emit(x_ref))
    pl.when(pl.program_id(0) >= nlat)(lambda: emit(ctx_ref))


def _modulate(x, ctx, mods, layer):
    tm = TM_LN
    nlat = TL // tm
    return pl.pallas_call(
        functools.partial(_modulate_kernel, nlat=nlat),
        grid=(T // tm,),
        in_specs=[
            pl.BlockSpec((tm, D), lambda i: (jnp.minimum(i, nlat - 1), 0)),
            pl.BlockSpec((tm, D), lambda i: (jnp.maximum(i - nlat, 0), 0)),
            _mod_spec(layer, 1, tm, 0),
            _mod_spec(layer, 0, tm, 0),
        ],
        out_specs=[pl.BlockSpec((tm, D), lambda i: (i, 0))] * 2,
        out_shape=[jax.ShapeDtypeStruct((T, D), F32), jax.ShapeDtypeStruct((T, D), BF16)],
        compiler_params=_cp("arbitrary"),
        name="modulate",
    )(x, ctx, mods, mods)


SIDE_BLOCKS_MAX = 128


def _call_with_side(kernel_fn, *, grid, in_specs, out_specs, out_shape, scratch_shapes, args, side, name):
    n_in, n_out, n_side = len(in_specs), len(out_specs), len(side)
    strides = [math.prod(grid[k + 1:]) for k in range(len(grid))]
    n_blocks = SIDE_BLOCKS_MAX
    while n_blocks > math.prod(grid):
        n_blocks //= 2
    step_of = lambda g: sum(gk * st for gk, st in zip(g, strides))
    blk = lambda *g: jnp.minimum(step_of(g), n_blocks - 1)
    side_in, side_out, side_shape = [], [], []
    for stacked, layer, rows in side:
        cols = stacked.shape[1]
        br = rows // n_blocks
        side_in.append(pl.BlockSpec((br, cols), lambda *g, layer=layer: (layer * n_blocks + blk(*g), 0)))
        side_out.append(pl.BlockSpec((br, cols), lambda *g: (blk(*g), 0)))
        side_shape.append(jax.ShapeDtypeStruct((rows, cols), BF16))

    def wrapped(*refs):
        ins, refs = refs[:n_in], refs[n_in:]
        srcs, refs = refs[:n_side], refs[n_side:]
        outs, refs = refs[:n_out], refs[n_out:]
        dsts, scratch = refs[:n_side], refs[n_side:]

        @pl.when(step_of([pl.program_id(k) for k in range(len(grid))]) < n_blocks)
        def _():
            for src, dst in zip(srcs, dsts):
                dst[...] = src[...].astype(BF16)

        kernel_fn(*ins, *outs, *scratch)

    res = pl.pallas_call(
        wrapped,
        grid=grid,
        in_specs=list(in_specs) + side_in,
        out_specs=list(out_specs) + side_out,
        out_shape=list(out_shape) + side_shape,
        scratch_shapes=scratch_shapes,
        compiler_params=_cp(*["arbitrary"] * len(grid)),
        name=name,
    )(*args, *[s[0] for s in side])
    return res[:n_out], res[n_out:]


def _rope_tables(tm):
    rows = S // GRID_W
    row = jnp.repeat(jnp.arange(rows, dtype=F32), GRID_W)
    col = jnp.tile(jnp.arange(GRID_W, dtype=F32), rows)
    half = SWA_DH // 2
    inv_freq = ROPE_BASE ** (-jnp.arange(0, half, 2, dtype=F32) / half)
    ang_r = row[:, None] * inv_freq[None, :]
    ang_c = col[:, None] * inv_freq[None, :]
    ang = jnp.concatenate([ang_r, ang_r, ang_c, ang_c], -1)
    cos, sin = jnp.cos(ang), jnp.sin(ang)
    first = (jnp.arange(SWA_DH) % 32) < 16
    sin_up = jnp.where(first[None, :], -sin, 0.0)
    sin_dn = jnp.where(first[None, :], 0.0, sin)

    def lay(t, ident):
        t = jnp.tile(t, (1, LANES // SWA_DH))
        return jnp.concatenate([t, jnp.full((tm, LANES), ident, F32)], 0)

    return lay(cos, 1.0), lay(sin_up, 0.0), lay(sin_dn, 0.0)


def _proj_kernel(a_ref, w_ref, *refs, rope, nq, nqk, qscale):
    if rope:
        cos_ref, up_ref, dn_ref, o_ref, wb_ref = refs
    else:
        o_ref, wb_ref = refs
    j = pl.program_id(0)
    i = pl.program_id(1)

    @pl.when(i == 0)
    def _():
        wb_ref[...] = w_ref[...].astype(BF16)

    tm, tn = o_ref.shape
    subs = [pl.ds(r * PROJ_SUB, PROJ_SUB) for r in range(tm // PROJ_SUB)]
    scale = jnp.where(j < nq, qscale, 1.0).astype(F32)

    def plain(scaled):
        for rows in subs:
            acc = jnp.dot(a_ref[rows, :], wb_ref[...], preferred_element_type=F32)
            o_ref[rows, :] = (acc * scale if scaled else acc).astype(o_ref.dtype)

    if not rope:
        plain(nq > 0)
        return

    @pl.when(j < nqk)
    def _():
        for rows in subs:
            acc = jnp.dot(a_ref[rows, :], wb_ref[...], preferred_element_type=F32)
            cos = cos_ref[rows, :] * scale
            up = up_ref[rows, :] * scale
            dn = dn_ref[rows, :] * scale
            for c in range(tn // LANES):
                x = acc[:, c * LANES:(c + 1) * LANES]
                y = x * cos + pltpu.roll(x, LANES - 16, 1) * up + pltpu.roll(x, 16, 1) * dn
                o_ref[rows, c * LANES:(c + 1) * LANES] = y.astype(o_ref.dtype)

    pl.when(j >= nqk)(lambda: plain(False))


def _proj(a, w, tn, rope=None, qscale_cols=None):
    n = w.shape[1]
    tm = TM
    in_specs = [
        pl.BlockSpec((tm, D), lambda j, i: (i, 0)),
        pl.BlockSpec((D, tn), lambda j, i: (0, j)),
    ]
    args = [a, w]
    kw = dict(rope=False, nq=0, nqk=0, qscale=1.0)
    if qscale_cols is not None:
        kw = dict(rope=False, nq=qscale_cols[0] // tn, nqk=0, qscale=qscale_cols[1])
    if rope is not None:
        nq_cols, nqk_cols, qscale, tables = rope
        nlat = TL // tm
        per_seq = S // tm

        def tmap(j, i):
            return (jnp.where(i < nlat, i % per_seq, per_seq), 0)

        in_specs += [pl.BlockSpec((tm, LANES), tmap)] * 3
        args += list(tables)
        kw = dict(rope=True, nq=nq_cols // tn, nqk=nqk_cols // tn, qscale=qscale)
    return pl.pallas_call(
        functools.partial(_proj_kernel, **kw),
        grid=(n // tn, T // tm),
        in_specs=in_specs,
        out_specs=pl.BlockSpec((tm, tn), lambda j, i: (i, j)),
        out_shape=jax.ShapeDtypeStruct((T, n), BF16),
        scratch_shapes=[pltpu.VMEM((D, tn), BF16)],
        compiler_params=_cp("arbitrary", "arbitrary"),
        name="proj",
    )(*args)


def _ln_epilogue(z, lng, lnb):
    mu = jnp.mean(z, -1, keepdims=True)
    zc = z - mu
    var = jnp.mean(zc * zc, -1, keepdims=True)
    return zc * lax.rsqrt(var + LN_EPS) * lng + lnb


def _residual_ln(rows, y, h_ref, g_ref, lng_ref, lnb_ref, sc_ref, sh_ref, hout_ref, uout_ref):
    hn = _ln_epilogue(ALPHA * h_ref[rows, :] + g_ref[0] * y, lng_ref[...], lnb_ref[...])
    hout_ref[rows, :] = hn
    if uout_ref is not None:
        uout_ref[rows, :] = (hn * (1.0 + sc_ref[0]) + sh_ref[0]).astype(BF16)


def _outln_kernel(*refs, nlat):
    if nlat is None:
        ol_ref, oc_ref = refs[0], None
        refs = refs[1:]
    else:
        ol_ref, oc_ref = refs[:2]
        refs = refs[2:]
    w_ref, h_ref, g_ref, lng_ref, lnb_ref, sc_ref, sh_ref, hout_ref, uout_ref = refs

    def body(o_ref):
        for r in range(o_ref.shape[0] // LN_SUB):
            rows = pl.ds(r * LN_SUB, LN_SUB)
            y = jnp.dot(o_ref[rows, :], w_ref[...], preferred_element_type=F32)
            _residual_ln(rows, y, h_ref, g_ref, lng_ref, lnb_ref, sc_ref, sh_ref, hout_ref, uout_ref)

    if nlat is None:
        body(ol_ref)
    else:
        pl.when(pl.program_id(0) < nlat)(lambda: body(ol_ref))
        pl.when(pl.program_id(0) >= nlat)(lambda: body(oc_ref))


def _outln(o_lat, o_ctx, w_bf16, h, mods, layer, lng, lnb):
    kin = o_lat.shape[1]
    tm = TM_LN * D // kin
    const = lambda i: (0, 0)
    nlat = TL // tm
    if o_ctx is None:
        n_rows = TL
        o_specs = [pl.BlockSpec((tm, kin), lambda i: (i, 0))]
        o_args = [o_lat]
    else:
        n_rows = T
        o_specs = [pl.BlockSpec((tm, kin), lambda i: (jnp.minimum(i, nlat - 1), 0)),
                   pl.BlockSpec((tm, kin), lambda i: (jnp.maximum(i - nlat, 0), 0))]
        o_args = [o_lat, o_ctx]
    return pl.pallas_call(
        functools.partial(_outln_kernel, nlat=None if o_ctx is None else nlat),
        grid=(n_rows // tm,),
        in_specs=o_specs + [
            pl.BlockSpec((kin, D), const, pipeline_mode=pl.Buffered(1)),
            pl.BlockSpec((tm, D), lambda i: (i, 0)),
            _mod_spec(layer, 2, tm, 0),
            pl.BlockSpec((1, D), const),
            pl.BlockSpec((1, D), const),
            _mod_spec(layer, 4, tm, 0),
            _mod_spec(layer, 3, tm, 0),
        ],
        out_specs=[pl.BlockSpec((tm, D), lambda i: (i, 0)), pl.BlockSpec((tm, D), lambda i: (i, 0))],
        out_shape=[jax.ShapeDtypeStruct((n_rows, D), F32), jax.ShapeDtypeStruct((n_rows, D), BF16)],
        compiler_params=_cp("arbitrary"),
        name="outln",
    )(*o_args, w_bf16, h, mods, lng, lnb, mods, mods)


def _mlp_kernel(u_ref, w1_ref, w2_ref, h_ref, g_ref, lng_ref, lnb_ref, *refs, emit_u):
    if emit_u:
        sc_ref, sh_ref, hout_ref, uout_ref = refs
    else:
        (hout_ref,) = refs
        sc_ref = sh_ref = uout_ref = None
    j = pl.program_id(1)
    tm = u_ref.shape[0]

    subs = [pl.ds(r * LN_SUB, LN_SUB) for r in range(tm // LN_SUB)]

    def hidden():
        w1 = w1_ref[...]
        hids = []
        for rows in subs:
            hid = jnp.dot(u_ref[rows, :], w1, preferred_element_type=F32)
            hids.append(jnp.square(jnp.maximum(hid, 0.0)).astype(BF16))
        return hids

    last = pl.num_programs(1) - 1

    @pl.when(j == 0)
    def _():
        w2 = w2_ref[...]
        for rows, hid in zip(subs, hidden()):
            hout_ref[rows, :] = jnp.dot(hid, w2, preferred_element_type=F32)

    @pl.when((j > 0) & (j < last))
    def _():
        w2 = w2_ref[...]
        for rows, hid in zip(subs, hidden()):
            hout_ref[rows, :] += jnp.dot(hid, w2, preferred_element_type=F32)

    @pl.when(j == last)
    def _():
        w2 = w2_ref[...]
        for rows, hid in zip(subs, hidden()):
            acc = hout_ref[rows, :] + jnp.dot(hid, w2, preferred_element_type=F32)
            _residual_ln(rows, acc, h_ref, g_ref, lng_ref, lnb_ref, sc_ref, sh_ref, hout_ref, uout_ref)


def _mlp(u, w1, w2, h, mods, layer, lng, lnb, n_rows):
    tm = TM_MLP
    tf = 1024
    emit_u = layer < DEPTH - 1
    const = lambda i, j: (0, 0)
    row = lambda i, j: (i, 0)
    nff = D_FF // tf
    once = pl.Buffered(1)
    in_specs = [
        pl.BlockSpec((tm, D), row, pipeline_mode=once),
        pl.BlockSpec((D, tf), lambda i, j: (0, j)),
        pl.BlockSpec((tf, D), lambda i, j: (j, 0)),
        pl.BlockSpec((tm, D), row, pipeline_mode=once),
        _mod_spec(layer, 5, tm, 0),
        pl.BlockSpec((1, D), const),
        pl.BlockSpec((1, D), const),
    ]
    args = [u, w1, w2, h, mods, lng, lnb]
    out_specs = [pl.BlockSpec((tm, D), row, pipeline_mode=once)]
    out_shape = [jax.ShapeDtypeStruct((n_rows, D), F32)]
    if emit_u:
        in_specs += [_mod_spec(layer + 1, 1, tm, 0), _mod_spec(layer + 1, 0, tm, 0)]
        args += [mods, mods]
        out_specs.append(pl.BlockSpec((tm, D), row, pipeline_mode=once))
        out_shape.append(jax.ShapeDtypeStruct((n_rows, D), BF16))
    res = pl.pallas_call(
        functools.partial(_mlp_kernel, emit_u=emit_u),
        grid=(n_rows // tm, nff),
        in_specs=in_specs,
        out_specs=out_specs,
        out_shape=out_shape,
        compiler_params=_cp("arbitrary", "arbitrary"),
        name="mlp",
    )(*args)
    return (res[0], res[1]) if emit_u else (res[0], None)


NSUB = 8
LRU_CHUNK = 256
LRU_T = L + S
LRU_PAD = 3 * NSUB


def _interleave(t, n):
    nseq = t.shape[0] // n
    return t.reshape(nseq, NSUB, n // NSUB, t.shape[1]).transpose(0, 2, 1, 3).reshape(t.shape)


def _deinterleave(t, n):
    nseq = t.shape[0] // n
    return t.reshape(nseq, n // NSUB, NSUB, t.shape[1]).transpose(0, 2, 1, 3).reshape(t.shape)


def _prev_segment(tile):
    s = lax.broadcasted_iota(jnp.int32, tile.shape, 0)
    return jnp.where(s >= 1, pltpu.roll(tile, 1, 0), 0.0)


def _next_segment(tile):
    s = lax.broadcasted_iota(jnp.int32, tile.shape, 0)
    return jnp.where(s < NSUB - 1, pltpu.roll(tile, NSUB - 1, 0), 0.0)


def _lru_conv(r_ref, xp_ref, cw, cb):
    n = r_ref.shape[0]
    x = r_ref[...].astype(F32)
    xp_ref[0:NSUB, :] = _prev_segment(x[n - 2 * NSUB:n - NSUB, :])
    xp_ref[NSUB:2 * NSUB, :] = _prev_segment(x[n - NSUB:n, :])
    xp_ref[2 * NSUB:n + 2 * NSUB, :] = x
    xp_ref[n + 2 * NSUB:n + 3 * NSUB, :] = _next_segment(x[0:NSUB, :])
    y = cb + cw[2:3] * x
    y = y + cw[0:1] * xp_ref[0:n, :]
    y = y + cw[1:2] * xp_ref[NSUB:n + NSUB, :]
    y = y + cw[3:4] * xp_ref[3 * NSUB:n + 3 * NSUB, :]
    return y


def _lru_scan(af_ref, bf_ref, ab_ref, bb_ref, row0, n, init_f, init_b):
    k_steps = n // NSUB

    def body(k, carry):
        hf, pf, hb, pb = carry
        rf = pl.ds(pl.multiple_of(row0 + NSUB * k, NSUB), NSUB)
        rb = pl.ds(pl.multiple_of(row0 + NSUB * (k_steps - 1 - k), NSUB), NSUB)
        a = af_ref[rf, :]
        hf = a * hf + bf_ref[rf, :]
        pf = a * pf
        bf_ref[rf, :] = hf
        af_ref[rf, :] = pf
        a = ab_ref[rb, :]
        hb = a * hb + bb_ref[rb, :]
        pb = a * pb
        bb_ref[rb, :] = hb
        ab_ref[rb, :] = pb
        return hf, pf, hb, pb

    zero = jnp.zeros((NSUB, LRU_BW), F32)
    one = jnp.ones((NSUB, LRU_BW), F32)
    hf, pf, hb, pb = lax.fori_loop(0, k_steps, body, (zero, one, zero, one))
    h = init_f
    in_f = []
    for s in range(NSUB):
        in_f.append(h)
        h = hf[s:s + 1, :] + pf[s:s + 1, :] * h
    final_f = h
    h = init_b
    in_b = [None] * NSUB
    for s in reversed(range(NSUB)):
        in_b[s] = h
        h = hb[s:s + 1, :] + pb[s:s + 1, :] * h
    return jnp.concatenate(in_f, 0), jnp.concatenate(in_b, 0), final_f, h


def _lru_kernel(gl_ref, gc_ref, rl_ref, rc_ref, cw_ref, cb_ref, wg_ref, bg_ref, lam_ref,
                ol_ref, oc_ref, x_ref, xp_ref, af_ref, bf_ref, ab_ref, bb_ref):
    cw = cw_ref[...]
    cb = cb_ref[...]
    x_ref[0:L, :] = _lru_conv(rc_ref, xp_ref, cw, cb)
    x_ref[L:LRU_T, :] = _lru_conv(rl_ref, xp_ref, cw, cb)

    lam = lam_ref[...]
    decay = LRU_C * (jnp.maximum(-lam, 0.0) + jnp.log1p(jnp.exp(-jnp.abs(lam))))
    half_decay = 0.5 * decay
    half_decay_l2 = -LOG2E * half_decay
    wg = wg_ref[0]
    bg = bg_ref[...]

    for c in range(LRU_T // LRU_CHUNK):
        rows = pl.ds(c * LRU_CHUNK, LRU_CHUNK)
        xs = x_ref[rows, :]
        xh = 0.5 * xs
        z = jnp.dot(xs.astype(BF16), wg, preferred_element_type=F32)
        for d, (a_ref, b_ref) in enumerate(((af_ref, bf_ref), (ab_ref, bb_ref))):
            tr = jnp.tanh(z[:, (2 * d) * LRU_BW:(2 * d + 1) * LRU_BW] + bg[2 * d:2 * d + 1])
            tg = jnp.tanh(z[:, (2 * d + 1) * LRU_BW:(2 * d + 2) * LRU_BW] + bg[2 * d + 1:2 * d + 2])
            hl = half_decay_l2[d:d + 1]
            hd = half_decay[d:d + 1]
            a = jnp.exp2(tr * hl + hl)
            m = jnp.tanh(tr * hd + hd) * (a * a + 1.0)
            a_ref[rows, :] = a
            b_ref[rows, :] = jnp.where(m > 0.0, m * lax.rsqrt(m), 0.0) * (xh * tg + xh)

    zero = jnp.zeros((1, LRU_BW), F32)
    refs = (af_ref, bf_ref, ab_ref, bb_ref)
    cf_in, cb_in, cf_fin, cb_fin = _lru_scan(*refs, 0, L, zero, zero)
    lf_in, lb_in, _, _ = _lru_scan(*refs, L, S, cf_fin, cb_fin)

    def emit(gate_ref, o_ref, row0, in_f, in_b):
        reps = LRU_CHUNK // NSUB
        in_f = jnp.concatenate([in_f] * reps, 0)
        in_b = jnp.concatenate([in_b] * reps, 0)
        for c in range(gate_ref.shape[0] // LRU_CHUNK):
            src = pl.ds(row0 + c * LRU_CHUNK, LRU_CHUNK)
            dst = pl.ds(c * LRU_CHUNK, LRU_CHUNK)
            h = (bf_ref[src, :] + af_ref[src, :] * in_f) + (bb_ref[src, :] + ab_ref[src, :] * in_b)
            o_ref[dst, :] = (jax.nn.gelu(gate_ref[dst, :].astype(F32)) * h).astype(BF16)

    emit(gc_ref, oc_ref, 0, cf_in, cb_in)
    emit(gl_ref, ol_ref, L, lf_in, lb_in)


def _lru(p, conv_w, conv_b, gate_w, gate_b, lam, side):
    wg = (0.5 * jnp.transpose(gate_w, (2, 3, 0, 1, 4))).reshape(LRU_NB, LRU_BW, 4 * LRU_BW).astype(BF16)
    bg = 0.5 * gate_b.reshape(4, D)
    ctx0 = TL // L
    lat = pl.BlockSpec((S, LRU_BW), lambda b, n: (b, n))
    ctx = pl.BlockSpec((L, LRU_BW), lambda b, n: (ctx0 + b, n))
    return _call_with_side(
        _lru_kernel,
        grid=(B, LRU_NB),
        in_specs=[
            lat, ctx,
            pl.BlockSpec((S, LRU_BW), lambda b, n: (b, LRU_NB + n)),
            pl.BlockSpec((L, LRU_BW), lambda b, n: (ctx0 + b, LRU_NB + n)),
            pl.BlockSpec((4, LRU_BW), lambda b, n: (0, n)),
            pl.BlockSpec((1, LRU_BW), lambda b, n: (0, n)),
            pl.BlockSpec((1, LRU_BW, 4 * LRU_BW), lambda b, n: (n, 0, 0)),
            pl.BlockSpec((4, LRU_BW), lambda b, n: (0, n)),
            pl.BlockSpec((2, LRU_BW), lambda b, n: (0, n)),
        ],
        out_specs=[lat, pl.BlockSpec((L, LRU_BW), lambda b, n: (b, n))],
        out_shape=[jax.ShapeDtypeStruct((TL, D), BF16), jax.ShapeDtypeStruct((TC, D), BF16)],
        scratch_shapes=[pltpu.VMEM((LRU_T, LRU_BW), F32), pltpu.VMEM((S + LRU_PAD, LRU_BW), F32)]
        + [pltpu.VMEM((LRU_T, LRU_BW), F32)] * 4,
        args=(p, p, p, p, conv_w, conv_b.reshape(1, D), wg, bg, lam),
        side=side,
        name="lru",
    )


def _dif_kernel(*refs, n_kv, lam_init):
    q_ref = refs[0]
    kv = refs[1:1 + 2 * n_kv]
    lam_ref, sub_ref, o_ref, kcat_ref, vt_ref = refs[1 + 2 * n_kv:]

    @pl.when(pl.program_id(2) == 0)
    def _():
        off = 0
        for t in range(n_kv):
            n = kv[2 * t].shape[0]
            kcat_ref[off:off + n, :] = kv[2 * t][...]
            vt_ref[0:DIF_DV, off:off + n] = kv[2 * t + 1][...].astype(F32).T.astype(BF16)
            off += n
        vt_ref[DIF_DV:, :] = jnp.ones((vt_ref.shape[0] - DIF_DV, vt_ref.shape[1]), BF16)

    lam = lam_ref[...]
    lam_full = (jnp.exp(jnp.sum(lam[0:1] * lam[1:2], -1, keepdims=True))
                - jnp.exp(jnp.sum(lam[2:3] * lam[3:4], -1, keepdims=True)) + lam_init)
    q = q_ref[...]
    lane = lax.broadcasted_iota(jnp.int32, q.shape, 1)
    st = []
    for m in range(2):
        qm = jnp.where((lane >= 64 * m) & (lane < 64 * (m + 1)), q, jnp.zeros_like(q))
        st.append(lax.dot_general(kcat_ref[...], qm, (((1,), (1,)), ((), ())), preferred_element_type=F32))
    outs = []
    for s in st:
        e = jnp.exp2(s - jnp.max(s, 0, keepdims=True)).astype(BF16)
        r = jnp.dot(vt_ref[...], e, preferred_element_type=F32)
        outs.append(r[:DIF_DV, :] / r[DIF_DV:DIF_DV + 1, :])
    ot = outs[0] - lam_full * outs[1]
    yt = ot * lax.rsqrt(jnp.mean(ot * ot, 0, keepdims=True) + LN_EPS)
    o_ref[...] = (yt.T * sub_ref[...] * (1.0 - lam_init)).astype(BF16)


def _dif_attention(qkv, lam, subln, layer_idx, side):
    lam_init = 0.8 - 0.6 * math.exp(-0.3 * layer_idx)
    tq = 1024
    nh = DIF_H
    kern = functools.partial(_dif_kernel, lam_init=lam_init)
    small = [pl.BlockSpec((4, 64), lambda *g: (0, 0)), pl.BlockSpec((1, DIF_DV), lambda *g: (0, 0))]
    sub = subln.reshape(1, DIF_DV)
    ctx0 = TL // L
    sem = _cp("arbitrary", "arbitrary", "arbitrary")
    ones_rows = 16
    scratch = lambda nk: [pltpu.VMEM((nk, DIF_DV), BF16), pltpu.VMEM((DIF_DV + ones_rows, nk), BF16)]
    (o_lat,), wb = _call_with_side(
        functools.partial(kern, n_kv=2),
        grid=(B, nh, S // tq),
        in_specs=[
            pl.BlockSpec((tq, DIF_DV), lambda b, h, i: (b * (S // tq) + i, h)),
            pl.BlockSpec((S, DIF_DV), lambda b, h, i: (b, nh + h)),
            pl.BlockSpec((S, DIF_DV), lambda b, h, i: (b, 2 * nh + h)),
            pl.BlockSpec((L, DIF_DV), lambda b, h, i: (ctx0 + b, nh + h)),
            pl.BlockSpec((L, DIF_DV), lambda b, h, i: (ctx0 + b, 2 * nh + h)),
        ] + small,
        out_specs=[pl.BlockSpec((tq, DIF_DV), lambda b, h, i: (b * (S // tq) + i, h))],
        out_shape=[jax.ShapeDtypeStruct((TL, D), BF16)],
        scratch_shapes=scratch(S + L),
        args=(qkv, qkv, qkv, qkv, qkv, lam, sub),
        side=side,
        name="dif_lat",
    )
    o_ctx = pl.pallas_call(
        functools.partial(kern, n_kv=1),
        grid=(B, nh, 1),
        in_specs=[
            pl.BlockSpec((L, DIF_DV), lambda b, h, i: (ctx0 + b, h)),
            pl.BlockSpec((L, DIF_DV), lambda b, h, i: (ctx0 + b, nh + h)),
            pl.BlockSpec((L, DIF_DV), lambda b, h, i: (ctx0 + b, 2 * nh + h)),
        ] + small,
        out_specs=pl.BlockSpec((L, DIF_DV), lambda b, h, i: (b, h)),
        out_shape=jax.ShapeDtypeStruct((TC, D), BF16),
        scratch_shapes=scratch(L),
        compiler_params=sem,
        name="dif_ctx",
    )(qkv, qkv, qkv, lam, sub)
    return (o_lat, o_ctx), wb


def _ret_kernel(ql_ref, qc_ref, kl_ref, kc_ref, vl_ref, vc_ref, gl_ref, gc_ref, lg_ref,
                ol_ref, oc_ref, rf_ref, rb_ref, il_ref, ic_ref):
    lg = lg_ref[0]
    lgf = lg[0:1, 0:1]
    lgb = lg[1:2, 0:1]

    def tables(C):
        row = lax.broadcasted_iota(jnp.int32, (C, C), 0).astype(F32)
        col = lax.broadcasted_iota(jnp.int32, (C, C), 1).astype(F32)
        rel = row - col
        pos = lax.broadcasted_iota(jnp.int32, (C, RET_DK), 0).astype(F32)
        return dict(
            decay=jnp.exp(jnp.abs(rel) * jnp.where(rel >= 0, lgf, lgb)),
            qdec_f=jnp.exp((pos + 1.0) * lgf),
            kdec_f=jnp.exp((C - 1.0 - pos) * lgf),
            qdec_b=jnp.exp((C - pos) * lgb),
            kdec_b=jnp.exp(pos * lgb),
            cdec_f=jnp.exp(C * lgf),
            cdec_b=jnp.exp(C * lgb),
        )

    tdot = lambda a, b: lax.dot_general(a, b, (((0,), (0,)), ((), ())), preferred_element_type=F32)

    def bwd_chunk(t, q_ref, k_ref, v_ref, i_ref, rows):
        q = q_ref[rows, :].astype(F32)
        k = k_ref[rows, :].astype(F32)
        v = v_ref[rows, :]
        i_ref[rows, :] = jnp.dot((q * t["qdec_b"]).astype(BF16), rb_ref[...].astype(BF16),
                                 preferred_element_type=F32)
        rb_ref[...] = t["cdec_b"] * rb_ref[...] + tdot((k * t["kdec_b"]).astype(BF16), v)

    def fwd_chunk(t, q_ref, k_ref, v_ref, g_ref, i_ref, o_ref, rows):
        qb = q_ref[rows, :]
        kb = k_ref[rows, :]
        v = v_ref[rows, :]
        q = qb.astype(F32)
        k = kb.astype(F32)
        s = lax.dot_general(qb, kb, (((1,), (1,)), ((), ())), preferred_element_type=F32) * t["decay"]
        o = jnp.dot(s.astype(BF16), v, preferred_element_type=F32)
        o = o + jnp.dot((q * t["qdec_f"]).astype(BF16), rf_ref[...].astype(BF16), preferred_element_type=F32)
        o = o + i_ref[rows, :]
        rf_ref[...] = t["cdec_f"] * rf_ref[...] + tdot((k * t["kdec_f"]).astype(BF16), v)
        o = o * lax.rsqrt(jnp.mean(o * o, -1, keepdims=True) + LN_EPS)
        g = g_ref[rows, :].astype(F32)
        o_ref[rows, :] = (g * jax.nn.sigmoid(g) * o).astype(BF16)

    rf_ref[...] = jnp.zeros_like(rf_ref)
    rb_ref[...] = jnp.zeros_like(rb_ref)
    tc = tables(L)
    tl = tables(RET_CH)
    lat_chunks = [pl.ds(n * RET_CH, RET_CH) for n in range(S // RET_CH)]
    all_ctx = pl.ds(0, L)

    bwd_chunk(tc, qc_ref, kc_ref, vc_ref, ic_ref, all_ctx)
    for rows in reversed(lat_chunks):
        bwd_chunk(tl, ql_ref, kl_ref, vl_ref, il_ref, rows)

    fwd_chunk(tc, qc_ref, kc_ref, vc_ref, gc_ref, ic_ref, oc_ref, all_ctx)
    for rows in lat_chunks:
        fwd_chunk(tl, ql_ref, kl_ref, vl_ref, gl_ref, il_ref, ol_ref, rows)


def _retention(p, side):
    log_gf = jnp.log1p(-jnp.exp2(-5.0 - jnp.arange(RET_H, dtype=F32)))
    lg = jnp.zeros((RET_H, 8, LANES), F32)
    lg = lg.at[:, 0, :].set(log_gf[:, None]).at[:, 1, :].set(log_gf[::-1][:, None])
    ctx0 = TL // L
    nk = D // RET_DK
    nv = 2 * D // RET_DV
    lat = lambda w, off: pl.BlockSpec((S, w), lambda b, h: (b, off + h))
    ctx = lambda w, off: pl.BlockSpec((L, w), lambda b, h: (ctx0 + b, off + h))
    return _call_with_side(
        _ret_kernel,
        grid=(B, RET_H),
        in_specs=[
            lat(RET_DK, 0), ctx(RET_DK, 0),
            lat(RET_DK, nk), ctx(RET_DK, nk),
            lat(RET_DV, nv), ctx(RET_DV, nv),
            lat(RET_DV, nv + RET_H), ctx(RET_DV, nv + RET_H),
            pl.BlockSpec((1, 8, LANES), lambda b, h: (h, 0, 0)),
        ],
        out_specs=[pl.BlockSpec((S, RET_DV), lambda b, h: (b, h)),
                   pl.BlockSpec((L, RET_DV), lambda b, h: (b, h))],
        out_shape=[jax.ShapeDtypeStruct((TL, 2 * D), BF16), jax.ShapeDtypeStruct((TC, 2 * D), BF16)],
        scratch_shapes=[
            pltpu.VMEM((RET_DK, RET_DV), F32), pltpu.VMEM((RET_DK, RET_DV), F32),
            pltpu.VMEM((S, RET_DV), F32), pltpu.VMEM((L, RET_DV), F32),
        ],
        args=(p, p, p, p, p, p, p, p, lg),
        side=side,
        name="retention",
    )


def _swa_kernel(q_ref, kp_ref, kc_ref, kn_ref, kx_ref, vp_ref, vc_ref, vn_ref, vx_ref, sink_ref, o_ref):
    n = pl.program_id(2)
    nb = pl.num_programs(2)
    k2 = jnp.concatenate([kp_ref[...], kc_ref[...], kn_ref[...], kx_ref[...]], axis=0)
    v2 = jnp.concatenate([vp_ref[...], vc_ref[...], vn_ref[...], vx_ref[...]], axis=0)
    nk = k2.shape[0]
    v2 = jnp.concatenate([v2, jnp.ones((nk, LANES), BF16)], axis=1)
    qi = lax.broadcasted_iota(jnp.int32, (QB, nk), 0)
    kj = lax.broadcasted_iota(jnp.int32, (QB, nk), 1)
    band = (kj >= qi) & (kj <= qi + 2 * WINDOW)
    band = band & ((kj >= WINDOW) | (n > 0)) & ((kj < WINDOW + QB) | (n < nb - 1))
    valid = band | (kj >= 2 * WINDOW + QB)
    bias = jnp.where(valid, 0.0, NEG_INF).astype(F32)
    bias = jnp.concatenate([bias] * SWA_G, axis=0)
    lane = lax.broadcasted_iota(jnp.int32, (QB, LANES), 1)
    lo = lane < SWA_DH
    qcols = [q_ref[:, c * LANES:(c + 1) * LANES].astype(F32) for c in range(4)]
    qswap = [pltpu.roll(x, SWA_DH, 1) for x in qcols]
    out_cols = [None] * 4
    heads_of = lambda t: [(2 * t + g // 2, g % 2) for g in range(SWA_G)]
    scores = []
    for t in range(2):
        keep = lo if t == 0 else ~lo
        qs = jnp.concatenate(
            [jnp.where(keep, qcols[c] if hf == t else qswap[c], 0.0).astype(BF16) for c, hf in heads_of(t)],
            axis=0)
        scores.append(lax.dot_general(qs, k2, (((1,), (1,)), ((), ())), preferred_element_type=F32) + bias)
    for t, s in enumerate(scores):
        heads = heads_of(t)
        sink = jnp.concatenate(
            [jnp.broadcast_to(sink_ref[2 * c + hf][:, 0:1] * LOG2E, (QB, 1)) for c, hf in heads], axis=0)
        mx = jnp.maximum(jnp.max(s, -1, keepdims=True), sink)
        e = jnp.exp2(s - mx).astype(BF16)
        r = jnp.dot(e, v2, preferred_element_type=F32)
        o = r[:, :LANES] / (r[:, LANES:] + jnp.exp2(sink - mx))
        for g, (c, hf) in enumerate(heads):
            og = o[g * QB:(g + 1) * QB, :]
            if hf != t:
                og = pltpu.roll(og, SWA_DH, 1)
            og = jnp.where(lo if hf == 0 else ~lo, og, 0.0)
            out_cols[c] = og if out_cols[c] is None else out_cols[c] + og
    for c in range(4):
        o_ref[:, c * LANES:(c + 1) * LANES] = out_cols[c].astype(BF16)


def _swa_attention(qkv, sink, side):
    nqb = S // QB
    kcol = SWA_H * SWA_DH // LANES
    vcol = kcol + SWA_KV * SWA_DH // LANES
    ctx0 = TL // L
    sink_t = jnp.broadcast_to(sink.astype(F32)[:, None, None], (SWA_H, 1, LANES))

    per = QB // WINDOW
    nhb = S // WINDOW

    def halo(col0, after):
        def index_map(b, p, n):
            blk = (n + 1) * per if after else n * per - 1
            return (b * nhb + jnp.clip(blk, 0, nhb - 1), col0 + p)
        return pl.BlockSpec((WINDOW, LANES), index_map)

    cur = lambda col0: pl.BlockSpec((QB, LANES), lambda b, p, n: (b * nqb + n, col0 + p))
    ctx = lambda col0: pl.BlockSpec((L, LANES), lambda b, p, n: (ctx0 + b, col0 + p))
    (o_lat,), wb = _call_with_side(
        _swa_kernel,
        grid=(B, SWA_KV // 2, nqb),
        in_specs=[
            pl.BlockSpec((QB, 4 * LANES), lambda b, p, n: (b * nqb + n, p)),
            halo(kcol, False), cur(kcol), halo(kcol, True), ctx(kcol),
            halo(vcol, False), cur(vcol), halo(vcol, True), ctx(vcol),
            pl.BlockSpec((8, 1, LANES), lambda b, p, n: (p, 0, 0)),
        ],
        out_specs=[pl.BlockSpec((QB, 4 * LANES), lambda b, p, n: (b * nqb + n, p))],
        out_shape=[jax.ShapeDtypeStruct((TL, D), BF16)],
        scratch_shapes=[],
        args=(qkv, qkv, qkv, qkv, qkv, qkv, qkv, qkv, qkv, sink_t),
        side=side,
        name="swa",
    )
    return (o_lat, None), wb


@jax.jit
def _forward(x, c, ctx, c_ctx, ada_w, ada_b, ln_g, ln_b, mlp_w1, mlp_w2,
             lru_w_in, lru_conv_w, lru_conv_b, lru_gate_w, lru_gate_b, lru_lambda, lru_w_out,
             dif_w_qkv, dif_lambda, dif_subln, dif_w_out, ret_w_qkvg, ret_w_out,
             swa_w_qkv, swa_sink, swa_w_out):
    cvec = jnp.concatenate([c, c_ctx[None, :], jnp.zeros((NSEG - B - 1, D), F32)], 0)
    mods = _ada(cvec, ada_w, ada_b)
    tables = _rope_tables(TM)
    h, u = _modulate(x.reshape(TL, D), ctx.reshape(TC, D), mods, 0)
    w_outs = [lru_w_out[0], dif_w_out[0], ret_w_out[0], swa_w_out[0]]
    w1 = mlp_w1.reshape(DEPTH * D, D_FF)
    w2 = mlp_w2.reshape(DEPTH * D_FF, D)
    qscale = SWA_DH ** -0.5 * LOG2E
    for i in range(DEPTH):
        last = i == DEPTH - 1
        n_rows = TL if last else T
        side = ((w1, i, D), (w2, i, D_FF), (w_outs[i], 0, w_outs[i].shape[0]))
        if i == 0:
            ui = jnp.concatenate([_interleave(u[:TL], S), _interleave(u[TL:], L)], 0)
            p = _proj(ui, lru_w_in[0], 1024)
            (o, oc), wb = _lru(p, lru_conv_w[0], lru_conv_b[0], lru_gate_w[0], lru_gate_b[0], lru_lambda[0], side)
            o, oc = _deinterleave(o, S), _deinterleave(oc, L)
        elif i == 1:
            p = _proj(u, dif_w_qkv[0], 1024, rope=(D, 2 * D, qscale, tables))
            (o, oc), wb = _dif_attention(p, dif_lambda[0], dif_subln[0], i, side)
        elif i == 2:
            p = _proj(u, ret_w_qkvg[0], 1024, qscale_cols=(D, RET_DK ** -0.5))
            (o, oc), wb = _retention(p, side)
        else:
            p = _proj(u, swa_w_qkv[0], 512, rope=(SWA_H * SWA_DH, (SWA_H + SWA_KV) * SWA_DH, qscale, tables))
            (o, oc), wb = _swa_attention(p, swa_sink[0], side)
        w1b, w2b, wob = wb
        h, u2 = _outln(o, oc, wob, h, mods, i, ln_g[i, 0:1], ln_b[i, 0:1])
        h, u = _mlp(u2, w1b, w2b, h, mods, i, ln_g[i, 1:2], ln_b[i, 1:2], n_rows)
    return h.reshape(B, S, D)


def kernel(x, c, ctx, c_ctx, ada_w, ada_b, ln_g, ln_b, mlp_w1, mlp_w2, lru_w_in, lru_conv_w, lru_conv_b,
           lru_gate_w, lru_gate_b, lru_lambda, lru_w_out, dif_w_qkv, dif_lambda, dif_subln, dif_w_out,
           ret_w_qkvg, ret_w_out, swa_w_qkv, swa_sink, swa_w_out):
    return _forward(x, c, ctx, c_ctx, ada_w, ada_b, ln_g, ln_b, mlp_w1, mlp_w2,
                    lru_w_in, lru_conv_w, lru_conv_b, lru_gate_w, lru_gate_b, lru_lambda, lru_w_out,
                    dif_w_qkv, dif_lambda, dif_subln, dif_w_out, ret_w_qkvg, ret_w_out,
                    swa_w_qkv, swa_sink, swa_w_out)
```
